```python
import jax, jax.numpy as jnp
from jax import lax
import numpy as np

D_MODEL = 2048
BATCH = 8
SEQ = 4096
DEPTH = 1
DEC_BATCH = 16
DEC_SEQ = 2048
PAST_LEN = 128

HEAD_DIM = 128
N_ATTN_HEADS = 8
N_RET_HEADS = 8
D_ATTN = N_ATTN_HEADS * HEAD_DIM
D_RET = N_RET_HEADS * HEAD_DIM
D_MIX = D_ATTN + D_RET
D_IN = 3 * D_ATTN + 4 * D_RET
DILATED_PATTERNS = ((128, 1), (512, 4), (2048, 16))
ROPE_THETA = 10000.0
RET_CHUNK = 128
MEM_TOKENS = 256
X_HEADS = 4
X_HEAD_DIM = 128
D_X = X_HEADS * X_HEAD_DIM
N_GROUPS = 4
EXP_PER_GROUP = 8
N_EXPERTS = N_GROUPS * EXP_PER_GROUP
TOP_K = 2
D_EXPERT = D_MODEL // 4
MOE_BLOCK = 128
DN_ALPHA = (2 * DEPTH) ** 0.25
DN_BETA = (8 * DEPTH) ** -0.25
LN_EPS = 1e-5
NEG_INF = -1e30

kernel_name = 'hybrid_dilated_retention_hmoe_encoder'


def layer_norm(x, g, b):
    xf = x.astype(jnp.float32)
    mu = jnp.mean(xf, axis=-1, keepdims=True)
    var = jnp.mean(jnp.square(xf - mu), axis=-1, keepdims=True)
    return ((xf - mu) * lax.rsqrt(var + LN_EPS) * g + b).astype(x.dtype)


def rotary(t, pos):
    half = t.shape[-1] // 2
    inv_freq = ROPE_THETA ** (-jnp.arange(half, dtype=jnp.float32) / half)
    ang = pos[:, None] * inv_freq[None, :]
    cos = jnp.cos(ang).astype(t.dtype)
    sin = jnp.sin(ang).astype(t.dtype)
    t1, t2 = t[..., :half], t[..., half:]
    return jnp.concatenate([t1 * cos - t2 * sin, t1 * sin + t2 * cos], axis=-1)


def dilated_branch(q, k, v, window, dil):
    B, H, S, Dh = q.shape
    L = S // dil
    hw = window // (2 * dil)
    C = hw
    nb = -(-L // C)
    Lp = nb * C

    def strided(t):
        return t.reshape(B, H, L, dil, Dh).transpose(0, 1, 3, 2, 4)

    qs = jnp.pad(strided(q), ((0, 0), (0, 0), (0, 0), (0, Lp - L), (0, 0))).reshape(B, H, dil, nb, C, Dh)

    def banded(t):
        tp = jnp.pad(strided(t), ((0, 0), (0, 0), (0, 0), (C, Lp - L + C), (0, 0))).reshape(B, H, dil, nb + 2, C, Dh)
        return jnp.concatenate([tp[:, :, :, :-2], tp[:, :, :, 1:-1], tp[:, :, :, 2:]], axis=4)

    kb, vb = banded(k), banded(v)
    s = jnp.einsum('bhrnqd,bhrnkd->bhrnqk', qs, kb, preferred_element_type=jnp.float32) * (Dh ** -0.5)
    blk = jnp.arange(nb)[:, None, None]
    qpos = blk * C + jnp.arange(C)[None, :, None]
    kpos = (blk - 1) * C + jnp.arange(3 * C)[None, None, :]
    valid = (jnp.abs(qpos - kpos) <= hw) & (kpos >= 0) & (kpos < L)
    s = jnp.where(valid, s, NEG_INF)
    lse = jax.nn.logsumexp(s, axis=-1)
    p = jnp.exp(s - lse[..., None]).astype(v.dtype)
    o = jnp.einsum('bhrnqk,bhrnkd->bhrnqd', p, vb)

    def unstrided(t):
        t = t.reshape((B, H, dil, Lp) + t.shape[5:])[:, :, :, :L]
        t = jnp.moveaxis(t, 2, 3)
        return t.reshape((B, H, S) + t.shape[4:])

    return unstrided(o), unstrided(lse)


def dilated_mixture(q, k, v):
    outs, lses = zip(*[dilated_branch(q, k, v, w, d) for (w, d) in DILATED_PATTERNS])
    wts = jax.nn.softmax(jnp.stack(lses), axis=0)
    o = jnp.sum(wts[..., None] * jnp.stack(outs).astype(jnp.float32), axis=0)
    return o.astype(q.dtype)


def retention_scan(q, k, v, gamma, strict):
    B, H, S, Dh = q.shape
    C = RET_CHUNK
    N = S // C
    log_g = jnp.log(gamma)[:, None]
    idx = jnp.arange(C, dtype=jnp.float32)
    diff = idx[:, None] - idx[None, :]
    causal = (diff > 0) if strict else (diff >= 0)
    decay = jnp.where(causal[None], jnp.exp(jnp.maximum(diff, 0.0)[None] * log_g[:, :, None]), 0.0)
    qc = q.reshape(B, H, N, C, Dh)
    kc = k.reshape(B, H, N, C, Dh)
    vc = v.reshape(B, H, N, C, Dh)
    inner = jnp.einsum('bhncd,bhnmd->bhncm', qc, kc) * decay[None, :, None]
    o_inner = jnp.einsum('bhncm,bhnme->bhnce', inner, vc)
    zeta = jnp.exp((C - 1 - idx)[None] * log_g)
    xi = jnp.exp((idx + 1)[None] * log_g)
    kv = jnp.einsum('bhncd,bhnce->nbhde', kc * zeta[None, :, None, :, None], vc)
    chunk_decay = jnp.exp(C * log_g)[None, :, :, None]

    def step(state, kv_n):
        return chunk_decay * state + kv_n, state

    _, prior = lax.scan(step, jnp.zeros((B, H, Dh, Dh), jnp.float32), kv)
    o_cross = jnp.einsum('bhncd,nbhde->bhnce', qc, prior) * xi[None, :, None, :, None]
    return (o_inner + o_cross).reshape(B, H, S, Dh)


def bidirectional_retention(q, k, v):
    h = jnp.arange(N_RET_HEADS, dtype=jnp.float32)
    gamma_f = 1.0 - 2.0 ** (-5.0 - h)
    gamma_b = 1.0 - 2.0 ** (-5.5 - h)
    q, k, v = q.astype(jnp.float32), k.astype(jnp.float32), v.astype(jnp.float32)
    fwd = retention_scan(q, k, v, gamma_f, False)
    flip = lambda t: jnp.flip(t, axis=2)
    bwd = flip(retention_scan(flip(q), flip(k), flip(v), gamma_b, True))
    return fwd + bwd


def hybrid_mixer(x, w_in, ret_gn_g, w_out):
    B, S, _ = x.shape
    h = x @ w_in
    sp = [D_ATTN, 2 * D_ATTN, 3 * D_ATTN, 3 * D_ATTN + D_RET, 3 * D_ATTN + 2 * D_RET, 3 * D_ATTN + 3 * D_RET]
    qa, ka, va, qr, kr, vr, gr = jnp.split(h, sp, axis=-1)
    heads = lambda t, n: t.reshape(B, S, n, HEAD_DIM).transpose(0, 2, 1, 3)
    pos = jnp.arange(S, dtype=jnp.float32)
    qa = rotary(heads(qa, N_ATTN_HEADS), pos)
    ka = rotary(heads(ka, N_ATTN_HEADS), pos)
    attn_o = dilated_mixture(qa, ka, heads(va, N_ATTN_HEADS))
    attn_o = attn_o.transpose(0, 2, 1, 3).reshape(B, S, D_ATTN)
    qr = rotary(heads(qr, N_RET_HEADS), pos)
    kr = rotary(heads(kr, N_RET_HEADS), pos) * (HEAD_DIM ** -0.5)
    r = bidirectional_retention(qr, kr, heads(vr, N_RET_HEADS))
    mu = jnp.mean(r, axis=-1, keepdims=True)
    var = jnp.mean(jnp.square(r - mu), axis=-1, keepdims=True)
    rn = ((r - mu) * lax.rsqrt(var + LN_EPS)).transpose(0, 2, 1, 3).reshape(B, S, D_RET) * ret_gn_g
    ret_o = (jax.nn.silu(gr.astype(jnp.float32)) * rn).astype(x.dtype)
    return jnp.concatenate([attn_o, ret_o], axis=-1) @ w_out


def memory_cross_attention(x, mem, w_mq, w_mkv, w_mo):
    B, S, _ = x.shape
    M = mem.shape[1]
    q = (x @ w_mq).reshape(B, S, X_HEADS, X_HEAD_DIM)
    kv = (mem @ w_mkv).reshape(B, M, 2, X_HEADS, X_HEAD_DIM)
    k, v = kv[:, :, 0], kv[:, :, 1]
    s = jnp.einsum('bshd,bmhd->bhsm', q, k, preferred_element_type=jnp.float32) * (X_HEAD_DIM ** -0.5)
    p = jax.nn.softmax(s, axis=-1).astype(v.dtype)
    o = jnp.einsum('bhsm,bmhd->bshd', p, v).reshape(B, S, D_X)
    return o @ w_mo


def hierarchical_moe(x, w_gr, b_gr, w_er, b_er, w1, w3, w2):
    B, S, D = x.shape
    T = B * S
    xf = x.reshape(T, D)
    g_logits = (xf @ w_gr).astype(jnp.float32) + b_gr
    g_prob = jax.nn.softmax(g_logits, axis=-1)
    g_idx = jnp.argmax(g_logits, axis=-1).astype(jnp.int32)
    p_g = jnp.take_along_axis(g_prob, g_idx[:, None], axis=1)[:, 0]
    e_logits = ((xf @ w_er).astype(jnp.float32) + b_er).reshape(T, N_GROUPS, EXP_PER_GROUP)
    e_logits = jnp.take_along_axis(e_logits, g_idx[:, None, None], axis=1)[:, 0]
    top_v, top_i = lax.top_k(e_logits, TOP_K)
    gate = (jax.nn.softmax(top_v, axis=-1) * p_g[:, None]).reshape(-1)
    eid = (g_idx[:, None] * EXP_PER_GROUP + top_i).reshape(-1).astype(jnp.int32)
    tok = jnp.repeat(jnp.arange(T, dtype=jnp.int32), TOP_K)
    M = T * TOP_K
    order = jnp.argsort(eid)
    e_s, t_s, g_s = eid[order], tok[order], gate[order]
    counts = jnp.bincount(eid, length=N_EXPERTS)
    start = jnp.cumsum(counts) - counts
    padded = (counts + MOE_BLOCK - 1) // MOE_BLOCK * MOE_BLOCK
    pend = jnp.cumsum(padded)
    pstart = pend - padded
    dest = pstart[e_s] + (jnp.arange(M, dtype=jnp.int32) - start[e_s])
    nblk = -(-M // MOE_BLOCK) + N_EXPERTS
    P = nblk * MOE_BLOCK
    buf_tok = jnp.full((P,), T, jnp.int32).at[dest].set(t_s)
    buf_gate = jnp.zeros((P,), jnp.float32).at[dest].set(g_s)
    blk_e = jnp.minimum(jnp.searchsorted(pend, jnp.arange(nblk) * MOE_BLOCK, side='right'), N_EXPERTS - 1).astype(jnp.int32)
    xpad = jnp.concatenate([xf, jnp.zeros((1, D), xf.dtype)], axis=0)
    xb = xpad[buf_tok].reshape(nblk, MOE_BLOCK, D)

    def expert_block(args):
        xblk, e = args
        hid = jax.nn.silu(xblk @ w1[e]) * (xblk @ w3[e])
        return hid @ w2[e]

    yb = lax.map(expert_block, (xb, blk_e)).reshape(P, D)
    yb = yb * buf_gate[:, None].astype(yb.dtype)
    out = jnp.zeros((T + 1, D), yb.dtype).at[buf_tok].add(yb)[:T]
    return out.reshape(B, S, D).astype(x.dtype)


def encoder_layer(x, mem, w_in, ret_gn_g, w_out, ln1_g, ln1_b, w_mq, w_mkv, w_mo, ln2_g, ln2_b,
                  w_gr, b_gr, w_er, b_er, w1, w3, w2, ln3_g, ln3_b):
    x = layer_norm(DN_ALPHA * x + hybrid_mixer(x, w_in, ret_gn_g, w_out), ln1_g, ln1_b)
    x = layer_norm(DN_ALPHA * x + memory_cross_attention(x, mem, w_mq, w_mkv, w_mo), ln2_g, ln2_b)
    x = layer_norm(DN_ALPHA * x + hierarchical_moe(x, w_gr, b_gr, w_er, b_er, w1, w3, w2), ln3_g, ln3_b)
    return x


def setup_inputs(seed: int = 0) -> dict:
    key = jax.random.key(seed)
    ks = jax.random.split(key, 24)
    nrm = lambda k, shape, scale: jax.random.normal(k, shape, jnp.float32) * scale
    L = DEPTH
    return {
        'x_prompt': nrm(ks[0], (BATCH, SEQ, D_MODEL), 1.0),
        'x_sample': nrm(ks[1], (DEC_BATCH, DEC_SEQ, D_MODEL), 1.0),
        'mem_prompt': nrm(ks[2], (BATCH, MEM_TOKENS, D_MODEL), 1.0),
        'mem_sample': nrm(ks[3], (DEC_BATCH, MEM_TOKENS, D_MODEL), 1.0),
        'w_in': nrm(ks[4], (L, D_MODEL, D_IN), D_MODEL ** -0.5),
        'ret_gn_g': 1.0 + nrm(ks[5], (L, D_RET), 0.02),
        'w_out': nrm(ks[6], (L, D_MIX, D_MODEL), DN_BETA * D_MIX ** -0.5),
        'ln1_g': 1.0 + nrm(ks[7], (L, D_MODEL), 0.02),
        'ln1_b': nrm(ks[8], (L, D_MODEL), 0.02),
        'w_mq': nrm(ks[9], (L, D_MODEL, D_X), D_MODEL ** -0.5),
        'w_mkv': nrm(ks[10], (L, D_MODEL, 2 * D_X), D_MODEL ** -0.5),
        'w_mo': nrm(ks[11], (L, D_X, D_MODEL), DN_BETA * D_X ** -0.5),
        'ln2_g': 1.0 + nrm(ks[12], (L, D_MODEL), 0.02),
        'ln2_b': nrm(ks[13], (L, D_MODEL), 0.02),
        'w_gr': nrm(ks[14], (L, D_MODEL, N_GROUPS), D_MODEL ** -0.5),
        'b_gr': nrm(ks[15], (L, N_GROUPS), 0.01),
        'w_er': nrm(ks[16], (L, D_MODEL, N_EXPERTS), D_MODEL ** -0.5),
        'b_er': nrm(ks[17], (L, N_EXPERTS), 0.01),
        'w1': nrm(ks[18], (L, N_EXPERTS, D_MODEL, D_EXPERT), D_MODEL ** -0.5),
        'w3': nrm(ks[19], (L, N_EXPERTS, D_MODEL, D_EXPERT), D_MODEL ** -0.5),
        'w2': nrm(ks[20], (L, N_EXPERTS, D_EXPERT, D_MODEL), DN_BETA * D_EXPERT ** -0.5),
        'ln3_g': 1.0 + nrm(ks[21], (L, D_MODEL), 0.02),
        'ln3_b': nrm(ks[22], (L, D_MODEL), 0.02),
    }


def reference(x_prompt, x_sample, mem_prompt, mem_sample, w_in, ret_gn_g, w_out, ln1_g, ln1_b,
              w_mq, w_mkv, w_mo, ln2_g, ln2_b, w_gr, b_gr, w_er, b_er, w1, w3, w2, ln3_g, ln3_b):
    y_prompt, y_sample = x_prompt, x_sample
    for l in range(DEPTH):
        lw = (w_in[l], ret_gn_g[l], w_out[l], ln1_g[l], ln1_b[l], w_mq[l], w_mkv[l], w_mo[l],
              ln2_g[l], ln2_b[l], w_gr[l], b_gr[l], w_er[l], b_er[l], w1[l], w3[l], w2[l],
              ln3_g[l], ln3_b[l])
        y_prompt = encoder_layer(y_prompt, mem_prompt, *lw)
        y_sample = encoder_layer(y_sample, mem_sample, *lw)
    return (y_prompt, y_sample)
```

```python
import functools

import numpy as np
import jax
import jax.numpy as jnp
from jax import lax
from jax.experimental import pallas as pl
from jax.experimental.pallas import tpu as pltpu

F32 = jnp.float32
BF16 = jnp.bfloat16

D_MODEL = 2048
HEAD_DIM = 128
N_HEADS = 8
SEG = N_HEADS * HEAD_DIM
N_SEG = 7
SEG_QA, SEG_KA, SEG_VA, SEG_QR, SEG_KR, SEG_VR, SEG_GR = range(7)
DILATIONS = (1, 4, 16)
HALF_WIN = 64
ROPE_THETA = 10000.0
RET_CHUNK = 128
MEM_TOKENS = 256
X_HEADS = 4
D_X = X_HEADS * HEAD_DIM
N_GROUPS = 4
EXP_PER_GROUP = 8
N_EXPERTS = N_GROUPS * EXP_PER_GROUP
D_EXPERT = D_MODEL // 4
LN_EPS = 1e-5
NEG_INF = -1e30
ATTN_SCALE = HEAD_DIM ** -0.5

LANES = 128
VMEM_LIMIT = 52 * 1024 * 1024

IN_PROJ_ROWS = 512
ATTN_ROWS = 512
RET_ROWS = 512
ROW_TILE = 256
MOE_ROWS = 256


def _params(*sem):
    return pltpu.CompilerParams(dimension_semantics=sem, vmem_limit_bytes=VMEM_LIMIT)


def _layer_norm(z, g, b):
    mu = jnp.mean(z, axis=-1, keepdims=True)
    d = z - mu
    var = jnp.mean(d * d, axis=-1, keepdims=True)
    return d * lax.rsqrt(var + LN_EPS) * g + b


def _dot_nt(a, b):
    return lax.dot_general(a, b, (((1,), (1,)), ((), ())), preferred_element_type=F32)


def _dot_tn(a, b):
    return lax.dot_general(a, b, (((0,), (0,)), ((), ())), preferred_element_type=F32)


def _dot(a, b):
    return jnp.dot(a, b, preferred_element_type=F32)


def _in_proj_kernel(x_ref, w_ref, cos_ref, sin_ref, o_ref, xb_ref):
    j = pl.program_id(1)

    @pl.when(j == 0)
    def _():
        xb_ref[...] = x_ref[...].astype(BF16)

    acc = _dot(xb_ref[...], w_ref[...])
    is_rot = (j == SEG_QA) | (j == SEG_KA) | (j == SEG_QR) | (j == SEG_KR)

    @pl.when(is_rot)
    def _():
        scale = jnp.where(j == SEG_KR, ATTN_SCALE, 1.0).astype(F32)
        c = cos_ref[...] * scale
        s = sin_ref[...] * scale
        for h in range(N_HEADS):
            cs = slice(h * HEAD_DIM, (h + 1) * HEAD_DIM)
            t = acc[:, cs]
            o_ref[:, cs] = (t * c + pltpu.roll(t, HEAD_DIM // 2, 1) * s).astype(BF16)

    @pl.when(jnp.logical_not(is_rot))
    def _():
        o_ref[...] = acc.astype(BF16)


def _in_proj(x2d, w_in, cos, sin, seq):
    t = x2d.shape[0]
    tm = IN_PROJ_ROWS
    n_s = seq // tm
    return pl.pallas_call(
        _in_proj_kernel,
        grid=(t // tm, N_SEG),
        in_specs=[
            pl.BlockSpec((tm, D_MODEL), lambda i, j: (i, 0)),
            pl.BlockSpec((D_MODEL, SEG), lambda i, j: (0, j)),
            pl.BlockSpec((tm, HEAD_DIM), lambda i, j: (i % n_s, 0)),
            pl.BlockSpec((tm, HEAD_DIM), lambda i, j: (i % n_s, 0)),
        ],
        out_specs=pl.BlockSpec((tm, SEG), lambda i, j: (i, j)),
        out_shape=jax.ShapeDtypeStruct((t, N_SEG * SEG), BF16),
        scratch_shapes=[pltpu.VMEM((tm, D_MODEL), BF16)],
        compiler_params=_params("arbitrary", "arbitrary"),
        name="in_proj",
    )(x2d, w_in, cos, sin)


def _local_attn_kernel(q_ref, kp_ref, kc_ref, kn_ref, vp_ref, vc_ref, vn_ref,
                       o_ref, lse_ref, kbuf, vbuf, *, lt, sub_len):
    i = pl.program_id(2)
    hw = HALF_WIN
    kbuf[0:hw, :] = kp_ref[...]
    kbuf[hw:hw + lt, :] = kc_ref[...]
    kbuf[hw + lt:2 * hw + lt, :] = kn_ref[...]
    vbuf[0:hw, :] = vp_ref[...]
    vbuf[hw:hw + lt, :] = vc_ref[...]
    vbuf[hw + lt:2 * hw + lt, :] = vn_ref[...]

    qb = 128
    kb = qb + 2 * hw
    row = lax.broadcasted_iota(jnp.int32, (qb, kb), 0)
    col = lax.broadcasted_iota(jnp.int32, (qb, kb), 1)
    band = jnp.abs(row + hw - col) <= hw
    lane = lax.broadcasted_iota(jnp.int32, (qb, LANES), 1)

    def body(j, carry):
        r0 = pl.multiple_of(j * qb, qb)
        kpos = i * lt + r0 - hw + col
        bias = jnp.where(band, 0.0, NEG_INF).astype(F32)
        bias = jnp.where(kpos >= 0, bias, NEG_INF)
        bias = jnp.where(kpos < sub_len, bias, NEG_INF)
        lse_tile = jnp.zeros((qb, LANES), F32)
        for h in range(N_HEADS):
            cs = slice(h * HEAD_DIM, (h + 1) * HEAD_DIM)
            q = q_ref[pl.ds(r0, qb), cs]
            k = kbuf[pl.ds(r0, kb), cs]
            v = vbuf[pl.ds(r0, kb), cs]
            s = _dot_nt(q, k) * ATTN_SCALE + bias
            m = jnp.max(s, axis=-1, keepdims=True)
            p = jnp.exp(s - m)
            l = jnp.sum(p, axis=-1, keepdims=True)
            o = _dot(p.astype(BF16), v) / l
            o_ref[pl.ds(r0, qb), cs] = o.astype(BF16)
            lse_tile = jnp.where(lane == h, m + jnp.log(l), lse_tile)
        lse_ref[pl.ds(r0, qb), :] = lse_tile
        return carry

    lax.fori_loop(0, lt // qb, body, 0)


def _local_attn(h3, batch, seq, dil):
    sub_len = seq // dil
    lt = min(sub_len, ATTN_ROWS)
    hw = HALF_WIN
    n_halo = sub_len // hw
    per = lt // hw

    def main(seg):
        return pl.BlockSpec((None, lt, SEG), lambda b, r, i: (b, i, r * N_SEG + seg))

    def prev(seg):
        return pl.BlockSpec((None, hw, SEG),
                            lambda b, r, i: (b, jnp.maximum(i * per - 1, 0), r * N_SEG + seg))

    def nxt(seg):
        return pl.BlockSpec((None, hw, SEG),
                            lambda b, r, i: (b, jnp.minimum((i + 1) * per, n_halo - 1), r * N_SEG + seg))

    kern = functools.partial(_local_attn_kernel, lt=lt, sub_len=sub_len)
    return pl.pallas_call(
        kern,
        grid=(batch, dil, sub_len // lt),
        in_specs=[main(SEG_QA), prev(SEG_KA), main(SEG_KA), nxt(SEG_KA),
                  prev(SEG_VA), main(SEG_VA), nxt(SEG_VA)],
        out_specs=[pl.BlockSpec((None, lt, SEG), lambda b, r, i: (b, i, r)),
                   pl.BlockSpec((None, lt, LANES), lambda b, r, i: (b, i, r))],
        out_shape=[jax.ShapeDtypeStruct((batch, sub_len, dil * SEG), BF16),
                   jax.ShapeDtypeStruct((batch, sub_len, dil * LANES), F32)],
        scratch_shapes=[pltpu.VMEM((lt + 2 * hw, SEG), BF16),
                        pltpu.VMEM((lt + 2 * hw, SEG), BF16)],
        compiler_params=_params("arbitrary", "arbitrary", "arbitrary"),
        name="local_attn_d%d" % dil,
    )(h3, h3, h3, h3, h3, h3, h3)


def _retention_tables():
    h = np.arange(N_HEADS, dtype=np.float64)
    gf = 1.0 - 2.0 ** (-5.0 - h)
    gb = 1.0 - 2.0 ** (-5.5 - h)
    c = RET_CHUNK
    idx = np.arange(c, dtype=np.float64)
    diff = idx[:, None] - idx[None, :]
    dec_f = np.where(diff >= 0, gf[:, None, None] ** np.maximum(diff, 0.0), 0.0)
    dec_b = np.where(diff < 0, gb[:, None, None] ** np.maximum(-diff, 0.0), 0.0)
    decay = dec_f + dec_b
    rows = lambda t: np.broadcast_to(t[:, :, None], (N_HEADS, c, HEAD_DIM))
    xi_f = rows(gf[:, None] ** (idx + 1.0)[None])
    zeta_f = rows(gf[:, None] ** (c - 1.0 - idx)[None])
    xi_b = rows(gb[:, None] ** (c - idx)[None])
    zeta_b = rows(gb[:, None] ** idx[None])
    f = lambda t: jnp.asarray(np.ascontiguousarray(t), F32)
    return (f(decay), f(xi_f), f(zeta_f), f(xi_b), f(zeta_b),
            tuple(float(g ** c) for g in gf), tuple(float(g ** c) for g in gb))


def _ret_fwd_kernel(q_ref, k_ref, v_ref, dec_ref, xi_ref, zeta_ref, o_ref, state, *, rt, cd):
    @pl.when(pl.program_id(1) == 0)
    def _():
        state[...] = jnp.zeros_like(state)

    for c in range(rt // RET_CHUNK):
        rs = slice(c * RET_CHUNK, (c + 1) * RET_CHUNK)
        for h in range(N_HEADS):
            cs = slice(h * HEAD_DIM, (h + 1) * HEAD_DIM)
            q = q_ref[rs, cs]
            k = k_ref[rs, cs]
            v = v_ref[rs, cs]
            a = _dot_nt(q, k) * dec_ref[h]
            o = _dot(a.astype(BF16), v)
            s_old = state[h]
            o = o + _dot(q, s_old.astype(BF16)) * xi_ref[h]
            kz = (k.astype(F32) * zeta_ref[h]).astype(BF16)
            state[h] = cd[h] * s_old + _dot_tn(kz, v)
            o_ref[rs, cs] = o


def _ret_bwd_kernel(q_ref, k_ref, v_ref, g_ref, r_ref, xi_ref, zeta_ref, gn_ref, o_ref, state,
                    *, rt, cd):
    @pl.when(pl.program_id(1) == 0)
    def _():
        state[...] = jnp.zeros_like(state)

    for c in reversed(range(rt // RET_CHUNK)):
        rs = slice(c * RET_CHUNK, (c + 1) * RET_CHUNK)
        for h in range(N_HEADS):
            cs = slice(h * HEAD_DIM, (h + 1) * HEAD_DIM)
            q = q_ref[rs, cs]
            k = k_ref[rs, cs]
            v = v_ref[rs, cs]
            s_old = state[h]
            r = r_ref[rs, cs] + _dot(q, s_old.astype(BF16)) * xi_ref[h]
            kz = (k.astype(F32) * zeta_ref[h]).astype(BF16)
            state[h] = cd[h] * s_old + _dot_tn(kz, v)
            mu = jnp.mean(r, axis=-1, keepdims=True)
            d = r - mu
            var = jnp.mean(d * d, axis=-1, keepdims=True)
            rn = d * lax.rsqrt(var + LN_EPS) * gn_ref[:, cs]
            g = g_ref[rs, cs].astype(F32)
            silu = g / (1.0 + jnp.exp(-g))
            o_ref[rs, cs] = (silu * rn).astype(BF16)


def _retention(h3, gn_g, batch, seq, tables):
    decay, xi_f, zeta_f, xi_b, zeta_b, cd_f, cd_b = tables
    rt = min(seq, RET_ROWS)
    nr = seq // rt
    tab = pl.BlockSpec((N_HEADS, RET_CHUNK, HEAD_DIM), lambda b, t: (0, 0, 0))
    state = pltpu.VMEM((N_HEADS, HEAD_DIM, HEAD_DIM), F32)

    fwd_seg = lambda seg: pl.BlockSpec((None, rt, SEG), lambda b, t: (b, t, seg))
    r_fwd = pl.pallas_call(
        functools.partial(_ret_fwd_kernel, rt=rt, cd=cd_f),
        grid=(batch, nr),
        in_specs=[fwd_seg(SEG_QR), fwd_seg(SEG_KR), fwd_seg(SEG_VR), tab, tab, tab],
        out_specs=pl.BlockSpec((None, rt, SEG), lambda b, t: (b, t, 0)),
        out_shape=jax.ShapeDtypeStruct((batch, seq, SEG), F32),
        scratch_shapes=[state],
        compiler_params=_params("arbitrary", "arbitrary"),
        name="retention_fwd",
    )(h3, h3, h3, decay, xi_f, zeta_f)

    bwd_seg = lambda seg: pl.BlockSpec((None, rt, SEG), lambda b, t: (b, nr - 1 - t, seg))
    return pl.pallas_call(
        functools.partial(_ret_bwd_kernel, rt=rt, cd=cd_b),
        grid=(batch, nr),
        in_specs=[bwd_seg(SEG_QR), bwd_seg(SEG_KR), bwd_seg(SEG_VR), bwd_seg(SEG_GR),
                  pl.BlockSpec((None, rt, SEG), lambda b, t: (b, nr - 1 - t, 0)),
                  tab, tab, pl.BlockSpec((1, SEG), lambda b, t: (0, 0))],
        out_specs=pl.BlockSpec((None, rt, SEG), lambda b, t: (b, nr - 1 - t, 0)),
        out_shape=jax.ShapeDtypeStruct((batch, seq, SEG), BF16),
        scratch_shapes=[state],
        compiler_params=_params("arbitrary", "arbitrary"),
        name="retention_bwd",
    )(h3, h3, h3, h3, r_fwd, xi_b, zeta_b, gn_g)


def _out_proj_kernel(o1_ref, o2_ref, o3_ref, l1_ref, l2_ref, l3_ref, ret_ref, x_ref, w_ref,
                     g_ref, b_ref, out_ref, attn_buf, *, alpha):
    la, lb, lc = l1_ref[...], l2_ref[...], l3_ref[...]
    m = jnp.maximum(jnp.maximum(la, lb), lc)
    ea, eb, ec = jnp.exp(la - m), jnp.exp(lb - m), jnp.exp(lc - m)
    inv = 1.0 / (ea + eb + ec)
    wa, wb, wc = ea * inv, eb * inv, ec * inv
    rows = la.shape[0]
    for h in range(N_HEADS):
        cs = slice(h * HEAD_DIM, (h + 1) * HEAD_DIM)
        bc = lambda w: jnp.broadcast_to(w[:, h:h + 1], (rows, HEAD_DIM))
        mix = (bc(wa) * o1_ref[:, cs].astype(F32) + bc(wb) * o2_ref[:, cs].astype(F32)
               + bc(wc) * o3_ref[:, cs].astype(F32))
        attn_buf[:, cs] = mix.astype(BF16)
    y = _dot(attn_buf[...], w_ref[0:SEG, :]) + _dot(ret_ref[...], w_ref[SEG:2 * SEG, :])
    out_ref[...] = _layer_norm(alpha * x_ref[...] + y, g_ref[...], b_ref[...])


def _out_proj(outs, lses, ret_o, x2d, w_out, ln_g, ln_b, alpha):
    t = x2d.shape[0]
    tm = ROW_TILE
    row = lambda w: pl.BlockSpec((tm, w), lambda i: (i, 0))
    full = lambda a: pl.BlockSpec(a.shape, lambda i: (0,) * a.ndim)
    return pl.pallas_call(
        functools.partial(_out_proj_kernel, alpha=alpha),
        grid=(t // tm,),
        in_specs=[row(SEG)] * 3 + [row(LANES)] * 3 + [row(SEG), row(D_MODEL),
                                                     full(w_out), full(ln_g), full(ln_b)],
        out_specs=row(D_MODEL),
        out_shape=jax.ShapeDtypeStruct((t, D_MODEL), F32),
        scratch_shapes=[pltpu.VMEM((tm, SEG), BF16)],
        compiler_params=_params("arbitrary"),
        name="out_proj_ln1",
    )(*outs, *lses, ret_o, x2d, w_out, ln_g, ln_b)


def _mem_kv_kernel(m_ref, w_ref, o_ref):
    o_ref[...] = _dot(m_ref[...].astype(BF16), w_ref[...]).astype(BF16)


def _mem_kv(mem2d, w_mkv):
    rows = mem2d.shape[0]
    tm = ROW_TILE
    return pl.pallas_call(
        _mem_kv_kernel,
        grid=(rows // tm,),
        in_specs=[pl.BlockSpec((tm, D_MODEL), lambda i: (i, 0)),
                  pl.BlockSpec(w_mkv.shape, lambda i: (0, 0))],
        out_specs=pl.BlockSpec((tm, 2 * D_X), lambda i: (i, 0)),
        out_shape=jax.ShapeDtypeStruct((rows, 2 * D_X), BF16),
        compiler_params=_params("arbitrary"),
        name="mem_kv",
    )(mem2d, w_mkv)


def _route(logits):
    shape = logits.shape
    lane_i = lax.broadcasted_iota(jnp.int32, shape, 1)
    lane = lane_i.astype(F32)
    lowest = jnp.float32(-3.0e38)
    none = jnp.float32(LANES)
    rmax = lambda t: jnp.max(t, axis=-1, keepdims=True)
    rmin = lambda t: jnp.min(t, axis=-1, keepdims=True)

    is_group = lane_i < N_GROUPS
    gl = jnp.where(is_group, logits, lowest)
    g_max = rmax(gl)
    g_idx = rmin(jnp.where(gl == g_max, lane, none))
    p_g = 1.0 / jnp.sum(jnp.where(is_group, jnp.exp(logits - g_max), 0.0), axis=-1, keepdims=True)

    e_lo = N_GROUPS + EXP_PER_GROUP * g_idx
    el = jnp.where(lane >= e_lo, logits, lowest)
    el = jnp.where(lane < e_lo + EXP_PER_GROUP, el, lowest)
    v1 = rmax(el)
    i1 = rmin(jnp.where(el == v1, lane, none))
    el2 = jnp.where(lane == i1, lowest, el)
    v2 = rmax(el2)
    i2 = rmin(jnp.where(el2 == v2, lane, none))
    e2 = jnp.exp(v2 - v1)
    g1 = p_g / (1.0 + e2)
    g2 = p_g * e2 / (1.0 + e2)
    out = jnp.where(lane_i == 0, i1 - N_GROUPS,
                    jnp.where(lane_i == 1, i2 - N_GROUPS,
                              jnp.where(lane_i == 2, g1, jnp.where(lane_i == 3, g2, 0.0))))
    return out.astype(F32)


def _xattn_kernel(x_ref, kv_ref, wq_ref, wo_ref, g_ref, b_ref, wrh_ref, wrl_ref, br_ref,
                  x2_ref, route_ref, obuf, *, alpha):
    x = x_ref[...]
    q = _dot(x.astype(BF16), wq_ref[...]).astype(BF16)
    for h in range(X_HEADS):
        cs = slice(h * HEAD_DIM, (h + 1) * HEAD_DIM)
        k = kv_ref[:, cs]
        v = kv_ref[:, D_X + h * HEAD_DIM:D_X + (h + 1) * HEAD_DIM]
        s = _dot_nt(q[:, cs], k) * ATTN_SCALE
        m = jnp.max(s, axis=-1, keepdims=True)
        p = jnp.exp(s - m)
        l = jnp.sum(p, axis=-1, keepdims=True)
        obuf[:, cs] = (_dot(p.astype(BF16), v) / l).astype(BF16)
    y = _dot(obuf[...], wo_ref[...])
    x2 = _layer_norm(alpha * x + y, g_ref[...], b_ref[...])
    x2_ref[...] = x2
    xh = x2.astype(BF16)
    xl = (x2 - xh.astype(F32)).astype(BF16)
    logits = (_dot(xh, wrh_ref[...]) + _dot(xl, wrh_ref[...]) + _dot(xh, wrl_ref[...])
              + br_ref[...])
    route_ref[...] = _route(logits)


def _xattn(x1, kv, w_mq, w_mo, ln_g, ln_b, wr_hi, wr_lo, b_r, seq, alpha):
    t = x1.shape[0]
    tm = ROW_TILE
    per_b = seq // tm
    row = lambda w: pl.BlockSpec((tm, w), lambda i: (i, 0))
    full = lambda a: pl.BlockSpec(a.shape, lambda i: (0,) * a.ndim)
    return pl.pallas_call(
        functools.partial(_xattn_kernel, alpha=alpha),
        grid=(t // tm,),
        in_specs=[row(D_MODEL),
                  pl.BlockSpec((MEM_TOKENS, 2 * D_X), lambda i: (i // per_b, 0)),
                  full(w_mq), full(w_mo), full(ln_g), full(ln_b),
                  full(wr_hi), full(wr_lo), full(b_r)],
        out_specs=[row(D_MODEL), row(LANES)],
        out_shape=[jax.ShapeDtypeStruct((t, D_MODEL), F32),
                   jax.ShapeDtypeStruct((t, LANES), F32)],
        scratch_shapes=[pltpu.VMEM((tm, D_X), BF16)],
        compiler_params=_params("arbitrary"),
        name="xattn_ln2_router",
    )(x1, kv, w_mq, w_mo, ln_g, ln_b, wr_hi, wr_lo, b_r)


def _moe_kernel(blk_e_ref, blk_nv_ref, asg_hbm, x_hbm, w13_ref, w2_ref, y_hbm,
                idx_smem, xbuf, ybuf, sem_i, sem_g, sem_s, *, nblk):
    del blk_e_ref
    b = pl.program_id(0)

    def idx_copy(blk):
        return pltpu.make_async_copy(asg_hbm.at[blk], idx_smem.at[blk % 3], sem_i.at[blk % 3])

    def gather_copy(blk, r, tok):
        s = blk % 2
        return pltpu.make_async_copy(x_hbm.at[pl.ds(tok, 1)], xbuf.at[s, pl.ds(r, 1)], sem_g.at[s])

    def scatter_copy(r, a):
        return pltpu.make_async_copy(ybuf.at[pl.ds(r, 1)], y_hbm.at[pl.ds(a, 1)], sem_s)

    def issue_gathers(blk):
        def body(r, c):
            gather_copy(blk, r, idx_smem[blk % 3, r] >> 1).start()
            return c
        lax.fori_loop(0, blk_nv_ref[blk], body, 0)

    @pl.when(b == 0)
    def _():
        xbuf[...] = jnp.zeros_like(xbuf)
        idx_copy(0).start()
        idx_copy(0).wait()
        issue_gathers(0)
        if nblk > 1:
            idx_copy(1).start()

    @pl.when(b + 1 < nblk)
    def _():
        idx_copy(b + 1).wait()
        issue_gathers(b + 1)

    @pl.when(b + 2 < nblk)
    def _():
        idx_copy(b + 2).start()

    nv = blk_nv_ref[b]

    @pl.when(nv > 0)
    def _():
        def wait_gather(r, c):
            gather_copy(b, 0, 0).wait()
            return c
        lax.fori_loop(0, nv, wait_gather, 0)

        xb = xbuf[b % 2].astype(BF16)
        h13 = _dot(xb, w13_ref[...])
        h1 = h13[:, :D_EXPERT]
        h3 = h13[:, D_EXPERT:]
        hid = (h1 / (1.0 + jnp.exp(-h1)) * h3).astype(BF16)
        ybuf[...] = _dot(hid, w2_ref[...])

        def issue_scatter(r, c):
            scatter_copy(r, idx_smem[b % 3, r]).start()
            return c
        lax.fori_loop(0, nv, issue_scatter, 0)

        def wait_scatter(r, c):
            scatter_copy(0, 0).wait()
            return c
        lax.fori_loop(0, nv, wait_scatter, 0)


def _moe_dispatch(route, bm):
    t = route.shape[0]
    m = 2 * t
    nblk = m // bm + N_EXPERTS
    eid = route[:, 0:2].astype(jnp.int32).reshape(m)
    counts = jnp.sum((eid[:, None] == jnp.arange(N_EXPERTS, dtype=jnp.int32)[None, :]).astype(jnp.int32),
                     axis=0)
    order = jnp.argsort(eid).astype(jnp.int32)
    start = jnp.cumsum(counts) - counts
    nb_e = (counts + bm - 1) // bm
    bend = jnp.cumsum(nb_e)
    bstart = bend - nb_e
    blk = jnp.arange(nblk, dtype=jnp.int32)
    blk_e = jnp.minimum(jnp.searchsorted(bend, blk, side='right'), N_EXPERTS - 1).astype(jnp.int32)
    within = blk - bstart[blk_e]
    blk_nv = jnp.where(blk < bend[-1], jnp.clip(counts[blk_e] - within * bm, 0, bm), 0).astype(jnp.int32)
    src = (start[blk_e] + within * bm)[:, None] + jnp.arange(bm, dtype=jnp.int32)[None, :]
    row_asg = order[jnp.clip(src, 0, m - 1)]
    return blk_e, blk_nv, row_asg.astype(jnp.int32), nblk


def _moe(x2, route, w13, w2):
    t = x2.shape[0]
    bm = MOE_ROWS
    blk_e, blk_nv, row_asg, nblk = _moe_dispatch(route, bm)
    grid_spec = pltpu.PrefetchScalarGridSpec(
        num_scalar_prefetch=2,
        grid=(nblk,),
        in_specs=[pl.BlockSpec(memory_space=pl.ANY),
                  pl.BlockSpec(memory_space=pl.ANY),
                  pl.BlockSpec((None, D_MODEL, 2 * D_EXPERT), lambda b, be, nv: (be[b], 0, 0)),
                  pl.BlockSpec((None, D_EXPERT, D_MODEL), lambda b, be, nv: (be[b], 0, 0))],
        out_specs=pl.BlockSpec(memory_space=pl.ANY),
        scratch_shapes=[pltpu.SMEM((3, bm), jnp.int32),
                        pltpu.VMEM((2, bm, D_MODEL), F32),
                        pltpu.VMEM((bm, D_MODEL), F32),
                        pltpu.SemaphoreType.DMA((3,)),
                        pltpu.SemaphoreType.DMA((2,)),
                        pltpu.SemaphoreType.DMA(())],
    )
    return pl.pallas_call(
        functools.partial(_moe_kernel, nblk=nblk),
        grid_spec=grid_spec,
        out_shape=jax.ShapeDtypeStruct((2 * t, D_MODEL), F32),
        compiler_params=pltpu.CompilerParams(dimension_semantics=("arbitrary",),
                                             vmem_limit_bytes=VMEM_LIMIT,
                                             disable_bounds_checks=True),
        name="moe_experts",
    )(blk_e, blk_nv, row_asg, x2, w13, w2)


def _final_kernel(x_ref, y_ref, route_ref, g_ref, b_ref, o_ref, *, alpha):
    r = route_ref[...]
    moe = r[:, 2:3] * y_ref[:, :D_MODEL] + r[:, 3:4] * y_ref[:, D_MODEL:]
    o_ref[...] = _layer_norm(alpha * x_ref[...] + moe, g_ref[...], b_ref[...])


def _final(x2, y2, route, ln_g, ln_b, alpha):
    t = x2.shape[0]
    tm = ROW_TILE
    row = lambda w: pl.BlockSpec((tm, w), lambda i: (i, 0))
    full = lambda a: pl.BlockSpec(a.shape, lambda i: (0,) * a.ndim)
    return pl.pallas_call(
        functools.partial(_final_kernel, alpha=alpha),
        grid=(t // tm,),
        in_specs=[row(D_MODEL), row(2 * D_MODEL), row(LANES), full(ln_g), full(ln_b)],
        out_specs=row(D_MODEL),
        out_shape=jax.ShapeDtypeStruct((t, D_MODEL), F32),
        compiler_params=_params("arbitrary"),
        name="moe_combine_ln3",
    )(x2, y2, route, ln_g, ln_b)


def _rotary_tables(seq):
    half = HEAD_DIM // 2
    inv_freq = ROPE_THETA ** (-jnp.arange(half, dtype=F32) / half)
    ang = jnp.arange(seq, dtype=F32)[:, None] * inv_freq[None, :]
    cos, sin = jnp.cos(ang), jnp.sin(ang)
    return jnp.concatenate([cos, cos], axis=-1), jnp.concatenate([-sin, sin], axis=-1)


def _prepare_weights(l, w_in, ret_gn_g, w_out, ln1_g, ln1_b, w_mq, w_mkv, w_mo, ln2_g, ln2_b,
                     w_gr, b_gr, w_er, b_er, w1, w3, w2, ln3_g, ln3_b):
    row = lambda v: v[l].reshape(1, -1).astype(F32)
    pad = LANES - N_GROUPS - N_EXPERTS
    w_r = jnp.concatenate([w_gr[l], w_er[l], jnp.zeros((D_MODEL, pad), F32)], axis=1)
    wr_hi = w_r.astype(BF16)
    wr_lo = (w_r - wr_hi.astype(F32)).astype(BF16)
    b_r = jnp.concatenate([b_gr[l], b_er[l], jnp.zeros((pad,), F32)]).reshape(1, LANES)
    return dict(
        w_in=w_in[l].astype(BF16), gn_g=row(ret_gn_g), w_out=w_out[l].astype(BF16),
        ln1=(row(ln1_g), row(ln1_b)), w_mq=w_mq[l].astype(BF16), w_mkv=w_mkv[l].astype(BF16),
        w_mo=w_mo[l].astype(BF16), ln2=(row(ln2_g), row(ln2_b)),
        wr_hi=wr_hi, wr_lo=wr_lo, b_r=b_r,
        w13=jnp.concatenate([w1[l], w3[l]], axis=-1).astype(BF16), w2=w2[l].astype(BF16),
        ln3=(row(ln3_g), row(ln3_b)))


def _encoder_layer(x, mem, w, alpha, ret_tables):
    batch, seq, _ = x.shape
    t = batch * seq
    x2d = x.reshape(t, D_MODEL)
    cos, sin = _rotary_tables(seq)
    h = _in_proj(x2d, w['w_in'], cos, sin, seq)
    outs, lses = [], []
    for dil in DILATIONS:
        o, lse = _local_attn(h.reshape(batch, seq // dil, dil * N_SEG * SEG), batch, seq, dil)
        outs.append(o.reshape(t, SEG))
        lses.append(lse.reshape(t, LANES))
    ret_o = _retention(h.reshape(batch, seq, N_SEG * SEG), w['gn_g'], batch, seq, ret_tables)
    x1 = _out_proj(outs, lses, ret_o.reshape(t, SEG), x2d, w['w_out'], *w['ln1'], alpha)
    kv = _mem_kv(mem.reshape(batch * MEM_TOKENS, D_MODEL), w['w_mkv'])
    x2, route = _xattn(x1, kv, w['w_mq'], w['w_mo'], *w['ln2'], w['wr_hi'], w['wr_lo'], w['b_r'],
                       seq, alpha)
    y2 = _moe(x2, route, w['w13'], w['w2'])
    out = _final(x2, y2.reshape(t, 2 * D_MODEL), route, *w['ln3'], alpha)
    return out.reshape(batch, seq, D_MODEL)


def kernel(x_prompt, x_sample, mem_prompt, mem_sample, w_in, ret_gn_g, w_out, ln1_g, ln1_b,
           w_mq, w_mkv, w_mo, ln2_g, ln2_b, w_gr, b_gr, w_er, b_er, w1, w3, w2, ln3_g, ln3_b):
    depth = w_in.shape[0]
    alpha = (2 * depth) ** 0.25
    ret_tables = _retention_tables()
    y_prompt, y_sample = x_prompt, x_sample
    for l in range(depth):
        w = _prepare_weights(l, w_in, ret_gn_g, w_out, ln1_g, ln1_b, w_mq, w_mkv, w_mo, ln2_g,
                             ln2_b, w_gr, b_gr, w_er, b_er, w1, w3, w2, ln3_g, ln3_b)
        y_prompt = _encoder_layer(y_prompt, mem_prompt, w, alpha, ret_tables)
        y_sample = _encoder_layer(y_sample, mem_sample, w, alpha, ret_tables)
    return (y_prompt, y_sample)
```

```python
import functools

import numpy as np
import jax
import jax.numpy as jnp
from jax import lax
from jax.experimental import pallas as pl
from jax.experimental.pallas import tpu as pltpu

F32 = jnp.float32
BF16 = jnp.bfloat16

D_MODEL = 2048
HEAD_DIM = 128
N_HEADS = 8
SEG = N_HEADS * HEAD_DIM
N_SEG = 7
SEG_QA, SEG_KA, SEG_VA, SEG_QR, SEG_KR, SEG_VR, SEG_GR = range(7)
DILATIONS = (1, 4, 16)
HALF_WIN = 64
ROPE_THETA = 10000.0
RET_CHUNK = 128
MEM_TOKENS = 256
X_HEADS = 4
D_X = X_HEADS * HEAD_DIM
N_GROUPS = 4
EXP_PER_GROUP = 8
N_EXPERTS = N_GROUPS * EXP_PER_GROUP
D_EXPERT = D_MODEL // 4
LN_EPS = 1e-5
NEG_INF = -1e30
ATTN_SCALE = HEAD_DIM ** -0.5

LANES = 128
VMEM_LIMIT = 52 * 1024 * 1024

IN_PROJ_ROWS = 512
ATTN_ROWS = 512
RET_ROWS = 512
ROW_TILE = 256
MOE_ROWS = 256


def _params(*sem):
    return pltpu.CompilerParams(dimension_semantics=sem, vmem_limit_bytes=VMEM_LIMIT)


def _layer_norm(z, g, b):
    mu = jnp.mean(z, axis=-1, keepdims=True)
    d = z - mu
    var = jnp.mean(d * d, axis=-1, keepdims=True)
    return d * lax.rsqrt(var + LN_EPS) * g + b


def _dot_nt(a, b):
    return lax.dot_general(a, b, (((1,), (1,)), ((), ())), preferred_element_type=F32)


def _dot_tn(a, b):
    return lax.dot_general(a, b, (((0,), (0,)), ((), ())), preferred_element_type=F32)


def _dot(a, b):
    return jnp.dot(a, b, preferred_element_type=F32)


def _in_proj_kernel(x_ref, w_ref, cos_ref, sin_ref, o_ref, o4_ref, o16_ref, xb_ref, slab, *, tm):
    j = pl.program_id(1)

    @pl.when(j == 0)
    def _():
        xb_ref[...] = x_ref[...].astype(BF16)

    acc = _dot(xb_ref[...], w_ref[...])
    is_rot = (j == SEG_QA) | (j == SEG_KA) | (j == SEG_QR) | (j == SEG_KR)
    heads = [slice(h * HEAD_DIM, (h + 1) * HEAD_DIM) for h in range(N_HEADS)]

    @pl.when(is_rot)
    def _():
        scale = jnp.where(j == SEG_KR, ATTN_SCALE, 1.0).astype(F32)
        c = cos_ref[...] * scale
        s = sin_ref[...] * scale
        for h, cs in enumerate(heads):
            t = acc[:, cs]
            slab[h] = t * c + pltpu.roll(t, HEAD_DIM // 2, 1) * s

    @pl.when(jnp.logical_not(is_rot))
    def _():
        for h, cs in enumerate(heads):
            slab[h] = acc[:, cs]

    for h, cs in enumerate(heads):
        o_ref[:, cs] = slab[h].astype(BF16)

    @pl.when(j <= SEG_VA)
    def _():
        for h, cs in enumerate(heads):
            for dil, ref in ((4, o4_ref), (16, o16_ref)):
                for r in range(dil):
                    ref[r, :, cs] = slab[h, pl.ds(r, tm // dil, stride=dil), :].astype(BF16)


def _in_proj(x2d, w_in, cos, sin, batch, seq):
    t = x2d.shape[0]
    tm = IN_PROJ_ROWS
    n_s = seq // tm
    n_att = SEG_VA + 1

    def strided_out(dil):
        shape = jax.ShapeDtypeStruct((batch, dil, seq // dil, n_att * SEG), BF16)
        spec = pl.BlockSpec((None, dil, tm // dil, SEG),
                            lambda i, j: (i // n_s, 0, i % n_s, jnp.minimum(j, SEG_VA)))
        return shape, spec

    (shape4, spec4), (shape16, spec16) = strided_out(4), strided_out(16)
    return pl.pallas_call(
        functools.partial(_in_proj_kernel, tm=tm),
        grid=(t // tm, N_SEG),
        in_specs=[
            pl.BlockSpec((tm, D_MODEL), lambda i, j: (i, 0)),
            pl.BlockSpec((D_MODEL, SEG), lambda i, j: (0, j)),
            pl.BlockSpec((tm, HEAD_DIM), lambda i, j: (i % n_s, 0)),
            pl.BlockSpec((tm, HEAD_DIM), lambda i, j: (i % n_s, 0)),
        ],
        out_specs=[pl.BlockSpec((tm, SEG), lambda i, j: (i, j)), spec4, spec16],
        out_shape=[jax.ShapeDtypeStruct((t, N_SEG * SEG), BF16), shape4, shape16],
        scratch_shapes=[pltpu.VMEM((tm, D_MODEL), BF16),
                        pltpu.VMEM((N_HEADS, tm, HEAD_DIM), F32)],
        compiler_params=_params("arbitrary", "arbitrary"),
        name="in_proj",
    )(x2d, w_in, cos, sin)


def _local_attn_kernel(q_ref, kp_ref, kc_ref, kn_ref, vp_ref, vc_ref, vn_ref,
                       o_ref, lse_ref, kbuf, vbuf, *, lt, sub_len):
    i = pl.program_id(2)
    hw = HALF_WIN
    kbuf[0:hw, :] = kp_ref[...]
    kbuf[hw:hw + lt, :] = kc_ref[...]
    kbuf[hw + lt:2 * hw + lt, :] = kn_ref[...]
    vbuf[0:hw, :] = vp_ref[...]
    vbuf[hw:hw + lt, :] = vc_ref[...]
    vbuf[hw + lt:2 * hw + lt, :] = vn_ref[...]

    qb = 128
    kb = qb + 2 * hw
    row = lax.broadcasted_iota(jnp.int32, (qb, kb), 0)
    col = lax.broadcasted_iota(jnp.int32, (qb, kb), 1)
    band = jnp.abs(row + hw - col) <= hw
    lane = lax.broadcasted_iota(jnp.int32, (qb, LANES), 1)

    def body(j, carry):
        r0 = pl.multiple_of(j * qb, qb)
        kpos = i * lt + r0 - hw + col
        bias = jnp.where(band, 0.0, NEG_INF).astype(F32)
        bias = jnp.where(kpos >= 0, bias, NEG_INF)
        bias = jnp.where(kpos < sub_len, bias, NEG_INF)
        lse_tile = jnp.zeros((qb, LANES), F32)
        for h in range(N_HEADS):
            cs = slice(h * HEAD_DIM, (h + 1) * HEAD_DIM)
            q = q_ref[pl.ds(r0, qb), cs]
            k = kbuf[pl.ds(r0, kb), cs]
            v = vbuf[pl.ds(r0, kb), cs]
            s = _dot_nt(q, k) * ATTN_SCALE + bias
            m = jnp.max(s, axis=-1, keepdims=True)
            p = jnp.exp(s - m)
            l = jnp.sum(p, axis=-1, keepdims=True)
            o = _dot(p.astype(BF16), v) / l
            o_ref[pl.ds(r0, qb), cs] = o.astype(BF16)
            lse_tile = jnp.where(lane == h, m + jnp.log(l), lse_tile)
        lse_ref[pl.ds(r0, qb), :] = lse_tile
        return carry

    lax.fori_loop(0, lt // qb, body, 0)


def _local_attn(qkv, batch, seq, dil):
    sub_len = seq // dil
    lt = min(sub_len, ATTN_ROWS)
    hw = HALF_WIN
    n_halo = sub_len // hw
    per = lt // hw

    def main(seg):
        return pl.BlockSpec((None, None, lt, SEG), lambda b, r, i: (b, r, i, seg))

    def prev(seg):
        return pl.BlockSpec((None, None, hw, SEG),
                            lambda b, r, i: (b, r, jnp.maximum(i * per - 1, 0), seg))

    def nxt(seg):
        return pl.BlockSpec((None, None, hw, SEG),
                            lambda b, r, i: (b, r, jnp.minimum((i + 1) * per, n_halo - 1), seg))

    kern = functools.partial(_local_attn_kernel, lt=lt, sub_len=sub_len)
    return pl.pallas_call(
        kern,
        grid=(batch, dil, sub_len // lt),
        in_specs=[main(SEG_QA), prev(SEG_KA), main(SEG_KA), nxt(SEG_KA),
                  prev(SEG_VA), main(SEG_VA), nxt(SEG_VA)],
        out_specs=[pl.BlockSpec((None, lt, SEG), lambda b, r, i: (b, i, r)),
                   pl.BlockSpec((None, lt, LANES), lambda b, r, i: (b, i, r))],
        out_shape=[jax.ShapeDtypeStruct((batch, sub_len, dil * SEG), BF16),
                   jax.ShapeDtypeStruct((batch, sub_len, dil * LANES), F32)],
        scratch_shapes=[pltpu.VMEM((lt + 2 * hw, SEG), BF16),
                        pltpu.VMEM((lt + 2 * hw, SEG), BF16)],
        compiler_params=_params("arbitrary", "arbitrary", "arbitrary"),
        name="local_attn_d%d" % dil,
    )(qkv, qkv, qkv, qkv, qkv, qkv, qkv)


def _retention_tables():
    h = np.arange(N_HEADS, dtype=np.float64)
    gf = 1.0 - 2.0 ** (-5.0 - h)
    gb = 1.0 - 2.0 ** (-5.5 - h)
    c = RET_CHUNK
    idx = np.arange(c, dtype=np.float64)
    diff = idx[:, None] - idx[None, :]
    dec_f = np.where(diff >= 0, gf[:, None, None] ** np.maximum(diff, 0.0), 0.0)
    dec_b = np.where(diff < 0, gb[:, None, None] ** np.maximum(-diff, 0.0), 0.0)
    decay = dec_f + dec_b
    rows = lambda t: np.broadcast_to(t[:, :, None], (N_HEADS, c, HEAD_DIM))
    xi_f = rows(gf[:, None] ** (idx + 1.0)[None])
    zeta_f = rows(gf[:, None] ** (c - 1.0 - idx)[None])
    xi_b = rows(gb[:, None] ** (c - idx)[None])
    zeta_b = rows(gb[:, None] ** idx[None])
    f = lambda t: jnp.asarray(np.ascontiguousarray(t), F32)
    return (f(decay), f(xi_f), f(zeta_f), f(xi_b), f(zeta_b),
            tuple(float(g ** c) for g in gf), tuple(float(g ** c) for g in gb))


def _ret_fwd_kernel(q_ref, k_ref, v_ref, dec_ref, xi_ref, zeta_ref, o_ref, state, *, rt, cd):
    @pl.when(pl.program_id(1) == 0)
    def _():
        state[...] = jnp.zeros_like(state)

    for c in range(rt // RET_CHUNK):
        rs = slice(c * RET_CHUNK, (c + 1) * RET_CHUNK)
        for h in range(N_HEADS):
            cs = slice(h * HEAD_DIM, (h + 1) * HEAD_DIM)
            q = q_ref[rs, cs]
            k = k_ref[rs, cs]
            v = v_ref[rs, cs]
            a = _dot_nt(q, k) * dec_ref[h]
            o = _dot(a.astype(BF16), v)
            s_old = state[h]
            o = o + _dot(q, s_old.astype(BF16)) * xi_ref[h]
            kz = (k.astype(F32) * zeta_ref[h]).astype(BF16)
            state[h] = cd[h] * s_old + _dot_tn(kz, v)
            o_ref[rs, cs] = o


def _ret_bwd_kernel(q_ref, k_ref, v_ref, g_ref, r_ref, xi_ref, zeta_ref, gn_ref, o_ref, state,
                    *, rt, cd):
    @pl.when(pl.program_id(1) == 0)
    def _():
        state[...] = jnp.zeros_like(state)

    for c in reversed(range(rt // RET_CHUNK)):
        rs = slice(c * RET_CHUNK, (c + 1) * RET_CHUNK)
        for h in range(N_HEADS):
            cs = slice(h * HEAD_DIM, (h + 1) * HEAD_DIM)
            q = q_ref[rs, cs]
            k = k_ref[rs, cs]
            v = v_ref[rs, cs]
            s_old = state[h]
            r = r_ref[rs, cs] + _dot(q, s_old.astype(BF16)) * xi_ref[h]
            kz = (k.astype(F32) * zeta_ref[h]).astype(BF16)
            state[h] = cd[h] * s_old + _dot_tn(kz, v)
            mu = jnp.mean(r, axis=-1, keepdims=True)
            d = r - mu
            var = jnp.mean(d * d, axis=-1, keepdims=True)
            rn = d * lax.rsqrt(var + LN_EPS) * gn_ref[:, cs]
            g = g_ref[rs, cs].astype(F32)
            silu = g / (1.0 + jnp.exp(-g))
            o_ref[rs, cs] = (silu * rn).astype(BF16)


def _retention(h3, gn_g, batch, seq, tables):
    decay, xi_f, zeta_f, xi_b, zeta_b, cd_f, cd_b = tables
    rt = min(seq, RET_ROWS)
    nr = seq // rt
    tab = pl.BlockSpec((N_HEADS, RET_CHUNK, HEAD_DIM), lambda b, t: (0, 0, 0))
    state = pltpu.VMEM((N_HEADS, HEAD_DIM, HEAD_DIM), F32)

    fwd_seg = lambda seg: pl.BlockSpec((None, rt, SEG), lambda b, t: (b, t, seg))
    r_fwd = pl.pallas_call(
        functools.partial(_ret_fwd_kernel, rt=rt, cd=cd_f),
        grid=(batch, nr),
        in_specs=[fwd_seg(SEG_QR), fwd_seg(SEG_KR), fwd_seg(SEG_VR), tab, tab, tab],
        out_specs=pl.BlockSpec((None, rt, SEG), lambda b, t: (b, t, 0)),
        out_shape=jax.ShapeDtypeStruct((batch, seq, SEG), F32),
        scratch_shapes=[state],
        compiler_params=_params("arbitrary", "arbitrary"),
        name="retention_fwd",
    )(h3, h3, h3, decay, xi_f, zeta_f)

    bwd_seg = lambda seg: pl.BlockSpec((None, rt, SEG), lambda b, t: (b, nr - 1 - t, seg))
    return pl.pallas_call(
        functools.partial(_ret_bwd_kernel, rt=rt, cd=cd_b),
        grid=(batch, nr),
        in_specs=[bwd_seg(SEG_QR), bwd_seg(SEG_KR), bwd_seg(SEG_VR), bwd_seg(SEG_GR),
                  pl.BlockSpec((None, rt, SEG), lambda b, t: (b, nr - 1 - t, 0)),
                  tab, tab, pl.BlockSpec((1, SEG), lambda b, t: (0, 0))],
        out_specs=pl.BlockSpec((None, rt, SEG), lambda b, t: (b, nr - 1 - t, 0)),
        out_shape=jax.ShapeDtypeStruct((batch, seq, SEG), BF16),
        scratch_shapes=[state],
        compiler_params=_params("arbitrary", "arbitrary"),
        name="retention_bwd",
    )(h3, h3, h3, h3, r_fwd, xi_b, zeta_b, gn_g)


def _out_proj_kernel(o1_ref, o2_ref, o3_ref, l1_ref, l2_ref, l3_ref, ret_ref, x_ref, w_ref,
                     g_ref, b_ref, out_ref, attn_buf, *, alpha):
    la, lb, lc = l1_ref[...], l2_ref[...], l3_ref[...]
    m = jnp.maximum(jnp.maximum(la, lb), lc)
    ea, eb, ec = jnp.exp(la - m), jnp.exp(lb - m), jnp.exp(lc - m)
    inv = 1.0 / (ea + eb + ec)
    wa, wb, wc = ea * inv, eb * inv, ec * inv
    rows = la.shape[0]
    for h in range(N_HEADS):
        cs = slice(h * HEAD_DIM, (h + 1) * HEAD_DIM)
        bc = lambda w: jnp.broadcast_to(w[:, h:h + 1], (rows, HEAD_DIM))
        mix = (bc(wa) * o1_ref[:, cs].astype(F32) + bc(wb) * o2_ref[:, cs].astype(F32)
               + bc(wc) * o3_ref[:, cs].astype(F32))
        attn_buf[:, cs] = mix.astype(BF16)
    y = _dot(attn_buf[...], w_ref[0:SEG, :]) + _dot(ret_ref[...], w_ref[SEG:2 * SEG, :])
    out_ref[...] = _layer_norm(alpha * x_ref[...] + y, g_ref[...], b_ref[...])


def _out_proj(outs, lses, ret_o, x2d, w_out, ln_g, ln_b, alpha):
    t = x2d.shape[0]
    tm = ROW_TILE
    row = lambda w: pl.BlockSpec((tm, w), lambda i: (i, 0))
    full = lambda a: pl.BlockSpec(a.shape, lambda i: (0,) * a.ndim)
    return pl.pallas_call(
        functools.partial(_out_proj_kernel, alpha=alpha),
        grid=(t // tm,),
        in_specs=[row(SEG)] * 3 + [row(LANES)] * 3 + [row(SEG), row(D_MODEL),
                                                     full(w_out), full(ln_g), full(ln_b)],
        out_specs=row(D_MODEL),
        out_shape=jax.ShapeDtypeStruct((t, D_MODEL), F32),
        scratch_shapes=[pltpu.VMEM((tm, SEG), BF16)],
        compiler_params=_params("arbitrary"),
        name="out_proj_ln1",
    )(*outs, *lses, ret_o, x2d, w_out, ln_g, ln_b)


def _mem_kv_kernel(m_ref, w_ref, o_ref):
    o_ref[...] = _dot(m_ref[...].astype(BF16), w_ref[...]).astype(BF16)


def _mem_kv(mem2d, w_mkv):
    rows = mem2d.shape[0]
    tm = ROW_TILE
    return pl.pallas_call(
        _mem_kv_kernel,
        grid=(rows // tm,),
        in_specs=[pl.BlockSpec((tm, D_MODEL), lambda i: (i, 0)),
                  pl.BlockSpec(w_mkv.shape, lambda i: (0, 0))],
        out_specs=pl.BlockSpec((tm, 2 * D_X), lambda i: (i, 0)),
        out_shape=jax.ShapeDtypeStruct((rows, 2 * D_X), BF16),
        compiler_params=_params("arbitrary"),
        name="mem_kv",
    )(mem2d, w_mkv)


def _route(logits):
    shape = logits.shape
    lane_i = lax.broadcasted_iota(jnp.int32, shape, 1)
    lane = lane_i.astype(F32)
    lowest = jnp.float32(-3.0e38)
    none = jnp.float32(LANES)
    rmax = lambda t: jnp.max(t, axis=-1, keepdims=True)
    rmin = lambda t: jnp.min(t, axis=-1, keepdims=True)

    is_group = lane_i < N_GROUPS
    gl = jnp.where(is_group, logits, lowest)
    g_max = rmax(gl)
    g_idx = rmin(jnp.where(gl == g_max, lane, none))
    p_g = 1.0 / jnp.sum(jnp.where(is_group, jnp.exp(logits - g_max), 0.0), axis=-1, keepdims=True)

    e_lo = N_GROUPS + EXP_PER_GROUP * g_idx
    el = jnp.where(lane >= e_lo, logits, lowest)
    el = jnp.where(lane < e_lo + EXP_PER_GROUP, el, lowest)
    v1 = rmax(el)
    i1 = rmin(jnp.where(el == v1, lane, none))
    el2 = jnp.where(lane == i1, lowest, el)
    v2 = rmax(el2)
    i2 = rmin(jnp.where(el2 == v2, lane, none))
    e2 = jnp.exp(v2 - v1)
    g1 = p_g / (1.0 + e2)
    g2 = p_g * e2 / (1.0 + e2)
    out = jnp.where(lane_i == 0, i1 - N_GROUPS,
                    jnp.where(lane_i == 1, i2 - N_GROUPS,
                              jnp.where(lane_i == 2, g1, jnp.where(lane_i == 3, g2, 0.0))))
    return out.astype(F32)


def _xattn_kernel(x_ref, kv_ref, wq_ref, wo_ref, g_ref, b_ref, wrh_ref, wrl_ref, br_ref,
                  x2_ref, route_ref, obuf, *, alpha):
    x = x_ref[...]
    q = _dot(x.astype(BF16), wq_ref[...]).astype(BF16)
    for h in range(X_HEADS):
        cs = slice(h * HEAD_DIM, (h + 1) * HEAD_DIM)
        k = kv_ref[:, cs]
        v = kv_ref[:, D_X + h * HEAD_DIM:D_X + (h + 1) * HEAD_DIM]
        s = _dot_nt(q[:, cs], k) * ATTN_SCALE
        m = jnp.max(s, axis=-1, keepdims=True)
        p = jnp.exp(s - m)
        l = jnp.sum(p, axis=-1, keepdims=True)
        obuf[:, cs] = (_dot(p.astype(BF16), v) / l).astype(BF16)
    y = _dot(obuf[...], wo_ref[...])
    x2 = _layer_norm(alpha * x + y, g_ref[...], b_ref[...])
    x2_ref[...] = x2
    xh = x2.astype(BF16)
    xl = (x2 - xh.astype(F32)).astype(BF16)
    logits = (_dot(xh, wrh_ref[...]) + _dot(xl, wrh_ref[...]) + _dot(xh, wrl_ref[...])
              + br_ref[...])
    route_ref[...] = _route(logits)


def _xattn(x1, kv, w_mq, w_mo, ln_g, ln_b, wr_hi, wr_lo, b_r, seq, alpha):
    t = x1.shape[0]
    tm = ROW_TILE
    per_b = seq // tm
    row = lambda w: pl.BlockSpec((tm, w), lambda i: (i, 0))
    full = lambda a: pl.BlockSpec(a.shape, lambda i: (0,) * a.ndim)
    return pl.pallas_call(
        functools.partial(_xattn_kernel, alpha=alpha),
        grid=(t // tm,),
        in_specs=[row(D_MODEL),
                  pl.BlockSpec((MEM_TOKENS, 2 * D_X), lambda i: (i // per_b, 0)),
                  full(w_mq), full(w_mo), full(ln_g), full(ln_b),
                  full(wr_hi), full(wr_lo), full(b_r)],
        out_specs=[row(D_MODEL), row(LANES)],
        out_shape=[jax.ShapeDtypeStruct((t, D_MODEL), F32),
                   jax.ShapeDtypeStruct((t, LANES), F32)],
        scratch_shapes=[pltpu.VMEM((tm, D_X), BF16)],
        compiler_params=_params("arbitrary"),
        name="xattn_ln2_router",
    )(x1, kv, w_mq, w_mo, ln_g, ln_b, wr_hi, wr_lo, b_r)


def _moe_kernel(blk_e_ref, blk_nv_ref, src_hbm, dst_hbm, x_hbm, w13_ref, w2_ref, y_hbm,
                src_smem, dst_smem, xbuf, ybuf, sem_src, sem_dst, sem_g, sem_s, *, nblk, bm, n_rows):
    del blk_e_ref
    b = pl.program_id(0)

    def src_copy(blk):
        return pltpu.make_async_copy(src_hbm.at[blk], src_smem.at[blk % 2], sem_src.at[blk % 2])

    def dst_copy(blk):
        return pltpu.make_async_copy(dst_hbm.at[blk], dst_smem.at[blk % 2], sem_dst.at[blk % 2])

    def issue_gathers(blk):
        s = blk % 2
        for r in range(bm):
            pltpu.make_async_copy(x_hbm.at[pl.ds(src_smem[s, r], 1)], xbuf.at[s, pl.ds(r, 1)],
                                  sem_g.at[s]).start()

    def wait_gathers(blk):
        s = blk % 2
        pltpu.make_async_copy(x_hbm.at[pl.ds(0, bm)], xbuf.at[s], sem_g.at[s]).wait()

    @pl.when(b == 0)
    def _():
        ybuf[...] = jnp.zeros_like(ybuf)
        spare = pltpu.make_async_copy(ybuf, y_hbm.at[pl.ds(n_rows, bm)], sem_s)
        spare.start()
        spare.wait()
        src_copy(0).start()
        src_copy(0).wait()
        issue_gathers(0)
        dst_copy(0).start()
        if nblk > 1:
            src_copy(1).start()

    @pl.when(b + 1 < nblk)
    def _():
        src_copy(b + 1).wait()
        dst_copy(b + 1).start()

        @pl.when(blk_nv_ref[b + 1] > 0)
        def _():
            issue_gathers(b + 1)

    @pl.when(b + 2 < nblk)
    def _():
        src_copy(b + 2).start()

    dst_copy(b).wait()

    @pl.when(blk_nv_ref[b] > 0)
    def _():
        wait_gathers(b)
        xb = xbuf[b % 2].astype(BF16)
        h13 = _dot(xb, w13_ref[...])
        h1 = h13[:, :D_EXPERT]
        h3 = h13[:, D_EXPERT:]
        hid = (h1 / (1.0 + jnp.exp(-h1)) * h3).astype(BF16)
        ybuf[...] = _dot(hid, w2_ref[...])
        s = b % 2
        for r in range(bm):
            pltpu.make_async_copy(ybuf.at[pl.ds(r, 1)], y_hbm.at[pl.ds(dst_smem[s, r], 1)],
                                  sem_s).start()
        pltpu.make_async_copy(ybuf, y_hbm.at[pl.ds(0, bm)], sem_s).wait()


def _moe_dispatch(route, bm):
    t = route.shape[0]
    m = 2 * t
    nblk = m // bm + N_EXPERTS
    eid = route[:, 0:2].astype(jnp.int32).T.reshape(m)
    counts = jnp.sum((eid[:, None] == jnp.arange(N_EXPERTS, dtype=jnp.int32)[None, :]).astype(jnp.int32),
                     axis=0)
    order = jnp.argsort(eid).astype(jnp.int32)
    start = jnp.cumsum(counts) - counts
    nb_e = (counts + bm - 1) // bm
    bend = jnp.cumsum(nb_e)
    bstart = bend - nb_e
    blk = jnp.arange(nblk, dtype=jnp.int32)
    blk_e = jnp.minimum(jnp.searchsorted(bend, blk, side='right'), N_EXPERTS - 1).astype(jnp.int32)
    within = blk - bstart[blk_e]
    blk_nv = jnp.where(blk < bend[-1], jnp.clip(counts[blk_e] - within * bm, 0, bm), 0).astype(jnp.int32)
    row = jnp.arange(bm, dtype=jnp.int32)[None, :]
    sorted_pos = (start[blk_e] + within * bm)[:, None] + row
    asg = order[jnp.clip(sorted_pos, 0, m - 1)]
    valid = row < blk_nv[:, None]
    row_src = jnp.where(valid, jnp.where(asg >= t, asg - t, asg), 0).astype(jnp.int32)
    row_dst = jnp.where(valid, asg, m + row).astype(jnp.int32)
    return blk_e, blk_nv, row_src, row_dst, nblk


def _moe(x2, route, w13, w2):
    t = x2.shape[0]
    bm = MOE_ROWS
    blk_e, blk_nv, row_src, row_dst, nblk = _moe_dispatch(route, bm)
    grid_spec = pltpu.PrefetchScalarGridSpec(
        num_scalar_prefetch=2,
        grid=(nblk,),
        in_specs=[pl.BlockSpec(memory_space=pl.ANY),
                  pl.BlockSpec(memory_space=pl.ANY),
                  pl.BlockSpec(memory_space=pl.ANY),
                  pl.BlockSpec((None, D_MODEL, 2 * D_EXPERT), lambda b, be, nv: (be[b], 0, 0)),
                  pl.BlockSpec((None, D_EXPERT, D_MODEL), lambda b, be, nv: (be[b], 0, 0))],
        out_specs=pl.BlockSpec(memory_space=pl.ANY),
        scratch_shapes=[pltpu.SMEM((2, bm), jnp.int32),
                        pltpu.SMEM((2, bm), jnp.int32),
                        pltpu.VMEM((2, bm, D_MODEL), F32),
                        pltpu.VMEM((bm, D_MODEL), F32),
                        pltpu.SemaphoreType.DMA((2,)),
                        pltpu.SemaphoreType.DMA((2,)),
                        pltpu.SemaphoreType.DMA((2,)),
                        pltpu.SemaphoreType.DMA(())],
    )
    return pl.pallas_call(
        functools.partial(_moe_kernel, nblk=nblk, bm=bm, n_rows=2 * t),
        grid_spec=grid_spec,
        out_shape=jax.ShapeDtypeStruct((2 * t + bm, D_MODEL), F32),
        compiler_params=pltpu.CompilerParams(dimension_semantics=("arbitrary",),
                                             vmem_limit_bytes=VMEM_LIMIT,
                                             disable_bounds_checks=True),
        name="moe_experts",
    )(blk_e, blk_nv, row_src, row_dst, x2, w13, w2)


def _final_kernel(x_ref, y0_ref, y1_ref, route_ref, g_ref, b_ref, o_ref, *, alpha):
    r = route_ref[...]
    moe = r[:, 2:3] * y0_ref[...] + r[:, 3:4] * y1_ref[...]
    o_ref[...] = _layer_norm(alpha * x_ref[...] + moe, g_ref[...], b_ref[...])


def _final(x2, y2, route, ln_g, ln_b, alpha):
    t = x2.shape[0]
    tm = ROW_TILE
    nt = t // tm
    row = lambda w: pl.BlockSpec((tm, w), lambda i: (i, 0))
    full = lambda a: pl.BlockSpec(a.shape, lambda i: (0,) * a.ndim)
    return pl.pallas_call(
        functools.partial(_final_kernel, alpha=alpha),
        grid=(nt,),
        in_specs=[row(D_MODEL), row(D_MODEL), pl.BlockSpec((tm, D_MODEL), lambda i: (i + nt, 0)),
                  row(LANES), full(ln_g), full(ln_b)],
        out_specs=row(D_MODEL),
        out_shape=jax.ShapeDtypeStruct((t, D_MODEL), F32),
        compiler_params=_params("arbitrary"),
        name="moe_combine_ln3",
    )(x2, y2, y2, route, ln_g, ln_b)


def _rotary_tables(seq):
    half = HEAD_DIM // 2
    inv_freq = ROPE_THETA ** (-jnp.arange(half, dtype=F32) / half)
    ang = jnp.arange(seq, dtype=F32)[:, None] * inv_freq[None, :]
    cos, sin = jnp.cos(ang), jnp.sin(ang)
    return jnp.concatenate([cos, cos], axis=-1), jnp.concatenate([-sin, sin], axis=-1)


def _prepare_weights(l, w_in, ret_gn_g, w_out, ln1_g, ln1_b, w_mq, w_mkv, w_mo, ln2_g, ln2_b,
                     w_gr, b_gr, w_er, b_er, w1, w3, w2, ln3_g, ln3_b):
    row = lambda v: v[l].reshape(1, -1).astype(F32)
    pad = LANES - N_GROUPS - N_EXPERTS
    w_r = jnp.concatenate([w_gr[l], w_er[l], jnp.zeros((D_MODEL, pad), F32)], axis=1)
    wr_hi = w_r.astype(BF16)
    wr_lo = (w_r - wr_hi.astype(F32)).astype(BF16)
    b_r = jnp.concatenate([b_gr[l], b_er[l], jnp.zeros((pad,), F32)]).reshape(1, LANES)
    return dict(
        w_in=w_in[l].astype(BF16), gn_g=row(ret_gn_g), w_out=w_out[l].astype(BF16),
        ln1=(row(ln1_g), row(ln1_b)), w_mq=w_mq[l].astype(BF16), w_mkv=w_mkv[l].astype(BF16),
        w_mo=w_mo[l].astype(BF16), ln2=(row(ln2_g), row(ln2_b)),
        wr_hi=wr_hi, wr_lo=wr_lo, b_r=b_r,
        w13=jnp.concatenate([w1[l], w3[l]], axis=-1).astype(BF16), w2=w2[l].astype(BF16),
        ln3=(row(ln3_g), row(ln3_b)))


def _encoder_layer(x, mem, w, alpha, ret_tables):
    batch, seq, _ = x.shape
    t = batch * seq
    x2d = x.reshape(t, D_MODEL)
    cos, sin = _rotary_tables(seq)
    h, ha4, ha16 = _in_proj(x2d, w['w_in'], cos, sin, batch, seq)
    outs, lses = [], []
    for dil, qkv in zip(DILATIONS, (h.reshape(batch, 1, seq, N_SEG * SEG), ha4, ha16)):
        o, lse = _local_attn(qkv, batch, seq, dil)
        outs.append(o.reshape(t, SEG))
        lses.append(lse.reshape(t, LANES))
    ret_o = _retention(h.reshape(batch, seq, N_SEG * SEG), w['gn_g'], batch, seq, ret_tables)
    x1 = _out_proj(outs, lses, ret_o.reshape(t, SEG), x2d, w['w_out'], *w['ln1'], alpha)
    kv = _mem_kv(mem.reshape(batch * MEM_TOKENS, D_MODEL), w['w_mkv'])
    x2, route = _xattn(x1, kv, w['w_mq'], w['w_mo'], *w['ln2'], w['wr_hi'], w['wr_lo'], w['b_r'],
                       seq, alpha)
    y2 = _moe(x2, route, w['w13'], w['w2'])
    out = _final(x2, y2, route, *w['ln3'], alpha)
    return out.reshape(batch, seq, D_MODEL)


def kernel(x_prompt, x_sample, mem_prompt, mem_sample, w_in, ret_gn_g, w_out, ln1_g, ln1_b,
           w_mq, w_mkv, w_mo, ln2_g, ln2_b, w_gr, b_gr, w_er, b_er, w1, w3, w2, ln3_g, ln3_b):
    depth = w_in.shape[0]
    alpha = (2 * depth) ** 0.25
    ret_tables = _retention_tables()
    y_prompt, y_sample = x_prompt, x_sample
    for l in range(depth):
        w = _prepare_weights(l, w_in, ret_gn_g, w_out, ln1_g, ln1_b, w_mq, w_mkv, w_mo, ln2_g,
                             ln2_b, w_gr, b_gr, w_er, b_er, w1, w3, w2, ln3_g, ln3_b)
        y_prompt = _encoder_layer(y_prompt, mem_prompt, w, alpha, ret_tables)
        y_sample = _encoder_layer(y_sample, mem_sample, w, alpha, ret_tables)
    return (y_prompt, y_sample)
```

```python
import functools

import numpy as np
import jax
import jax.numpy as jnp
from jax import lax
from jax.experimental import pallas as pl
from jax.experimental.pallas import tpu as pltpu

F32 = jnp.float32
BF16 = jnp.bfloat16

D_MODEL = 2048
HEAD_DIM = 128
N_HEADS = 8
SEG = N_HEADS * HEAD_DIM
N_SEG = 7
SEG_QA, SEG_KA, SEG_VA, SEG_QR, SEG_KR, SEG_VR, SEG_GR = range(7)
DILATIONS = (1, 4, 16)
HALF_WIN = 64
ROPE_THETA = 10000.0
RET_CHUNK = 128
MEM_TOKENS = 256
X_HEADS = 4
D_X = X_HEADS * HEAD_DIM
N_GROUPS = 4
EXP_PER_GROUP = 8
N_EXPERTS = N_GROUPS * EXP_PER_GROUP
D_EXPERT = D_MODEL // 4
LN_EPS = 1e-5
NEG_INF = -1e30
ATTN_SCALE = HEAD_DIM ** -0.5

LANES = 128
VMEM_LIMIT = 52 * 1024 * 1024

IN_PROJ_ROWS = 512
ATTN_ROWS = 512
RET_ROWS = 512
ROW_TILE = 256
MOE_ROWS = 256


def _params(*sem):
    return pltpu.CompilerParams(dimension_semantics=sem, vmem_limit_bytes=VMEM_LIMIT)


def _layer_norm(z, g, b):
    mu = jnp.mean(z, axis=-1, keepdims=True)
    d = z - mu
    var = jnp.mean(d * d, axis=-1, keepdims=True)
    return d * lax.rsqrt(var + LN_EPS) * g + b


def _dot_nt(a, b):
    return lax.dot_general(a, b, (((1,), (1,)), ((), ())), preferred_element_type=F32)


def _dot_tn(a, b):
    return lax.dot_general(a, b, (((0,), (0,)), ((), ())), preferred_element_type=F32)


def _dot(a, b):
    return jnp.dot(a, b, preferred_element_type=F32)


def _in_proj_kernel(x_ref, w_ref, cos_ref, sin_ref, o_ref, o4_ref, o16_ref, xb_ref, slab, slab4,
                    *, tm):
    j = pl.program_id(1)

    @pl.when(j == 0)
    def _():
        xb_ref[...] = x_ref[...].astype(BF16)

    def segment(rot, strided):
        if rot:
            scale = jnp.where(j == SEG_KR, ATTN_SCALE, 1.0).astype(F32)
            c = cos_ref[...] * scale
            s = sin_ref[...] * scale
        for pair in range(N_HEADS // 2):
            acc = _dot(xb_ref[...], w_ref[:, pair * 2 * HEAD_DIM:(pair + 1) * 2 * HEAD_DIM])
            for hh in range(2):
                h = 2 * pair + hh
                cs = slice(h * HEAD_DIM, (h + 1) * HEAD_DIM)
                t = acc[:, hh * HEAD_DIM:(hh + 1) * HEAD_DIM]
                if rot:
                    t = t * c + pltpu.roll(t, HEAD_DIM // 2, 1) * s
                o_ref[:, cs] = t.astype(BF16)
                if strided:
                    slab[h] = t
                    for r in range(4):
                        v4 = slab[h, pl.ds(r, tm // 4, stride=4), :]
                        o4_ref[r, :, cs] = v4.astype(BF16)
                        slab4[h, r] = v4
                        for a in range(4):
                            o16_ref[r + 4 * a, :, cs] = (
                                slab4[h, r, pl.ds(a, tm // 16, stride=4), :].astype(BF16))

    is_rot = (j == SEG_QA) | (j == SEG_KA) | (j == SEG_QR) | (j == SEG_KR)
    is_attn = j <= SEG_VA
    for rot in (True, False):
        for strided in (True, False):
            cond = (is_rot if rot else jnp.logical_not(is_rot)) & (
                is_attn if strided else jnp.logical_not(is_attn))
            pl.when(cond)(functools.partial(segment, rot, strided))


def _in_proj(x2d, w_in, cos, sin, batch, seq):
    t = x2d.shape[0]
    tm = IN_PROJ_ROWS
    n_s = seq // tm
    n_att = SEG_VA + 1

    def strided_out(dil):
        shape = jax.ShapeDtypeStruct((batch, dil, seq // dil, n_att * SEG), BF16)
        spec = pl.BlockSpec((None, dil, tm // dil, SEG),
                            lambda i, j: (i // n_s, 0, i % n_s, jnp.minimum(j, SEG_VA)))
        return shape, spec

    (shape4, spec4), (shape16, spec16) = strided_out(4), strided_out(16)
    return pl.pallas_call(
        functools.partial(_in_proj_kernel, tm=tm),
        grid=(t // tm, N_SEG),
        in_specs=[
            pl.BlockSpec((tm, D_MODEL), lambda i, j: (i, 0)),
            pl.BlockSpec((D_MODEL, SEG), lambda i, j: (0, j)),
            pl.BlockSpec((tm, HEAD_DIM), lambda i, j: (i % n_s, 0)),
            pl.BlockSpec((tm, HEAD_DIM), lambda i, j: (i % n_s, 0)),
        ],
        out_specs=[pl.BlockSpec((tm, SEG), lambda i, j: (i, j)), spec4, spec16],
        out_shape=[jax.ShapeDtypeStruct((t, N_SEG * SEG), BF16), shape4, shape16],
        scratch_shapes=[pltpu.VMEM((tm, D_MODEL), BF16),
                        pltpu.VMEM((N_HEADS, tm, HEAD_DIM), F32),
                        pltpu.VMEM((N_HEADS, 4, tm // 4, HEAD_DIM), F32)],
        compiler_params=_params("arbitrary", "arbitrary"),
        name="in_proj",
    )(x2d, w_in, cos, sin)


def _local_attn_kernel(q_ref, kp_ref, kc_ref, kn_ref, vp_ref, vc_ref, vn_ref,
                       o_ref, lse_ref, kbuf, vbuf, *, lt, sub_len):
    i = pl.program_id(2)
    hw = HALF_WIN
    kbuf[0:hw, :] = kp_ref[...]
    kbuf[hw:hw + lt, :] = kc_ref[...]
    kbuf[hw + lt:2 * hw + lt, :] = kn_ref[...]
    vbuf[0:hw, :] = vp_ref[...]
    vbuf[hw:hw + lt, :] = vc_ref[...]
    vbuf[hw + lt:2 * hw + lt, :] = vn_ref[...]

    qb = 128
    kb = qb + 2 * hw
    row = lax.broadcasted_iota(jnp.int32, (qb, kb), 0)
    col = lax.broadcasted_iota(jnp.int32, (qb, kb), 1)
    band = jnp.abs(row + hw - col) <= hw
    lane = lax.broadcasted_iota(jnp.int32, (qb, LANES), 1)

    def body(j, carry):
        r0 = pl.multiple_of(j * qb, qb)
        kpos = i * lt + r0 - hw + col
        bias = jnp.where(band, 0.0, NEG_INF).astype(F32)
        bias = jnp.where(kpos >= 0, bias, NEG_INF)
        bias = jnp.where(kpos < sub_len, bias, NEG_INF)
        lse_tile = jnp.zeros((qb, LANES), F32)
        for h in range(N_HEADS):
            cs = slice(h * HEAD_DIM, (h + 1) * HEAD_DIM)
            q = q_ref[pl.ds(r0, qb), cs]
            k = kbuf[pl.ds(r0, kb), cs]
            v = vbuf[pl.ds(r0, kb), cs]
            s = _dot_nt(q, k) * ATTN_SCALE + bias
            m = jnp.max(s, axis=-1, keepdims=True)
            p = jnp.exp(s - m)
            l = jnp.sum(p, axis=-1, keepdims=True)
            o = _dot(p.astype(BF16), v) / l
            o_ref[pl.ds(r0, qb), cs] = o.astype(BF16)
            lse_tile = jnp.where(lane == h, m + jnp.log(l), lse_tile)
        lse_ref[pl.ds(r0, qb), :] = lse_tile
        return carry

    lax.fori_loop(0, lt // qb, body, 0)


def _local_attn(qkv, batch, seq, dil):
    sub_len = seq // dil
    lt = min(sub_len, ATTN_ROWS)
    hw = HALF_WIN
    n_halo = sub_len // hw
    per = lt // hw

    def main(seg):
        return pl.BlockSpec((None, None, lt, SEG), lambda b, r, i: (b, r, i, seg))

    def prev(seg):
        return pl.BlockSpec((None, None, hw, SEG),
                            lambda b, r, i: (b, r, jnp.maximum(i * per - 1, 0), seg))

    def nxt(seg):
        return pl.BlockSpec((None, None, hw, SEG),
                            lambda b, r, i: (b, r, jnp.minimum((i + 1) * per, n_halo - 1), seg))

    kern = functools.partial(_local_attn_kernel, lt=lt, sub_len=sub_len)
    return pl.pallas_call(
        kern,
        grid=(batch, dil, sub_len // lt),
        in_specs=[main(SEG_QA), prev(SEG_KA), main(SEG_KA), nxt(SEG_KA),
                  prev(SEG_VA), main(SEG_VA), nxt(SEG_VA)],
        out_specs=[pl.BlockSpec((None, lt, SEG), lambda b, r, i: (b, i, r)),
                   pl.BlockSpec((None, lt, LANES), lambda b, r, i: (b, i, r))],
        out_shape=[jax.ShapeDtypeStruct((batch, sub_len, dil * SEG), BF16),
                   jax.ShapeDtypeStruct((batch, sub_len, dil * LANES), F32)],
        scratch_shapes=[pltpu.VMEM((lt + 2 * hw, SEG), BF16),
                        pltpu.VMEM((lt + 2 * hw, SEG), BF16)],
        compiler_params=_params("arbitrary", "arbitrary", "arbitrary"),
        name="local_attn_d%d" % dil,
    )(qkv, qkv, qkv, qkv, qkv, qkv, qkv)


def _retention_tables():
    h = np.arange(N_HEADS, dtype=np.float64)
    gf = 1.0 - 2.0 ** (-5.0 - h)
    gb = 1.0 - 2.0 ** (-5.5 - h)
    c = RET_CHUNK
    idx = np.arange(c, dtype=np.float64)
    diff = idx[:, None] - idx[None, :]
    dec_f = np.where(diff >= 0, gf[:, None, None] ** np.maximum(diff, 0.0), 0.0)
    dec_b = np.where(diff < 0, gb[:, None, None] ** np.maximum(-diff, 0.0), 0.0)
    decay = dec_f + dec_b
    rows = lambda t: np.broadcast_to(t[:, :, None], (N_HEADS, c, HEAD_DIM))
    xi_f = rows(gf[:, None] ** (idx + 1.0)[None])
    zeta_f = rows(gf[:, None] ** (c - 1.0 - idx)[None])
    xi_b = rows(gb[:, None] ** (c - idx)[None])
    zeta_b = rows(gb[:, None] ** idx[None])
    f = lambda t: jnp.asarray(np.ascontiguousarray(t), F32)
    return (f(decay), f(xi_f), f(zeta_f), f(xi_b), f(zeta_b),
            tuple(float(g ** c) for g in gf), tuple(float(g ** c) for g in gb))


def _ret_fwd_kernel(q_ref, k_ref, v_ref, dec_ref, xi_ref, zeta_ref, o_ref, state, *, rt, cd):
    @pl.when(pl.program_id(1) == 0)
    def _():
        state[...] = jnp.zeros_like(state)

    for c in range(rt // RET_CHUNK):
        rs = slice(c * RET_CHUNK, (c + 1) * RET_CHUNK)
        for h in range(N_HEADS):
            cs = slice(h * HEAD_DIM, (h + 1) * HEAD_DIM)
            q = q_ref[rs, cs]
            k = k_ref[rs, cs]
            v = v_ref[rs, cs]
            a = _dot_nt(q, k) * dec_ref[h]
            o = _dot(a.astype(BF16), v)
            s_old = state[h]
            o = o + _dot(q, s_old.astype(BF16)) * xi_ref[h]
            kz = (k.astype(F32) * zeta_ref[h]).astype(BF16)
            state[h] = cd[h] * s_old + _dot_tn(kz, v)
            o_ref[rs, cs] = o


def _ret_bwd_kernel(q_ref, k_ref, v_ref, g_ref, r_ref, xi_ref, zeta_ref, gn_ref, o_ref, state,
                    *, rt, cd):
    @pl.when(pl.program_id(1) == 0)
    def _():
        state[...] = jnp.zeros_like(state)

    for c in reversed(range(rt // RET_CHUNK)):
        rs = slice(c * RET_CHUNK, (c + 1) * RET_CHUNK)
        for h in range(N_HEADS):
            cs = slice(h * HEAD_DIM, (h + 1) * HEAD_DIM)
            q = q_ref[rs, cs]
            k = k_ref[rs, cs]
            v = v_ref[rs, cs]
            s_old = state[h]
            r = r_ref[rs, cs] + _dot(q, s_old.astype(BF16)) * xi_ref[h]
            kz = (k.astype(F32) * zeta_ref[h]).astype(BF16)
            state[h] = cd[h] * s_old + _dot_tn(kz, v)
            mu = jnp.mean(r, axis=-1, keepdims=True)
            d = r - mu
            var = jnp.mean(d * d, axis=-1, keepdims=True)
            rn = d * lax.rsqrt(var + LN_EPS) * gn_ref[:, cs]
            g = g_ref[rs, cs].astype(F32)
            silu = g / (1.0 + jnp.exp(-g))
            o_ref[rs, cs] = (silu * rn).astype(BF16)


def _retention(h3, gn_g, batch, seq, tables):
    decay, xi_f, zeta_f, xi_b, zeta_b, cd_f, cd_b = tables
    rt = min(seq, RET_ROWS)
    nr = seq // rt
    tab = pl.BlockSpec((N_HEADS, RET_CHUNK, HEAD_DIM), lambda b, t: (0, 0, 0))
    state = pltpu.VMEM((N_HEADS, HEAD_DIM, HEAD_DIM), F32)

    fwd_seg = lambda seg: pl.BlockSpec((None, rt, SEG), lambda b, t: (b, t, seg))
    r_fwd = pl.pallas_call(
        functools.partial(_ret_fwd_kernel, rt=rt, cd=cd_f),
        grid=(batch, nr),
        in_specs=[fwd_seg(SEG_QR), fwd_seg(SEG_KR), fwd_seg(SEG_VR), tab, tab, tab],
        out_specs=pl.BlockSpec((None, rt, SEG), lambda b, t: (b, t, 0)),
        out_shape=jax.ShapeDtypeStruct((batch, seq, SEG), F32),
        scratch_shapes=[state],
        compiler_params=_params("arbitrary", "arbitrary"),
        name="retention_fwd",
    )(h3, h3, h3, decay, xi_f, zeta_f)

    bwd_seg = lambda seg: pl.BlockSpec((None, rt, SEG), lambda b, t: (b, nr - 1 - t, seg))
    return pl.pallas_call(
        functools.partial(_ret_bwd_kernel, rt=rt, cd=cd_b),
        grid=(batch, nr),
        in_specs=[bwd_seg(SEG_QR), bwd_seg(SEG_KR), bwd_seg(SEG_VR), bwd_seg(SEG_GR),
                  pl.BlockSpec((None, rt, SEG), lambda b, t: (b, nr - 1 - t, 0)),
                  tab, tab, pl.BlockSpec((1, SEG), lambda b, t: (0, 0))],
        out_specs=pl.BlockSpec((None, rt, SEG), lambda b, t: (b, nr - 1 - t, 0)),
        out_shape=jax.ShapeDtypeStruct((batch, seq, SEG), BF16),
        scratch_shapes=[state],
        compiler_params=_params("arbitrary", "arbitrary"),
        name="retention_bwd",
    )(h3, h3, h3, h3, r_fwd, xi_b, zeta_b, gn_g)


def _out_proj_kernel(o1_ref, o2_ref, o3_ref, l1_ref, l2_ref, l3_ref, ret_ref, x_ref, w_ref,
                     g_ref, b_ref, out_ref, attn_buf, *, alpha):
    la, lb, lc = l1_ref[...], l2_ref[...], l3_ref[...]
    m = jnp.maximum(jnp.maximum(la, lb), lc)
    ea, eb, ec = jnp.exp(la - m), jnp.exp(lb - m), jnp.exp(lc - m)
    inv = 1.0 / (ea + eb + ec)
    wa, wb, wc = ea * inv, eb * inv, ec * inv
    rows = la.shape[0]
    for h in range(N_HEADS):
        cs = slice(h * HEAD_DIM, (h + 1) * HEAD_DIM)
        bc = lambda w: jnp.broadcast_to(w[:, h:h + 1], (rows, HEAD_DIM))
        mix = (bc(wa) * o1_ref[:, cs].astype(F32) + bc(wb) * o2_ref[:, cs].astype(F32)
               + bc(wc) * o3_ref[:, cs].astype(F32))
        attn_buf[:, cs] = mix.astype(BF16)
    y = _dot(attn_buf[...], w_ref[0:SEG, :]) + _dot(ret_ref[...], w_ref[SEG:2 * SEG, :])
    out_ref[...] = _layer_norm(alpha * x_ref[...] + y, g_ref[...], b_ref[...])


def _out_proj(outs, lses, ret_o, x2d, w_out, ln_g, ln_b, alpha):
    t = x2d.shape[0]
    tm = ROW_TILE
    row = lambda w: pl.BlockSpec((tm, w), lambda i: (i, 0))
    full = lambda a: pl.BlockSpec(a.shape, lambda i: (0,) * a.ndim)
    return pl.pallas_call(
        functools.partial(_out_proj_kernel, alpha=alpha),
        grid=(t // tm,),
        in_specs=[row(SEG)] * 3 + [row(LANES)] * 3 + [row(SEG), row(D_MODEL),
                                                     full(w_out), full(ln_g), full(ln_b)],
        out_specs=row(D_MODEL),
        out_shape=jax.ShapeDtypeStruct((t, D_MODEL), F32),
        scratch_shapes=[pltpu.VMEM((tm, SEG), BF16)],
        compiler_params=_params("arbitrary"),
        name="out_proj_ln1",
    )(*outs, *lses, ret_o, x2d, w_out, ln_g, ln_b)


def _mem_kv_kernel(m_ref, w_ref, o_ref):
    o_ref[...] = _dot(m_ref[...].astype(BF16), w_ref[...]).astype(BF16)


def _mem_kv(mem2d, w_mkv):
    rows = mem2d.shape[0]
    tm = ROW_TILE
    return pl.pallas_call(
        _mem_kv_kernel,
        grid=(rows // tm,),
        in_specs=[pl.BlockSpec((tm, D_MODEL), lambda i: (i, 0)),
                  pl.BlockSpec(w_mkv.shape, lambda i: (0, 0))],
        out_specs=pl.BlockSpec((tm, 2 * D_X), lambda i: (i, 0)),
        out_shape=jax.ShapeDtypeStruct((rows, 2 * D_X), BF16),
        compiler_params=_params("arbitrary"),
        name="mem_kv",
    )(mem2d, w_mkv)


def _route(logits):
    shape = logits.shape
    lane_i = lax.broadcasted_iota(jnp.int32, shape, 1)
    lane = lane_i.astype(F32)
    lowest = jnp.float32(-3.0e38)
    none = jnp.float32(LANES)
    rmax = lambda t: jnp.max(t, axis=-1, keepdims=True)
    rmin = lambda t: jnp.min(t, axis=-1, keepdims=True)

    is_group = lane_i < N_GROUPS
    gl = jnp.where(is_group, logits, lowest)
    g_max = rmax(gl)
    g_idx = rmin(jnp.where(gl == g_max, lane, none))
    p_g = 1.0 / jnp.sum(jnp.where(is_group, jnp.exp(logits - g_max), 0.0), axis=-1, keepdims=True)

    e_lo = N_GROUPS + EXP_PER_GROUP * g_idx
    el = jnp.where(lane >= e_lo, logits, lowest)
    el = jnp.where(lane < e_lo + EXP_PER_GROUP, el, lowest)
    v1 = rmax(el)
    i1 = rmin(jnp.where(el == v1, lane, none))
    el2 = jnp.where(lane == i1, lowest, el)
    v2 = rmax(el2)
    i2 = rmin(jnp.where(el2 == v2, lane, none))
    e2 = jnp.exp(v2 - v1)
    g1 = p_g / (1.0 + e2)
    g2 = p_g * e2 / (1.0 + e2)
    out = jnp.where(lane_i == 0, i1 - N_GROUPS,
                    jnp.where(lane_i == 1, i2 - N_GROUPS,
                              jnp.where(lane_i == 2, g1, jnp.where(lane_i == 3, g2, 0.0))))
    return out.astype(F32)


def _xattn_kernel(x_ref, kv_ref, wq_ref, wo_ref, g_ref, b_ref, wrh_ref, wrl_ref, br_ref,
                  x2_ref, route_ref, obuf, *, alpha):
    x = x_ref[...]
    q = _dot(x.astype(BF16), wq_ref[...]).astype(BF16)
    for h in range(X_HEADS):
        cs = slice(h * HEAD_DIM, (h + 1) * HEAD_DIM)
        k = kv_ref[:, cs]
        v = kv_ref[:, D_X + h * HEAD_DIM:D_X + (h + 1) * HEAD_DIM]
        s = _dot_nt(q[:, cs], k) * ATTN_SCALE
        m = jnp.max(s, axis=-1, keepdims=True)
        p = jnp.exp(s - m)
        l = jnp.sum(p, axis=-1, keepdims=True)
        obuf[:, cs] = (_dot(p.astype(BF16), v) / l).astype(BF16)
    y = _dot(obuf[...], wo_ref[...])
    x2 = _layer_norm(alpha * x + y, g_ref[...], b_ref[...])
    x2_ref[...] = x2
    xh = x2.astype(BF16)
    xl = (x2 - xh.astype(F32)).astype(BF16)
    logits = (_dot(xh, wrh_ref[...]) + _dot(xl, wrh_ref[...]) + _dot(xh, wrl_ref[...])
              + br_ref[...])
    route_ref[...] = _route(logits)


def _xattn(x1, kv, w_mq, w_mo, ln_g, ln_b, wr_hi, wr_lo, b_r, seq, alpha):
    t = x1.shape[0]
    tm = ROW_TILE
    per_b = seq // tm
    row = lambda w: pl.BlockSpec((tm, w), lambda i: (i, 0))
    full = lambda a: pl.BlockSpec(a.shape, lambda i: (0,) * a.ndim)
    return pl.pallas_call(
        functools.partial(_xattn_kernel, alpha=alpha),
        grid=(t // tm,),
        in_specs=[row(D_MODEL),
                  pl.BlockSpec((MEM_TOKENS, 2 * D_X), lambda i: (i // per_b, 0)),
                  full(w_mq), full(w_mo), full(ln_g), full(ln_b),
                  full(wr_hi), full(wr_lo), full(b_r)],
        out_specs=[row(D_MODEL), row(LANES)],
        out_shape=[jax.ShapeDtypeStruct((t, D_MODEL), F32),
                   jax.ShapeDtypeStruct((t, LANES), F32)],
        scratch_shapes=[pltpu.VMEM((tm, D_X), BF16)],
        compiler_params=_params("arbitrary"),
        name="xattn_ln2_router",
    )(x1, kv, w_mq, w_mo, ln_g, ln_b, wr_hi, wr_lo, b_r)


def _moe_kernel(blk_e_ref, blk_nv_ref, src_hbm, dst_hbm, x_hbm, w13_ref, w2_ref, y_hbm,
                src_smem, dst_smem, xbuf0, xbuf1, ybuf0, ybuf1, sem_src, sem_dst, sem_g, sem_s,
                *, nblk, bm, n_rows):
    del blk_e_ref
    b = pl.program_id(0)
    cur = b % 2
    xbuf = (xbuf0, xbuf1)
    ybuf = (ybuf0, ybuf1)

    def rows_of(blk):
        inside = (blk >= 0) & (blk < nblk)
        return jnp.where(inside, blk_nv_ref[jnp.clip(blk, 0, nblk - 1)], 0)

    nv_prev2, nv_prev, nv_cur, nv_next = rows_of(b - 2), rows_of(b - 1), rows_of(b), rows_of(b + 1)

    def src_copy(blk):
        return pltpu.make_async_copy(src_hbm.at[blk], src_smem.at[blk % 2], sem_src.at[blk % 2])

    def dst_copy(blk):
        return pltpu.make_async_copy(dst_hbm.at[blk], dst_smem.at[blk % 2], sem_dst.at[blk % 2])

    def issue_gathers(slot):
        for r in range(bm):
            pltpu.make_async_copy(x_hbm.at[pl.ds(src_smem[slot, r], 1)],
                                  xbuf[slot].at[pl.ds(r, 1)], sem_g.at[slot]).start()

    def wait_gathers(slot):
        pltpu.make_async_copy(x_hbm.at[pl.ds(0, bm)], xbuf[slot], sem_g.at[slot]).wait()

    def issue_scatters(slot):
        for r in range(bm):
            pltpu.make_async_copy(ybuf[slot].at[pl.ds(r, 1)],
                                  y_hbm.at[pl.ds(dst_smem[slot, r], 1)], sem_s.at[slot]).start()

    def wait_scatters(slot):
        pltpu.make_async_copy(ybuf[slot], y_hbm.at[pl.ds(0, bm)], sem_s.at[slot]).wait()

    def compute(slot):
        xb = xbuf[slot][...].astype(BF16)
        h13 = _dot(xb, w13_ref[...])
        h1 = h13[:, :D_EXPERT]
        h3 = h13[:, D_EXPERT:]
        hid = (h1 / (1.0 + jnp.exp(-h1)) * h3).astype(BF16)
        ybuf[slot][...] = _dot(hid, w2_ref[...])

    @pl.when(b == 0)
    def _():
        ybuf0[...] = jnp.zeros_like(ybuf0)
        spare = pltpu.make_async_copy(ybuf0, y_hbm.at[pl.ds(n_rows, bm)], sem_s.at[0])
        spare.start()
        spare.wait()
        src_copy(0).start()
        src_copy(0).wait()

        @pl.when(nv_cur > 0)
        def _():
            issue_gathers(0)
        if nblk > 1:
            src_copy(1).start()

    @pl.when(b >= 1)
    def _():
        dst_copy(b - 1).wait()

    @pl.when(b + 1 < nblk)
    def _():
        src_copy(b + 1).wait()
        dst_copy(b).start()

    @pl.when(b + 2 < nblk)
    def _():
        src_copy(b + 2).start()

    steady = (nv_prev > 0) & (nv_cur > 0) & (nv_next > 0)
    for p in (0, 1):
        q = 1 - p
        mine = cur == p

        @pl.when(mine & (nv_prev2 > 0) & (nv_prev > 0))
        def _():
            wait_scatters(p)

        @pl.when(mine & steady)
        def _():
            wait_gathers(p)
            compute(p)
            issue_gathers(q)
            issue_scatters(q)

        @pl.when(mine & (nv_cur > 0) & jnp.logical_not(steady))
        def _():
            wait_gathers(p)
            compute(p)

            @pl.when(nv_next > 0)
            def _():
                issue_gathers(q)

            @pl.when(nv_prev > 0)
            def _():
                issue_scatters(q)

        @pl.when(mine & (nv_cur == 0) & (nv_prev > 0))
        def _():
            issue_scatters(q)
            wait_scatters(q)


def _moe_dispatch(route, bm):
    t = route.shape[0]
    m = 2 * t
    nblk = m // bm + N_EXPERTS
    eid = route[:, 0:2].astype(jnp.int32).T.reshape(m)
    counts = jnp.sum((eid[:, None] == jnp.arange(N_EXPERTS, dtype=jnp.int32)[None, :]).astype(jnp.int32),
                     axis=0)
    order = jnp.argsort(eid).astype(jnp.int32)
    start = jnp.cumsum(counts) - counts
    nb_e = (counts + bm - 1) // bm
    bend = jnp.cumsum(nb_e)
    bstart = bend - nb_e
    blk = jnp.arange(nblk, dtype=jnp.int32)
    blk_e = jnp.minimum(jnp.sum((blk[:, None] >= bend[None, :]).astype(jnp.int32), axis=1),
                        N_EXPERTS - 1).astype(jnp.int32)
    within = blk - bstart[blk_e]
    blk_nv = jnp.where(blk < bend[-1], jnp.clip(counts[blk_e] - within * bm, 0, bm), 0).astype(jnp.int32)
    row = jnp.arange(bm, dtype=jnp.int32)[None, :]
    sorted_pos = (start[blk_e] + within * bm)[:, None] + row
    asg = order[jnp.clip(sorted_pos, 0, m - 1)]
    valid = row < blk_nv[:, None]
    row_src = jnp.where(valid, jnp.where(asg >= t, asg - t, asg), 0).astype(jnp.int32)
    row_dst = jnp.where(valid, asg, m + row).astype(jnp.int32)
    return blk_e, blk_nv, row_src, row_dst, nblk


def _moe(x2, route, w13, w2):
    t = x2.shape[0]
    bm = MOE_ROWS
    blk_e, blk_nv, row_src, row_dst, nblk = _moe_dispatch(route, bm)
    grid_spec = pltpu.PrefetchScalarGridSpec(
        num_scalar_prefetch=2,
        grid=(nblk,),
        in_specs=[pl.BlockSpec(memory_space=pl.ANY),
                  pl.BlockSpec(memory_space=pl.ANY),
                  pl.BlockSpec(memory_space=pl.ANY),
                  pl.BlockSpec((None, D_MODEL, 2 * D_EXPERT), lambda b, be, nv: (be[b], 0, 0)),
                  pl.BlockSpec((None, D_EXPERT, D_MODEL), lambda b, be, nv: (be[b], 0, 0))],
        out_specs=pl.BlockSpec(memory_space=pl.ANY),
        scratch_shapes=[pltpu.SMEM((2, bm), jnp.int32),
                        pltpu.SMEM((2, bm), jnp.int32),
                        pltpu.VMEM((bm, D_MODEL), F32),
                        pltpu.VMEM((bm, D_MODEL), F32),
                        pltpu.VMEM((bm, D_MODEL), F32),
                        pltpu.VMEM((bm, D_MODEL), F32),
                        pltpu.SemaphoreType.DMA((2,)),
                        pltpu.SemaphoreType.DMA((2,)),
                        pltpu.SemaphoreType.DMA((2,)),
                        pltpu.SemaphoreType.DMA((2,))],
    )
    return pl.pallas_call(
        functools.partial(_moe_kernel, nblk=nblk, bm=bm, n_rows=2 * t),
        grid_spec=grid_spec,
        out_shape=jax.ShapeDtypeStruct((2 * t + bm, D_MODEL), F32),
        compiler_params=pltpu.CompilerParams(dimension_semantics=("arbitrary",),
                                             vmem_limit_bytes=VMEM_LIMIT,
                                             disable_bounds_checks=True),
        name="moe_experts",
    )(blk_e, blk_nv, row_src, row_dst, x2, w13, w2)


def _final_kernel(x_ref, y0_ref, y1_ref, route_ref, g_ref, b_ref, o_ref, *, alpha):
    r = route_ref[...]
    moe = r[:, 2:3] * y0_ref[...] + r[:, 3:4] * y1_ref[...]
    o_ref[...] = _layer_norm(alpha * x_ref[...] + moe, g_ref[...], b_ref[...])


def _final(x2, y2, route, ln_g, ln_b, alpha):
    t = x2.shape[0]
    tm = ROW_TILE
    nt = t // tm
    row = lambda w: pl.BlockSpec((tm, w), lambda i: (i, 0))
    full = lambda a: pl.BlockSpec(a.shape, lambda i: (0,) * a.ndim)
    return pl.pallas_call(
        functools.partial(_final_kernel, alpha=alpha),
        grid=(nt,),
        in_specs=[row(D_MODEL), row(D_MODEL), pl.BlockSpec((tm, D_MODEL), lambda i: (i + nt, 0)),
                  row(LANES), full(ln_g), full(ln_b)],
        out_specs=row(D_MODEL),
        out_shape=jax.ShapeDtypeStruct((t, D_MODEL), F32),
        compiler_params=_params("arbitrary"),
        name="moe_combine_ln3",
    )(x2, y2, y2, route, ln_g, ln_b)


def _rotary_tables(seq):
    half = HEAD_DIM // 2
    inv_freq = ROPE_THETA ** (-jnp.arange(half, dtype=F32) / half)
    ang = jnp.arange(seq, dtype=F32)[:, None] * inv_freq[None, :]
    cos, sin = jnp.cos(ang), jnp.sin(ang)
    return jnp.concatenate([cos, cos], axis=-1), jnp.concatenate([-sin, sin], axis=-1)


def _prepare_weights(l, w_in, ret_gn_g, w_out, ln1_g, ln1_b, w_mq, w_mkv, w_mo, ln2_g, ln2_b,
                     w_gr, b_gr, w_er, b_er, w1, w3, w2, ln3_g, ln3_b):
    row = lambda v: v[l].reshape(1, -1).astype(F32)
    pad = LANES - N_GROUPS - N_EXPERTS
    w_r = jnp.concatenate([w_gr[l], w_er[l], jnp.zeros((D_MODEL, pad), F32)], axis=1)
    wr_hi = w_r.astype(BF16)
    wr_lo = (w_r - wr_hi.astype(F32)).astype(BF16)
    b_r = jnp.concatenate([b_gr[l], b_er[l], jnp.zeros((pad,), F32)]).reshape(1, LANES)
    return dict(
        w_in=w_in[l].astype(BF16), gn_g=row(ret_gn_g), w_out=w_out[l].astype(BF16),
        ln1=(row(ln1_g), row(ln1_b)), w_mq=w_mq[l].astype(BF16), w_mkv=w_mkv[l].astype(BF16),
        w_mo=w_mo[l].astype(BF16), ln2=(row(ln2_g), row(ln2_b)),
        wr_hi=wr_hi, wr_lo=wr_lo, b_r=b_r,
        w13=jnp.concatenate([w1[l], w3[l]], axis=-1).astype(BF16), w2=w2[l].astype(BF16),
        ln3=(row(ln3_g), row(ln3_b)))


def _encoder_layer(x, mem, w, alpha, ret_tables):
    batch, seq, _ = x.shape
    t = batch * seq
    x2d = x.reshape(t, D_MODEL)
    cos, sin = _rotary_tables(seq)
    h, ha4, ha16 = _in_proj(x2d, w['w_in'], cos, sin, batch, seq)
    outs, lses = [], []
    for dil, qkv in zip(DILATIONS, (h.reshape(batch, 1, seq, N_SEG * SEG), ha4, ha16)):
        o, lse = _local_attn(qkv, batch, seq, dil)
        outs.append(o.reshape(t, SEG))
        lses.append(lse.reshape(t, LANES))
    ret_o = _retention(h.reshape(batch, seq, N_SEG * SEG), w['gn_g'], batch, seq, ret_tables)
    x1 = _out_proj(outs, lses, ret_o.reshape(t, SEG), x2d, w['w_out'], *w['ln1'], alpha)
    kv = _mem_kv(mem.reshape(batch * MEM_TOKENS, D_MODEL), w['w_mkv'])
    x2, route = _xattn(x1, kv, w['w_mq'], w['w_mo'], *w['ln2'], w['wr_hi'], w['wr_lo'], w['b_r'],
                       seq, alpha)
    y2 = _moe(x2, route, w['w13'], w['w2'])
    out = _final(x2, y2, route, *w['ln3'], alpha)
    return out.reshape(batch, seq, D_MODEL)


def kernel(x_prompt, x_sample, mem_prompt, mem_sample, w_in, ret_gn_g, w_out, ln1_g, ln1_b,
           w_mq, w_mkv, w_mo, ln2_g, ln2_b, w_gr, b_gr, w_er, b_er, w1, w3, w2, ln3_g, ln3_b):
    depth = w_in.shape[0]
    alpha = (2 * depth) ** 0.25
    ret_tables = _retention_tables()
    y_prompt, y_sample = x_prompt, x_sample
    for l in range(depth):
        w = _prepare_weights(l, w_in, ret_gn_g, w_out, ln1_g, ln1_b, w_mq, w_mkv, w_mo, ln2_g,
                             ln2_b, w_gr, b_gr, w_er, b_er, w1, w3, w2, ln3_g, ln3_b)
        y_prompt = _encoder_layer(y_prompt, mem_prompt, w, alpha, ret_tables)
        y_sample = _encoder_layer(y_sample, mem_sample, w, alpha, ret_tables)
    return (y_prompt, y_sample)
```

```python
import functools

import numpy as np
import jax
import jax.numpy as jnp
from jax import lax
from jax.experimental import pallas as pl
from jax.experimental.pallas import tpu as pltpu

F32 = jnp.float32
BF16 = jnp.bfloat16

D_MODEL = 2048
HEAD_DIM = 128
N_HEADS = 8
SEG = N_HEADS * HEAD_DIM
N_SEG = 7
SEG_QA, SEG_KA, SEG_VA, SEG_QR, SEG_KR, SEG_VR, SEG_GR = range(7)
DILATIONS = (1, 4, 16)
HALF_WIN = 64
ROPE_THETA = 10000.0
RET_CHUNK = 128
MEM_TOKENS = 256
X_HEADS = 4
D_X = X_HEADS * HEAD_DIM
N_GROUPS = 4
EXP_PER_GROUP = 8
N_EXPERTS = N_GROUPS * EXP_PER_GROUP
D_EXPERT = D_MODEL // 4
LN_EPS = 1e-5
NEG_INF = -1e30
ATTN_SCALE = HEAD_DIM ** -0.5

LANES = 128
VMEM_LIMIT = 52 * 1024 * 1024

IN_PROJ_ROWS = 1024
IN_PROJ_VMEM = 58 * 1024 * 1024
N_SLABS = 4
TOKEN_ROWS = D_MODEL // LANES
MOE_PITCH = 24
ATTN_ROWS = 512
RET_ROWS = 512
ROW_TILE = 256
MOE_ROWS = 256


def _params(*sem):
    return pltpu.CompilerParams(dimension_semantics=sem, vmem_limit_bytes=VMEM_LIMIT)


def _layer_norm(z, g, b):
    mu = jnp.mean(z, axis=-1, keepdims=True)
    d = z - mu
    var = jnp.mean(d * d, axis=-1, keepdims=True)
    return d * lax.rsqrt(var + LN_EPS) * g + b


def _dot_nt(a, b):
    return lax.dot_general(a, b, (((1,), (1,)), ((), ())), preferred_element_type=F32)


def _dot_tn(a, b):
    return lax.dot_general(a, b, (((0,), (0,)), ((), ())), preferred_element_type=F32)


def _dot(a, b):
    return jnp.dot(a, b, preferred_element_type=F32)


def _in_proj_kernel(x_ref, w_ref, cos_ref, sin_ref, o_ref, o4_ref, o16_ref, xb_ref, slab, slab4,
                    *, tm):
    j = pl.program_id(1)

    @pl.when(j == 0)
    def _():
        xb_ref[...] = x_ref[...].astype(BF16)

    def segment(rot, strided):
        if rot:
            scale = jnp.where(j == SEG_KR, ATTN_SCALE, 1.0).astype(F32)
            c = cos_ref[...] * scale
            s = sin_ref[...] * scale
        for pair in range(N_HEADS // 2):
            acc = _dot(xb_ref[...], w_ref[:, pair * 2 * HEAD_DIM:(pair + 1) * 2 * HEAD_DIM])
            for hh in range(2):
                h = 2 * pair + hh
                cs = slice(h * HEAD_DIM, (h + 1) * HEAD_DIM)
                t = acc[:, hh * HEAD_DIM:(hh + 1) * HEAD_DIM]
                if rot:
                    t = t * c + pltpu.roll(t, HEAD_DIM // 2, 1) * s
                o_ref[:, cs] = t.astype(BF16)
                if strided:
                    k = h % N_SLABS
                    slab[k] = t
                    for r in range(4):
                        v4 = slab[k, pl.ds(r, tm // 4, stride=4), :]
                        o4_ref[r, :, cs] = v4.astype(BF16)
                        slab4[k, r] = v4
                        for a in range(4):
                            o16_ref[r + 4 * a, :, cs] = (
                                slab4[k, r, pl.ds(a, tm // 16, stride=4), :].astype(BF16))

    is_rot = (j == SEG_QA) | (j == SEG_KA) | (j == SEG_QR) | (j == SEG_KR)
    is_attn = j <= SEG_VA
    for rot in (True, False):
        for strided in (True, False):
            cond = (is_rot if rot else jnp.logical_not(is_rot)) & (
                is_attn if strided else jnp.logical_not(is_attn))
            pl.when(cond)(functools.partial(segment, rot, strided))


def _in_proj(x2d, w_in, cos, sin, batch, seq):
    t = x2d.shape[0]
    tm = IN_PROJ_ROWS
    n_s = seq // tm
    n_att = SEG_VA + 1

    def strided_out(dil):
        shape = jax.ShapeDtypeStruct((batch, dil, seq // dil, n_att * SEG), BF16)
        spec = pl.BlockSpec((None, dil, tm // dil, SEG),
                            lambda i, j: (i // n_s, 0, i % n_s, jnp.minimum(j, SEG_VA)))
        return shape, spec

    (shape4, spec4), (shape16, spec16) = strided_out(4), strided_out(16)
    return pl.pallas_call(
        functools.partial(_in_proj_kernel, tm=tm),
        grid=(t // tm, N_SEG),
        in_specs=[
            pl.BlockSpec((tm, D_MODEL), lambda i, j: (i, 0)),
            pl.BlockSpec((D_MODEL, SEG), lambda i, j: (0, j)),
            pl.BlockSpec((tm, HEAD_DIM), lambda i, j: (i % n_s, 0)),
            pl.BlockSpec((tm, HEAD_DIM), lambda i, j: (i % n_s, 0)),
        ],
        out_specs=[pl.BlockSpec((tm, SEG), lambda i, j: (i, j)), spec4, spec16],
        out_shape=[jax.ShapeDtypeStruct((t, N_SEG * SEG), BF16), shape4, shape16],
        scratch_shapes=[pltpu.VMEM((tm, D_MODEL), BF16),
                        pltpu.VMEM((N_SLABS, tm, HEAD_DIM), F32),
                        pltpu.VMEM((N_SLABS, 4, tm // 4, HEAD_DIM), F32)],
        compiler_params=pltpu.CompilerParams(dimension_semantics=("arbitrary", "arbitrary"),
                                             vmem_limit_bytes=IN_PROJ_VMEM),
        name="in_proj",
    )(x2d, w_in, cos, sin)


def _local_attn_kernel(q_ref, kp_ref, kc_ref, kn_ref, vp_ref, vc_ref, vn_ref,
                       o_ref, lse_ref, kbuf, vbuf, *, lt, sub_len):
    i = pl.program_id(2)
    hw = HALF_WIN
    kbuf[0:hw, :] = kp_ref[...]
    kbuf[hw:hw + lt, :] = kc_ref[...]
    kbuf[hw + lt:2 * hw + lt, :] = kn_ref[...]
    vbuf[0:hw, :] = vp_ref[...]
    vbuf[hw:hw + lt, :] = vc_ref[...]
    vbuf[hw + lt:2 * hw + lt, :] = vn_ref[...]

    qb = 128
    kb = qb + 2 * hw
    row = lax.broadcasted_iota(jnp.int32, (qb, kb), 0)
    col = lax.broadcasted_iota(jnp.int32, (qb, kb), 1)
    band = jnp.abs(row + hw - col) <= hw
    lane = lax.broadcasted_iota(jnp.int32, (qb, LANES), 1)

    def body(j, carry):
        r0 = pl.multiple_of(j * qb, qb)
        kpos = i * lt + r0 - hw + col
        bias = jnp.where(band, 0.0, NEG_INF).astype(F32)
        bias = jnp.where(kpos >= 0, bias, NEG_INF)
        bias = jnp.where(kpos < sub_len, bias, NEG_INF)
        lse_tile = jnp.zeros((qb, LANES), F32)
        for h in range(N_HEADS):
            cs = slice(h * HEAD_DIM, (h + 1) * HEAD_DIM)
            q = q_ref[pl.ds(r0, qb), cs]
            k = kbuf[pl.ds(r0, kb), cs]
            v = vbuf[pl.ds(r0, kb), cs]
            s = _dot_nt(q, k) * ATTN_SCALE + bias
            m = jnp.max(s, axis=-1, keepdims=True)
            p = jnp.exp(s - m)
            l = jnp.sum(p, axis=-1, keepdims=True)
            o = _dot(p.astype(BF16), v) / l
            o_ref[pl.ds(r0, qb), cs] = o.astype(BF16)
            lse_tile = jnp.where(lane == h, m + jnp.log(l), lse_tile)
        lse_ref[pl.ds(r0, qb), :] = lse_tile
        return carry

    lax.fori_loop(0, lt // qb, body, 0)


def _local_attn(qkv, batch, seq, dil):
    sub_len = seq // dil
    lt = min(sub_len, ATTN_ROWS)
    hw = HALF_WIN
    n_halo = sub_len // hw
    per = lt // hw

    def main(seg):
        return pl.BlockSpec((None, None, lt, SEG), lambda b, r, i: (b, r, i, seg))

    def prev(seg):
        return pl.BlockSpec((None, None, hw, SEG),
                            lambda b, r, i: (b, r, jnp.maximum(i * per - 1, 0), seg))

    def nxt(seg):
        return pl.BlockSpec((None, None, hw, SEG),
                            lambda b, r, i: (b, r, jnp.minimum((i + 1) * per, n_halo - 1), seg))

    kern = functools.partial(_local_attn_kernel, lt=lt, sub_len=sub_len)
    return pl.pallas_call(
        kern,
        grid=(batch, dil, sub_len // lt),
        in_specs=[main(SEG_QA), prev(SEG_KA), main(SEG_KA), nxt(SEG_KA),
                  prev(SEG_VA), main(SEG_VA), nxt(SEG_VA)],
        out_specs=[pl.BlockSpec((None, lt, SEG), lambda b, r, i: (b, i, r)),
                   pl.BlockSpec((None, lt, LANES), lambda b, r, i: (b, i, r))],
        out_shape=[jax.ShapeDtypeStruct((batch, sub_len, dil * SEG), BF16),
                   jax.ShapeDtypeStruct((batch, sub_len, dil * LANES), F32)],
        scratch_shapes=[pltpu.VMEM((lt + 2 * hw, SEG), BF16),
                        pltpu.VMEM((lt + 2 * hw, SEG), BF16)],
        compiler_params=_params("arbitrary", "arbitrary", "arbitrary"),
        name="local_attn_d%d" % dil,
    )(qkv, qkv, qkv, qkv, qkv, qkv, qkv)


def _retention_tables():
    h = np.arange(N_HEADS, dtype=np.float64)
    gf = 1.0 - 2.0 ** (-5.0 - h)
    gb = 1.0 - 2.0 ** (-5.5 - h)
    c = RET_CHUNK
    idx = np.arange(c, dtype=np.float64)
    diff = idx[:, None] - idx[None, :]
    dec_f = np.where(diff >= 0, gf[:, None, None] ** np.maximum(diff, 0.0), 0.0)
    dec_b = np.where(diff < 0, gb[:, None, None] ** np.maximum(-diff, 0.0), 0.0)
    decay = dec_f + dec_b
    rows = lambda t: np.broadcast_to(t[:, :, None], (N_HEADS, c, HEAD_DIM))
    xi_f = rows(gf[:, None] ** (idx + 1.0)[None])
    zeta_f = rows(gf[:, None] ** (c - 1.0 - idx)[None])
    xi_b = rows(gb[:, None] ** (c - idx)[None])
    zeta_b = rows(gb[:, None] ** idx[None])
    f = lambda t: jnp.asarray(np.ascontiguousarray(t), F32)
    return (f(decay), f(xi_f), f(zeta_f), f(xi_b), f(zeta_b),
            tuple(float(g ** c) for g in gf), tuple(float(g ** c) for g in gb))


def _ret_fwd_kernel(q_ref, k_ref, v_ref, dec_ref, xi_ref, zeta_ref, o_ref, state, *, rt, cd):
    @pl.when(pl.program_id(1) == 0)
    def _():
        state[...] = jnp.zeros_like(state)

    for c in range(rt // RET_CHUNK):
        rs = slice(c * RET_CHUNK, (c + 1) * RET_CHUNK)
        for h in range(N_HEADS):
            cs = slice(h * HEAD_DIM, (h + 1) * HEAD_DIM)
            q = q_ref[rs, cs]
            k = k_ref[rs, cs]
            v = v_ref[rs, cs]
            a = _dot_nt(q, k) * dec_ref[h]
            o = _dot(a.astype(BF16), v)
            s_old = state[h]
            o = o + _dot(q, s_old.astype(BF16)) * xi_ref[h]
            kz = (k.astype(F32) * zeta_ref[h]).astype(BF16)
            state[h] = cd[h] * s_old + _dot_tn(kz, v)
            o_ref[rs, cs] = o


def _ret_bwd_kernel(q_ref, k_ref, v_ref, g_ref, r_ref, xi_ref, zeta_ref, gn_ref, o_ref, state,
                    *, rt, cd):
    @pl.when(pl.program_id(1) == 0)
    def _():
        state[...] = jnp.zeros_like(state)

    for c in reversed(range(rt // RET_CHUNK)):
        rs = slice(c * RET_CHUNK, (c + 1) * RET_CHUNK)
        for h in range(N_HEADS):
            cs = slice(h * HEAD_DIM, (h + 1) * HEAD_DIM)
            q = q_ref[rs, cs]
            k = k_ref[rs, cs]
            v = v_ref[rs, cs]
            s_old = state[h]
            r = r_ref[rs, cs] + _dot(q, s_old.astype(BF16)) * xi_ref[h]
            kz = (k.astype(F32) * zeta_ref[h]).astype(BF16)
            state[h] = cd[h] * s_old + _dot_tn(kz, v)
            mu = jnp.mean(r, axis=-1, keepdims=True)
            d = r - mu
            var = jnp.mean(d * d, axis=-1, keepdims=True)
            rn = d * lax.rsqrt(var + LN_EPS) * gn_ref[:, cs]
            g = g_ref[rs, cs].astype(F32)
            silu = g / (1.0 + jnp.exp(-g))
            o_ref[rs, cs] = (silu * rn).astype(BF16)


def _retention(h3, gn_g, batch, seq, tables):
    decay, xi_f, zeta_f, xi_b, zeta_b, cd_f, cd_b = tables
    rt = min(seq, RET_ROWS)
    nr = seq // rt
    tab = pl.BlockSpec((N_HEADS, RET_CHUNK, HEAD_DIM), lambda b, t: (0, 0, 0))
    state = pltpu.VMEM((N_HEADS, HEAD_DIM, HEAD_DIM), F32)

    fwd_seg = lambda seg: pl.BlockSpec((None, rt, SEG), lambda b, t: (b, t, seg))
    r_fwd = pl.pallas_call(
        functools.partial(_ret_fwd_kernel, rt=rt, cd=cd_f),
        grid=(batch, nr),
        in_specs=[fwd_seg(SEG_QR), fwd_seg(SEG_KR), fwd_seg(SEG_VR), tab, tab, tab],
        out_specs=pl.BlockSpec((None, rt, SEG), lambda b, t: (b, t, 0)),
        out_shape=jax.ShapeDtypeStruct((batch, seq, SEG), F32),
        scratch_shapes=[state],
        compiler_params=_params("arbitrary", "arbitrary"),
        name="retention_fwd",
    )(h3, h3, h3, decay, xi_f, zeta_f)

    bwd_seg = lambda seg: pl.BlockSpec((None, rt, SEG), lambda b, t: (b, nr - 1 - t, seg))
    return pl.pallas_call(
        functools.partial(_ret_bwd_kernel, rt=rt, cd=cd_b),
        grid=(batch, nr),
        in_specs=[bwd_seg(SEG_QR), bwd_seg(SEG_KR), bwd_seg(SEG_VR), bwd_seg(SEG_GR),
                  pl.BlockSpec((None, rt, SEG), lambda b, t: (b, nr - 1 - t, 0)),
                  tab, tab, pl.BlockSpec((1, SEG), lambda b, t: (0, 0))],
        out_specs=pl.BlockSpec((None, rt, SEG), lambda b, t: (b, nr - 1 - t, 0)),
        out_shape=jax.ShapeDtypeStruct((batch, seq, SEG), BF16),
        scratch_shapes=[state],
        compiler_params=_params("arbitrary", "arbitrary"),
        name="retention_bwd",
    )(h3, h3, h3, h3, r_fwd, xi_b, zeta_b, gn_g)


def _out_proj_kernel(o1_ref, o2_ref, o3_ref, l1_ref, l2_ref, l3_ref, ret_ref, x_ref, w_ref,
                     g_ref, b_ref, out_ref, attn_buf, *, alpha):
    la, lb, lc = l1_ref[...], l2_ref[...], l3_ref[...]
    m = jnp.maximum(jnp.maximum(la, lb), lc)
    ea, eb, ec = jnp.exp(la - m), jnp.exp(lb - m), jnp.exp(lc - m)
    inv = 1.0 / (ea + eb + ec)
    wa, wb, wc = ea * inv, eb * inv, ec * inv
    rows = la.shape[0]
    for h in range(N_HEADS):
        cs = slice(h * HEAD_DIM, (h + 1) * HEAD_DIM)
        bc = lambda w: jnp.broadcast_to(w[:, h:h + 1], (rows, HEAD_DIM))
        mix = (bc(wa) * o1_ref[:, cs].astype(F32) + bc(wb) * o2_ref[:, cs].astype(F32)
               + bc(wc) * o3_ref[:, cs].astype(F32))
        attn_buf[:, cs] = mix.astype(BF16)
    y = _dot(attn_buf[...], w_ref[0:SEG, :]) + _dot(ret_ref[...], w_ref[SEG:2 * SEG, :])
    out_ref[...] = _layer_norm(alpha * x_ref[...] + y, g_ref[...], b_ref[...])


def _out_proj(outs, lses, ret_o, x2d, w_out, ln_g, ln_b, alpha):
    t = x2d.shape[0]
    tm = ROW_TILE
    row = lambda w: pl.BlockSpec((tm, w), lambda i: (i, 0))
    full = lambda a: pl.BlockSpec(a.shape, lambda i: (0,) * a.ndim)
    return pl.pallas_call(
        functools.partial(_out_proj_kernel, alpha=alpha),
        grid=(t // tm,),
        in_specs=[row(SEG)] * 3 + [row(LANES)] * 3 + [row(SEG), row(D_MODEL),
                                                     full(w_out), full(ln_g), full(ln_b)],
        out_specs=row(D_MODEL),
        out_shape=jax.ShapeDtypeStruct((t, D_MODEL), F32),
        scratch_shapes=[pltpu.VMEM((tm, SEG), BF16)],
        compiler_params=_params("arbitrary"),
        name="out_proj_ln1",
    )(*outs, *lses, ret_o, x2d, w_out, ln_g, ln_b)


def _mem_kv_kernel(m_ref, w_ref, o_ref):
    o_ref[...] = _dot(m_ref[...].astype(BF16), w_ref[...]).astype(BF16)


def _mem_kv(mem2d, w_mkv):
    rows = mem2d.shape[0]
    tm = ROW_TILE
    return pl.pallas_call(
        _mem_kv_kernel,
        grid=(rows // tm,),
        in_specs=[pl.BlockSpec((tm, D_MODEL), lambda i: (i, 0)),
                  pl.BlockSpec(w_mkv.shape, lambda i: (0, 0))],
        out_specs=pl.BlockSpec((tm, 2 * D_X), lambda i: (i, 0)),
        out_shape=jax.ShapeDtypeStruct((rows, 2 * D_X), BF16),
        compiler_params=_params("arbitrary"),
        name="mem_kv",
    )(mem2d, w_mkv)


def _route(logits):
    shape = logits.shape
    lane_i = lax.broadcasted_iota(jnp.int32, shape, 1)
    lane = lane_i.astype(F32)
    lowest = jnp.float32(-3.0e38)
    none = jnp.float32(LANES)
    rmax = lambda t: jnp.max(t, axis=-1, keepdims=True)
    rmin = lambda t: jnp.min(t, axis=-1, keepdims=True)

    is_group = lane_i < N_GROUPS
    gl = jnp.where(is_group, logits, lowest)
    g_max = rmax(gl)
    g_idx = rmin(jnp.where(gl == g_max, lane, none))
    p_g = 1.0 / jnp.sum(jnp.where(is_group, jnp.exp(logits - g_max), 0.0), axis=-1, keepdims=True)

    e_lo = N_GROUPS + EXP_PER_GROUP * g_idx
    el = jnp.where(lane >= e_lo, logits, lowest)
    el = jnp.where(lane < e_lo + EXP_PER_GROUP, el, lowest)
    v1 = rmax(el)
    i1 = rmin(jnp.where(el == v1, lane, none))
    el2 = jnp.where(lane == i1, lowest, el)
    v2 = rmax(el2)
    i2 = rmin(jnp.where(el2 == v2, lane, none))
    e2 = jnp.exp(v2 - v1)
    g1 = p_g / (1.0 + e2)
    g2 = p_g * e2 / (1.0 + e2)
    out = jnp.where(lane_i == 0, i1 - N_GROUPS,
                    jnp.where(lane_i == 1, i2 - N_GROUPS,
                              jnp.where(lane_i == 2, g1, jnp.where(lane_i == 3, g2, 0.0))))
    return out.astype(F32)


def _xattn_kernel(x_ref, kv_ref, wq_ref, wo_ref, g_ref, b_ref, wrh_ref, wrl_ref, br_ref,
                  x2_ref, x2_rows_ref, route_ref, obuf, *, alpha):
    x = x_ref[...]
    q = _dot(x.astype(BF16), wq_ref[...]).astype(BF16)
    for h in range(X_HEADS):
        cs = slice(h * HEAD_DIM, (h + 1) * HEAD_DIM)
        k = kv_ref[:, cs]
        v = kv_ref[:, D_X + h * HEAD_DIM:D_X + (h + 1) * HEAD_DIM]
        s = _dot_nt(q[:, cs], k) * ATTN_SCALE
        m = jnp.max(s, axis=-1, keepdims=True)
        p = jnp.exp(s - m)
        l = jnp.sum(p, axis=-1, keepdims=True)
        obuf[:, cs] = (_dot(p.astype(BF16), v) / l).astype(BF16)
    y = _dot(obuf[...], wo_ref[...])
    x2 = _layer_norm(alpha * x + y, g_ref[...], b_ref[...])
    x2_ref[...] = x2
    rows = x2.shape[0]
    for s in range(TOKEN_ROWS):
        x2_rows_ref[pl.ds(s, rows, stride=TOKEN_ROWS), :] = x2[:, s * LANES:(s + 1) * LANES]
    xh = x2.astype(BF16)
    xl = (x2 - xh.astype(F32)).astype(BF16)
    logits = (_dot(xh, wrh_ref[...]) + _dot(xl, wrh_ref[...]) + _dot(xh, wrl_ref[...])
              + br_ref[...])
    route_ref[...] = _route(logits)


def _xattn(x1, kv, w_mq, w_mo, ln_g, ln_b, wr_hi, wr_lo, b_r, seq, alpha):
    t = x1.shape[0]
    tm = ROW_TILE
    per_b = seq // tm
    row = lambda w: pl.BlockSpec((tm, w), lambda i: (i, 0))
    full = lambda a: pl.BlockSpec(a.shape, lambda i: (0,) * a.ndim)
    return pl.pallas_call(
        functools.partial(_xattn_kernel, alpha=alpha),
        grid=(t // tm,),
        in_specs=[row(D_MODEL),
                  pl.BlockSpec((MEM_TOKENS, 2 * D_X), lambda i: (i // per_b, 0)),
                  full(w_mq), full(w_mo), full(ln_g), full(ln_b),
                  full(wr_hi), full(wr_lo), full(b_r)],
        out_specs=[row(D_MODEL), pl.BlockSpec((tm * TOKEN_ROWS, LANES), lambda i: (i, 0)),
                   row(LANES)],
        out_shape=[jax.ShapeDtypeStruct((t, D_MODEL), F32),
                   jax.ShapeDtypeStruct((t * TOKEN_ROWS, LANES), F32),
                   jax.ShapeDtypeStruct((t, LANES), F32)],
        scratch_shapes=[pltpu.VMEM((tm, D_X), BF16)],
        compiler_params=_params("arbitrary"),
        name="xattn_ln2_router",
    )(x1, kv, w_mq, w_mo, ln_g, ln_b, wr_hi, wr_lo, b_r)


def _moe_kernel(blk_e_ref, blk_nv_ref, src_hbm, dst_hbm, x_hbm, w13_ref, w2_ref, y_hbm,
                src_smem, dst_smem, xbuf0, xbuf1, ybuf0, ybuf1, xb_ref, sem_src, sem_dst, sem_g,
                sem_s, *, nblk, bm, n_rows):
    del blk_e_ref
    b = pl.program_id(0)
    cur = b % 2
    xbuf = (xbuf0, xbuf1)
    ybuf = (ybuf0, ybuf1)

    def rows_of(blk):
        inside = (blk >= 0) & (blk < nblk)
        return jnp.where(inside, blk_nv_ref[jnp.clip(blk, 0, nblk - 1)], 0)

    nv_prev2, nv_prev, nv_cur, nv_next = rows_of(b - 2), rows_of(b - 1), rows_of(b), rows_of(b + 1)

    def src_copy(blk):
        return pltpu.make_async_copy(src_hbm.at[blk], src_smem.at[blk % 2], sem_src.at[blk % 2])

    def dst_copy(blk):
        return pltpu.make_async_copy(dst_hbm.at[blk], dst_smem.at[blk % 2], sem_dst.at[blk % 2])

    tr, pitch = TOKEN_ROWS, MOE_PITCH

    def slab(buf, r):
        return buf.at[pl.ds(r * pitch, tr)]

    def all_slabs_bytes(buf):
        return buf.at[pl.ds(0, bm * tr)]

    def issue_gathers(slot):
        for r in range(bm):
            off = pl.multiple_of(src_smem[slot, r], tr)
            pltpu.make_async_copy(x_hbm.at[pl.ds(off, tr)], slab(xbuf[slot], r),
                                  sem_g.at[slot]).start()

    def wait_gathers(slot):
        pltpu.make_async_copy(x_hbm.at[pl.ds(0, bm * tr)], all_slabs_bytes(xbuf[slot]),
                              sem_g.at[slot]).wait()

    def issue_scatters(slot):
        for r in range(bm):
            off = pl.multiple_of(dst_smem[slot, r], tr)
            pltpu.make_async_copy(slab(ybuf[slot], r), y_hbm.at[pl.ds(off, tr)],
                                  sem_s.at[slot]).start()

    def wait_scatters(slot):
        pltpu.make_async_copy(all_slabs_bytes(ybuf[slot]), y_hbm.at[pl.ds(0, bm * tr)],
                              sem_s.at[slot]).wait()

    def compute(slot):
        for s in range(tr):
            xb_ref[:, s * LANES:(s + 1) * LANES] = (
                xbuf[slot][pl.ds(s, bm, stride=pitch), :].astype(BF16))
        h13 = _dot(xb_ref[...], w13_ref[...])
        h1 = h13[:, :D_EXPERT]
        h3 = h13[:, D_EXPERT:]
        hid = (h1 / (1.0 + jnp.exp(-h1)) * h3).astype(BF16)
        y = _dot(hid, w2_ref[...])
        for s in range(tr):
            ybuf[slot][pl.ds(s, bm, stride=pitch), :] = y[:, s * LANES:(s + 1) * LANES]

    @pl.when(b == 0)
    def _():
        ybuf0[...] = jnp.zeros_like(ybuf0)
        spare = pltpu.make_async_copy(all_slabs_bytes(ybuf0),
                                      y_hbm.at[pl.ds(n_rows * tr, bm * tr)], sem_s.at[0])
        spare.start()
        spare.wait()
        src_copy(0).start()
        src_copy(0).wait()

        @pl.when(nv_cur > 0)
        def _():
            issue_gathers(0)
        if nblk > 1:
            src_copy(1).start()

    @pl.when(b >= 1)
    def _():
        dst_copy(b - 1).wait()

    @pl.when(b + 1 < nblk)
    def _():
        src_copy(b + 1).wait()
        dst_copy(b).start()

    @pl.when(b + 2 < nblk)
    def _():
        src_copy(b + 2).start()

    steady = (nv_prev > 0) & (nv_cur > 0) & (nv_next > 0)
    for p in (0, 1):
        q = 1 - p
        mine = cur == p

        @pl.when(mine & (nv_prev2 > 0) & (nv_prev > 0))
        def _():
            wait_scatters(p)

        @pl.when(mine & steady)
        def _():
            wait_gathers(p)
            compute(p)
            issue_gathers(q)
            issue_scatters(q)

        @pl.when(mine & (nv_cur > 0) & jnp.logical_not(steady))
        def _():
            wait_gathers(p)
            compute(p)

            @pl.when(nv_next > 0)
            def _():
                issue_gathers(q)

            @pl.when(nv_prev > 0)
            def _():
                issue_scatters(q)

        @pl.when(mine & (nv_cur == 0) & (nv_prev > 0))
        def _():
            issue_scatters(q)
            wait_scatters(q)


def _moe_dispatch(route, bm):
    t = route.shape[0]
    m = 2 * t
    nblk = m // bm + N_EXPERTS
    eid = route[:, 0:2].astype(jnp.int32).T.reshape(m)
    counts = jnp.sum((eid[:, None] == jnp.arange(N_EXPERTS, dtype=jnp.int32)[None, :]).astype(jnp.int32),
                     axis=0)
    order = jnp.argsort(eid).astype(jnp.int32)
    start = jnp.cumsum(counts) - counts
    nb_e = (counts + bm - 1) // bm
    bend = jnp.cumsum(nb_e)
    bstart = bend - nb_e
    blk = jnp.arange(nblk, dtype=jnp.int32)
    blk_e = jnp.minimum(jnp.sum((blk[:, None] >= bend[None, :]).astype(jnp.int32), axis=1),
                        N_EXPERTS - 1).astype(jnp.int32)
    within = blk - bstart[blk_e]
    blk_nv = jnp.where(blk < bend[-1], jnp.clip(counts[blk_e] - within * bm, 0, bm), 0).astype(jnp.int32)
    row = jnp.arange(bm, dtype=jnp.int32)[None, :]
    sorted_pos = (start[blk_e] + within * bm)[:, None] + row
    asg = order[jnp.clip(sorted_pos, 0, m - 1)]
    valid = row < blk_nv[:, None]
    row_src = jnp.where(valid, jnp.where(asg >= t, asg - t, asg), 0).astype(jnp.int32)
    row_dst = jnp.where(valid, asg, m + row).astype(jnp.int32)
    return blk_e, blk_nv, row_src * TOKEN_ROWS, row_dst * TOKEN_ROWS, nblk


def _moe(x2_rows, route, w13, w2):
    t = route.shape[0]
    bm = MOE_ROWS
    blk_e, blk_nv, row_src, row_dst, nblk = _moe_dispatch(route, bm)
    grid_spec = pltpu.PrefetchScalarGridSpec(
        num_scalar_prefetch=2,
        grid=(nblk,),
        in_specs=[pl.BlockSpec(memory_space=pl.ANY),
                  pl.BlockSpec(memory_space=pl.ANY),
                  pl.BlockSpec(memory_space=pl.ANY),
                  pl.BlockSpec((None, D_MODEL, 2 * D_EXPERT), lambda b, be, nv: (be[b], 0, 0)),
                  pl.BlockSpec((None, D_EXPERT, D_MODEL), lambda b, be, nv: (be[b], 0, 0))],
        out_specs=pl.BlockSpec(memory_space=pl.ANY),
        scratch_shapes=[pltpu.SMEM((2, bm), jnp.int32),
                        pltpu.SMEM((2, bm), jnp.int32),
                        pltpu.VMEM((bm * MOE_PITCH, LANES), F32),
                        pltpu.VMEM((bm * MOE_PITCH, LANES), F32),
                        pltpu.VMEM((bm * MOE_PITCH, LANES), F32),
                        pltpu.VMEM((bm * MOE_PITCH, LANES), F32),
                        pltpu.VMEM((bm, D_MODEL), BF16),
                        pltpu.SemaphoreType.DMA((2,)),
                        pltpu.SemaphoreType.DMA((2,)),
                        pltpu.SemaphoreType.DMA((2,)),
                        pltpu.SemaphoreType.DMA((2,))],
    )
    return pl.pallas_call(
        functools.partial(_moe_kernel, nblk=nblk, bm=bm, n_rows=2 * t),
        grid_spec=grid_spec,
        out_shape=jax.ShapeDtypeStruct(((2 * t + bm) * TOKEN_ROWS, LANES), F32),
        compiler_params=pltpu.CompilerParams(dimension_semantics=("arbitrary",),
                                             vmem_limit_bytes=VMEM_LIMIT,
                                             disable_bounds_checks=True),
        name="moe_experts",
    )(blk_e, blk_nv, row_src, row_dst, x2_rows, w13, w2)


def _final_kernel(x_ref, y0_ref, y1_ref, route_ref, g_ref, b_ref, o_ref, z_ref, *, alpha):
    r = route_ref[...]
    rows = r.shape[0]
    g0 = jnp.broadcast_to(r[:, 2:3], (rows, LANES))
    g1 = jnp.broadcast_to(r[:, 3:4], (rows, LANES))
    for s in range(TOKEN_ROWS):
        cs = slice(s * LANES, (s + 1) * LANES)
        z_ref[:, cs] = (alpha * x_ref[:, cs]
                        + g0 * y0_ref[pl.ds(s, rows, stride=TOKEN_ROWS), :]
                        + g1 * y1_ref[pl.ds(s, rows, stride=TOKEN_ROWS), :])
    o_ref[...] = _layer_norm(z_ref[...], g_ref[...], b_ref[...])


def _final(x2, y_rows, route, ln_g, ln_b, alpha):
    t = x2.shape[0]
    tm = ROW_TILE
    nt = t // tm
    row = lambda w: pl.BlockSpec((tm, w), lambda i: (i, 0))
    full = lambda a: pl.BlockSpec(a.shape, lambda i: (0,) * a.ndim)
    return pl.pallas_call(
        functools.partial(_final_kernel, alpha=alpha),
        grid=(nt,),
        in_specs=[row(D_MODEL),
                  pl.BlockSpec((tm * TOKEN_ROWS, LANES), lambda i: (i, 0)),
                  pl.BlockSpec((tm * TOKEN_ROWS, LANES), lambda i: (i + nt, 0)),
                  row(LANES), full(ln_g), full(ln_b)],
        out_specs=row(D_MODEL),
        out_shape=jax.ShapeDtypeStruct((t, D_MODEL), F32),
        scratch_shapes=[pltpu.VMEM((tm, D_MODEL), F32)],
        compiler_params=_params("arbitrary"),
        name="moe_combine_ln3",
    )(x2, y_rows, y_rows, route, ln_g, ln_b)


def _rotary_tables(seq):
    half = HEAD_DIM // 2
    inv_freq = ROPE_THETA ** (-jnp.arange(half, dtype=F32) / half)
    ang = jnp.arange(seq, dtype=F32)[:, None] * inv_freq[None, :]
    cos, sin = jnp.cos(ang), jnp.sin(ang)
    return jnp.concatenate([cos, cos], axis=-1), jnp.concatenate([-sin, sin], axis=-1)


def _prepare_weights(l, w_in, ret_gn_g, w_out, ln1_g, ln1_b, w_mq, w_mkv, w_mo, ln2_g, ln2_b,
                     w_gr, b_gr, w_er, b_er, w1, w3, w2, ln3_g, ln3_b):
    row = lambda v: v[l].reshape(1, -1).astype(F32)
    pad = LANES - N_GROUPS - N_EXPERTS
    w_r = jnp.concatenate([w_gr[l], w_er[l], jnp.zeros((D_MODEL, pad), F32)], axis=1)
    wr_hi = w_r.astype(BF16)
    wr_lo = (w_r - wr_hi.astype(F32)).astype(BF16)
    b_r = jnp.concatenate([b_gr[l], b_er[l], jnp.zeros((pad,), F32)]).reshape(1, LANES)
    return dict(
        w_in=w_in[l].astype(BF16), gn_g=row(ret_gn_g), w_out=w_out[l].astype(BF16),
        ln1=(row(ln1_g), row(ln1_b)), w_mq=w_mq[l].astype(BF16), w_mkv=w_mkv[l].astype(BF16),
        w_mo=w_mo[l].astype(BF16), ln2=(row(ln2_g), row(ln2_b)),
        wr_hi=wr_hi, wr_lo=wr_lo, b_r=b_r,
        w13=jnp.concatenate([w1[l], w3[l]], axis=-1).astype(BF16), w2=w2[l].astype(BF16),
        ln3=(row(ln3_g), row(ln3_b)))


def _encoder_layer(x, mem, w, alpha, ret_tables):
    batch, seq, _ = x.shape
    t = batch * seq
    x2d = x.reshape(t, D_MODEL)
    cos, sin = _rotary_tables(seq)
    h, ha4, ha16 = _in_proj(x2d, w['w_in'], cos, sin, batch, seq)
    outs, lses = [], []
    for dil, qkv in zip(DILATIONS, (h.reshape(batch, 1, seq, N_SEG * SEG), ha4, ha16)):
        o, lse = _local_attn(qkv, batch, seq, dil)
        outs.append(o.reshape(t, SEG))
        lses.append(lse.reshape(t, LANES))
    ret_o = _retention(h.reshape(batch, seq, N_SEG * SEG), w['gn_g'], batch, seq, ret_tables)
    x1 = _out_proj(outs, lses, ret_o.reshape(t, SEG), x2d, w['w_out'], *w['ln1'], alpha)
    kv = _mem_kv(mem.reshape(batch * MEM_TOKENS, D_MODEL), w['w_mkv'])
    x2, x2_rows, route = _xattn(x1, kv, w['w_mq'], w['w_mo'], *w['ln2'], w['wr_hi'], w['wr_lo'],
                                w['b_r'], seq, alpha)
    y_rows = _moe(x2_rows, route, w['w13'], w['w2'])
    out = _final(x2, y_rows, route, *w['ln3'], alpha)
    return out.reshape(batch, seq, D_MODEL)


def kernel(x_prompt, x_sample, mem_prompt, mem_sample, w_in, ret_gn_g, w_out, ln1_g, ln1_b,
           w_mq, w_mkv, w_mo, ln2_g, ln2_b, w_gr, b_gr, w_er, b_er, w1, w3, w2, ln3_g, ln3_b):
    depth = w_in.shape[0]
    alpha = (2 * depth) ** 0.25
    ret_tables = _retention_tables()
    y_prompt, y_sample = x_prompt, x_sample
    for l in range(depth):
        w = _prepare_weights(l, w_in, ret_gn_g, w_out, ln1_g, ln1_b, w_mq, w_mkv, w_mo, ln2_g,
                             ln2_b, w_gr, b_gr, w_er, b_er, w1, w3, w2, ln3_g, ln3_b)
        y_prompt = _encoder_layer(y_prompt, mem_prompt, w, alpha, ret_tables)
        y_sample = _encoder_layer(y_sample, mem_sample, w, alpha, ret_tables)
    return (y_prompt, y_sample)
```

```python
import functools

import numpy as np
import jax
import jax.numpy as jnp
from jax import lax
from jax.experimental import pallas as pl
from jax.experimental.pallas import tpu as pltpu

F32 = jnp.float32
BF16 = jnp.bfloat16

D_MODEL = 2048
HEAD_DIM = 128
N_HEADS = 8
SEG = N_HEADS * HEAD_DIM
N_SEG = 7
SEG_QA, SEG_KA, SEG_VA, SEG_QR, SEG_KR, SEG_VR, SEG_GR = range(7)
DILATIONS = (1, 4, 16)
HALF_WIN = 64
ROPE_THETA = 10000.0
RET_CHUNK = 128
MEM_TOKENS = 256
X_HEADS = 4
D_X = X_HEADS * HEAD_DIM
N_GROUPS = 4
EXP_PER_GROUP = 8
N_EXPERTS = N_GROUPS * EXP_PER_GROUP
D_EXPERT = D_MODEL // 4
LN_EPS = 1e-5
NEG_INF = -1e30
ATTN_SCALE = HEAD_DIM ** -0.5

LANES = 128
VMEM_LIMIT = 52 * 1024 * 1024

IN_PROJ_ROWS = 1024
IN_PROJ_VMEM = 58 * 1024 * 1024
N_SLABS = 4
TOKEN_ROWS = D_MODEL // LANES
MOE_PITCH = 24
ATTN_ROWS = 512
RET_ROWS = 512
ROW_TILE = 256
MOE_ROWS = 256


def _params(*sem):
    return pltpu.CompilerParams(dimension_semantics=sem, vmem_limit_bytes=VMEM_LIMIT)


def _layer_norm(z, g, b):
    mu = jnp.mean(z, axis=-1, keepdims=True)
    d = z - mu
    var = jnp.mean(d * d, axis=-1, keepdims=True)
    return d * lax.rsqrt(var + LN_EPS) * g + b


def _dot_nt(a, b):
    return lax.dot_general(a, b, (((1,), (1,)), ((), ())), preferred_element_type=F32)


def _dot_tn(a, b):
    return lax.dot_general(a, b, (((0,), (0,)), ((), ())), preferred_element_type=F32)


def _dot(a, b):
    return jnp.dot(a, b, preferred_element_type=F32)


def _in_proj_kernel(x_ref, w_ref, cos_ref, sin_ref, o_ref, o4_ref, o16_ref, xb_ref, slab, slab4,
                    *, tm):
    j = pl.program_id(1)

    @pl.when(j == 0)
    def _():
        xb_ref[...] = x_ref[...].astype(BF16)

    def segment(rot, strided):
        if rot:
            scale = jnp.where(j == SEG_KR, ATTN_SCALE, 1.0).astype(F32)
            c = cos_ref[...] * scale
            s = sin_ref[...] * scale
        for pair in range(N_HEADS // 2):
            acc = _dot(xb_ref[...], w_ref[:, pair * 2 * HEAD_DIM:(pair + 1) * 2 * HEAD_DIM])
            for hh in range(2):
                h = 2 * pair + hh
                cs = slice(h * HEAD_DIM, (h + 1) * HEAD_DIM)
                t = acc[:, hh * HEAD_DIM:(hh + 1) * HEAD_DIM]
                if rot:
                    t = t * c + pltpu.roll(t, HEAD_DIM // 2, 1) * s
                o_ref[:, cs] = t.astype(BF16)
                if strided:
                    k = h % N_SLABS
                    slab[k] = t
                    for r in range(4):
                        v4 = slab[k, pl.ds(r, tm // 4, stride=4), :]
                        o4_ref[r, :, cs] = v4.astype(BF16)
                        slab4[k, r] = v4
                        for a in range(4):
                            o16_ref[r + 4 * a, :, cs] = (
                                slab4[k, r, pl.ds(a, tm // 16, stride=4), :].astype(BF16))

    is_rot = (j == SEG_QA) | (j == SEG_KA) | (j == SEG_QR) | (j == SEG_KR)
    is_attn = j <= SEG_VA
    for rot in (True, False):
        for strided in (True, False):
            cond = (is_rot if rot else jnp.logical_not(is_rot)) & (
                is_attn if strided else jnp.logical_not(is_attn))
            pl.when(cond)(functools.partial(segment, rot, strided))


def _in_proj(x2d, w_in, cos, sin, batch, seq):
    t = x2d.shape[0]
    tm = IN_PROJ_ROWS
    n_s = seq // tm
    n_att = SEG_VA + 1

    def strided_out(dil):
        shape = jax.ShapeDtypeStruct((n_att, batch, dil, seq // dil, SEG), BF16)
        spec = pl.BlockSpec((None, None, dil, tm // dil, SEG),
                            lambda i, j: (jnp.minimum(j, SEG_VA), i // n_s, 0, i % n_s, 0))
        return shape, spec

    (shape4, spec4), (shape16, spec16) = strided_out(4), strided_out(16)
    return pl.pallas_call(
        functools.partial(_in_proj_kernel, tm=tm),
        grid=(t // tm, N_SEG),
        in_specs=[
            pl.BlockSpec((tm, D_MODEL), lambda i, j: (i, 0)),
            pl.BlockSpec((None, D_MODEL, SEG), lambda i, j: (j, 0, 0)),
            pl.BlockSpec((tm, HEAD_DIM), lambda i, j: (i % n_s, 0)),
            pl.BlockSpec((tm, HEAD_DIM), lambda i, j: (i % n_s, 0)),
        ],
        out_specs=[pl.BlockSpec((None, tm, SEG), lambda i, j: (j, i, 0)), spec4, spec16],
        out_shape=[jax.ShapeDtypeStruct((N_SEG, t, SEG), BF16), shape4, shape16],
        scratch_shapes=[pltpu.VMEM((tm, D_MODEL), BF16),
                        pltpu.VMEM((N_SLABS, tm, HEAD_DIM), F32),
                        pltpu.VMEM((N_SLABS, 4, tm // 4, HEAD_DIM), F32)],
        compiler_params=pltpu.CompilerParams(dimension_semantics=("arbitrary", "arbitrary"),
                                             vmem_limit_bytes=IN_PROJ_VMEM),
        name="in_proj",
    )(x2d, w_in, cos, sin)


def _local_attn_kernel(q_ref, kp_ref, kc_ref, kn_ref, vp_ref, vc_ref, vn_ref,
                       o_ref, lse_ref, kbuf, vbuf, *, lt, sub_len):
    i = pl.program_id(2)
    hw = HALF_WIN
    kbuf[0:hw, :] = kp_ref[...]
    kbuf[hw:hw + lt, :] = kc_ref[...]
    kbuf[hw + lt:2 * hw + lt, :] = kn_ref[...]
    vbuf[0:hw, :] = vp_ref[...]
    vbuf[hw:hw + lt, :] = vc_ref[...]
    vbuf[hw + lt:2 * hw + lt, :] = vn_ref[...]

    qb = 128
    kb = qb + 2 * hw
    row = lax.broadcasted_iota(jnp.int32, (qb, kb), 0)
    col = lax.broadcasted_iota(jnp.int32, (qb, kb), 1)
    band = jnp.abs(row + hw - col) <= hw
    lane = lax.broadcasted_iota(jnp.int32, (qb, LANES), 1)

    def body(j, carry):
        r0 = pl.multiple_of(j * qb, qb)
        kpos = i * lt + r0 - hw + col
        bias = jnp.where(band, 0.0, NEG_INF).astype(F32)
        bias = jnp.where(kpos >= 0, bias, NEG_INF)
        bias = jnp.where(kpos < sub_len, bias, NEG_INF)
        lse_tile = jnp.zeros((qb, LANES), F32)
        for h in range(N_HEADS):
            cs = slice(h * HEAD_DIM, (h + 1) * HEAD_DIM)
            q = q_ref[pl.ds(r0, qb), cs]
            k = kbuf[pl.ds(r0, kb), cs]
            v = vbuf[pl.ds(r0, kb), cs]
            s = _dot_nt(q, k) * ATTN_SCALE + bias
            m = jnp.max(s, axis=-1, keepdims=True)
            p = jnp.exp(s - m)
            l = jnp.sum(p, axis=-1, keepdims=True)
            o = _dot(p.astype(BF16), v) / l
            o_ref[pl.ds(r0, qb), cs] = o.astype(BF16)
            lse_tile = jnp.where(lane == h, m + jnp.log(l), lse_tile)
        lse_ref[pl.ds(r0, qb), :] = lse_tile
        return carry

    lax.fori_loop(0, lt // qb, body, 0)


def _local_attn(qkv, batch, seq, dil):
    sub_len = seq // dil
    lt = min(sub_len, ATTN_ROWS)
    hw = HALF_WIN
    n_halo = sub_len // hw
    per = lt // hw

    def main(seg):
        return pl.BlockSpec((None, None, None, lt, SEG), lambda b, r, i: (seg, b, r, i, 0))

    def prev(seg):
        return pl.BlockSpec((None, None, None, hw, SEG),
                            lambda b, r, i: (seg, b, r, jnp.maximum(i * per - 1, 0), 0))

    def nxt(seg):
        return pl.BlockSpec((None, None, None, hw, SEG),
                            lambda b, r, i: (seg, b, r, jnp.minimum((i + 1) * per, n_halo - 1), 0))

    kern = functools.partial(_local_attn_kernel, lt=lt, sub_len=sub_len)
    return pl.pallas_call(
        kern,
        grid=(batch, dil, sub_len // lt),
        in_specs=[main(SEG_QA), prev(SEG_KA), main(SEG_KA), nxt(SEG_KA),
                  prev(SEG_VA), main(SEG_VA), nxt(SEG_VA)],
        out_specs=[pl.BlockSpec((None, lt, SEG), lambda b, r, i: (b, i, r)),
                   pl.BlockSpec((None, lt, LANES), lambda b, r, i: (b, i, r))],
        out_shape=[jax.ShapeDtypeStruct((batch, sub_len, dil * SEG), BF16),
                   jax.ShapeDtypeStruct((batch, sub_len, dil * LANES), F32)],
        scratch_shapes=[pltpu.VMEM((lt + 2 * hw, SEG), BF16),
                        pltpu.VMEM((lt + 2 * hw, SEG), BF16)],
        compiler_params=_params("arbitrary", "arbitrary", "arbitrary"),
        name="local_attn_d%d" % dil,
    )(qkv, qkv, qkv, qkv, qkv, qkv, qkv)


def _retention_tables():
    h = np.arange(N_HEADS, dtype=np.float64)
    gf = 1.0 - 2.0 ** (-5.0 - h)
    gb = 1.0 - 2.0 ** (-5.5 - h)
    c = RET_CHUNK
    idx = np.arange(c, dtype=np.float64)
    diff = idx[:, None] - idx[None, :]
    dec_f = np.where(diff >= 0, gf[:, None, None] ** np.maximum(diff, 0.0), 0.0)
    dec_b = np.where(diff < 0, gb[:, None, None] ** np.maximum(-diff, 0.0), 0.0)
    decay = dec_f + dec_b
    rows = lambda t: np.broadcast_to(t[:, :, None], (N_HEADS, c, HEAD_DIM))
    xi_f = rows(gf[:, None] ** (idx + 1.0)[None])
    zeta_f = rows(gf[:, None] ** (c - 1.0 - idx)[None])
    xi_b = rows(gb[:, None] ** (c - idx)[None])
    zeta_b = rows(gb[:, None] ** idx[None])
    f = lambda t: jnp.asarray(np.ascontiguousarray(t), F32)
    return (f(decay), f(xi_f), f(zeta_f), f(xi_b), f(zeta_b),
            tuple(float(g ** c) for g in gf), tuple(float(g ** c) for g in gb))


def _ret_fwd_kernel(q_ref, k_ref, v_ref, dec_ref, xi_ref, zeta_ref, o_ref, state, *, rt, cd):
    @pl.when(pl.program_id(1) == 0)
    def _():
        state[...] = jnp.zeros_like(state)

    for c in range(rt // RET_CHUNK):
        rs = slice(c * RET_CHUNK, (c + 1) * RET_CHUNK)
        for h in range(N_HEADS):
            cs = slice(h * HEAD_DIM, (h + 1) * HEAD_DIM)
            q = q_ref[rs, cs]
            k = k_ref[rs, cs]
            v = v_ref[rs, cs]
            a = _dot_nt(q, k) * dec_ref[h]
            o = _dot(a.astype(BF16), v)
            s_old = state[h]
            o = o + _dot(q, s_old.astype(BF16)) * xi_ref[h]
            kz = (k.astype(F32) * zeta_ref[h]).astype(BF16)
            state[h] = cd[h] * s_old + _dot_tn(kz, v)
            o_ref[rs, cs] = o


def _ret_bwd_kernel(q_ref, k_ref, v_ref, g_ref, r_ref, xi_ref, zeta_ref, gn_ref, o_ref, state,
                    *, rt, cd):
    @pl.when(pl.program_id(1) == 0)
    def _():
        state[...] = jnp.zeros_like(state)

    for c in reversed(range(rt // RET_CHUNK)):
        rs = slice(c * RET_CHUNK, (c + 1) * RET_CHUNK)
        for h in range(N_HEADS):
            cs = slice(h * HEAD_DIM, (h + 1) * HEAD_DIM)
            q = q_ref[rs, cs]
            k = k_ref[rs, cs]
            v = v_ref[rs, cs]
            s_old = state[h]
            r = r_ref[rs, cs] + _dot(q, s_old.astype(BF16)) * xi_ref[h]
            kz = (k.astype(F32) * zeta_ref[h]).astype(BF16)
            state[h] = cd[h] * s_old + _dot_tn(kz, v)
            mu = jnp.mean(r, axis=-1, keepdims=True)
            d = r - mu
            var = jnp.mean(d * d, axis=-1, keepdims=True)
            rn = d * lax.rsqrt(var + LN_EPS) * gn_ref[:, cs]
            g = g_ref[rs, cs].astype(F32)
            silu = g / (1.0 + jnp.exp(-g))
            o_ref[rs, cs] = (silu * rn).astype(BF16)


def _retention(h3, gn_g, batch, seq, tables):
    decay, xi_f, zeta_f, xi_b, zeta_b, cd_f, cd_b = tables
    rt = min(seq, RET_ROWS)
    nr = seq // rt
    tab = pl.BlockSpec((N_HEADS, RET_CHUNK, HEAD_DIM), lambda b, t: (0, 0, 0))
    state = pltpu.VMEM((N_HEADS, HEAD_DIM, HEAD_DIM), F32)

    fwd_seg = lambda seg: pl.BlockSpec((None, None, rt, SEG), lambda b, t: (seg, b, t, 0))
    r_fwd = pl.pallas_call(
        functools.partial(_ret_fwd_kernel, rt=rt, cd=cd_f),
        grid=(batch, nr),
        in_specs=[fwd_seg(SEG_QR), fwd_seg(SEG_KR), fwd_seg(SEG_VR), tab, tab, tab],
        out_specs=pl.BlockSpec((None, rt, SEG), lambda b, t: (b, t, 0)),
        out_shape=jax.ShapeDtypeStruct((batch, seq, SEG), F32),
        scratch_shapes=[state],
        compiler_params=_params("arbitrary", "arbitrary"),
        name="retention_fwd",
    )(h3, h3, h3, decay, xi_f, zeta_f)

    bwd_seg = lambda seg: pl.BlockSpec((None, None, rt, SEG),
                                       lambda b, t: (seg, b, nr - 1 - t, 0))
    return pl.pallas_call(
        functools.partial(_ret_bwd_kernel, rt=rt, cd=cd_b),
        grid=(batch, nr),
        in_specs=[bwd_seg(SEG_QR), bwd_seg(SEG_KR), bwd_seg(SEG_VR), bwd_seg(SEG_GR),
                  pl.BlockSpec((None, rt, SEG), lambda b, t: (b, nr - 1 - t, 0)),
                  tab, tab, pl.BlockSpec((1, SEG), lambda b, t: (0, 0))],
        out_specs=pl.BlockSpec((None, rt, SEG), lambda b, t: (b, nr - 1 - t, 0)),
        out_shape=jax.ShapeDtypeStruct((batch, seq, SEG), BF16),
        scratch_shapes=[state],
        compiler_params=_params("arbitrary", "arbitrary"),
        name="retention_bwd",
    )(h3, h3, h3, h3, r_fwd, xi_b, zeta_b, gn_g)


def _out_proj_kernel(o1_ref, o2_ref, o3_ref, l1_ref, l2_ref, l3_ref, ret_ref, x_ref, w_ref,
                     g_ref, b_ref, out_ref, attn_buf, *, alpha):
    la, lb, lc = l1_ref[...], l2_ref[...], l3_ref[...]
    m = jnp.maximum(jnp.maximum(la, lb), lc)
    ea, eb, ec = jnp.exp(la - m), jnp.exp(lb - m), jnp.exp(lc - m)
    inv = 1.0 / (ea + eb + ec)
    wa, wb, wc = ea * inv, eb * inv, ec * inv
    rows = la.shape[0]
    for h in range(N_HEADS):
        cs = slice(h * HEAD_DIM, (h + 1) * HEAD_DIM)
        bc = lambda w: jnp.broadcast_to(w[:, h:h + 1], (rows, HEAD_DIM))
        mix = (bc(wa) * o1_ref[:, cs].astype(F32) + bc(wb) * o2_ref[:, cs].astype(F32)
               + bc(wc) * o3_ref[:, cs].astype(F32))
        attn_buf[:, cs] = mix.astype(BF16)
    y = _dot(attn_buf[...], w_ref[0:SEG, :]) + _dot(ret_ref[...], w_ref[SEG:2 * SEG, :])
    out_ref[...] = _layer_norm(alpha * x_ref[...] + y, g_ref[...], b_ref[...])


def _out_proj(outs, lses, ret_o, x2d, w_out, ln_g, ln_b, alpha):
    t = x2d.shape[0]
    tm = ROW_TILE
    row = lambda w: pl.BlockSpec((tm, w), lambda i: (i, 0))
    full = lambda a: pl.BlockSpec(a.shape, lambda i: (0,) * a.ndim)
    return pl.pallas_call(
        functools.partial(_out_proj_kernel, alpha=alpha),
        grid=(t // tm,),
        in_specs=[row(SEG)] * 3 + [row(LANES)] * 3 + [row(SEG), row(D_MODEL),
                                                     full(w_out), full(ln_g), full(ln_b)],
        out_specs=row(D_MODEL),
        out_shape=jax.ShapeDtypeStruct((t, D_MODEL), F32),
        scratch_shapes=[pltpu.VMEM((tm, SEG), BF16)],
        compiler_params=_params("arbitrary"),
        name="out_proj_ln1",
    )(*outs, *lses, ret_o, x2d, w_out, ln_g, ln_b)


def _mem_kv_kernel(m_ref, w_ref, o_ref):
    o_ref[...] = _dot(m_ref[...].astype(BF16), w_ref[...]).astype(BF16)


def _mem_kv(mem2d, w_mkv):
    rows = mem2d.shape[0]
    tm = ROW_TILE
    return pl.pallas_call(
        _mem_kv_kernel,
        grid=(rows // tm,),
        in_specs=[pl.BlockSpec((tm, D_MODEL), lambda i: (i, 0)),
                  pl.BlockSpec(w_mkv.shape, lambda i: (0, 0))],
        out_specs=pl.BlockSpec((tm, 2 * D_X), lambda i: (i, 0)),
        out_shape=jax.ShapeDtypeStruct((rows, 2 * D_X), BF16),
        compiler_params=_params("arbitrary"),
        name="mem_kv",
    )(mem2d, w_mkv)


def _route(logits):
    shape = logits.shape
    lane_i = lax.broadcasted_iota(jnp.int32, shape, 1)
    lane = lane_i.astype(F32)
    lowest = jnp.float32(-3.0e38)
    none = jnp.float32(LANES)
    rmax = lambda t: jnp.max(t, axis=-1, keepdims=True)
    rmin = lambda t: jnp.min(t, axis=-1, keepdims=True)

    is_group = lane_i < N_GROUPS
    gl = jnp.where(is_group, logits, lowest)
    g_max = rmax(gl)
    g_idx = rmin(jnp.where(gl == g_max, lane, none))
    p_g = 1.0 / jnp.sum(jnp.where(is_group, jnp.exp(logits - g_max), 0.0), axis=-1, keepdims=True)

    e_lo = N_GROUPS + EXP_PER_GROUP * g_idx
    el = jnp.where(lane >= e_lo, logits, lowest)
    el = jnp.where(lane < e_lo + EXP_PER_GROUP, el, lowest)
    v1 = rmax(el)
    i1 = rmin(jnp.where(el == v1, lane, none))
    el2 = jnp.where(lane == i1, lowest, el)
    v2 = rmax(el2)
    i2 = rmin(jnp.where(el2 == v2, lane, none))
    e2 = jnp.exp(v2 - v1)
    g1 = p_g / (1.0 + e2)
    g2 = p_g * e2 / (1.0 + e2)
    out = jnp.where(lane_i == 0, i1 - N_GROUPS,
                    jnp.where(lane_i == 1, i2 - N_GROUPS,
                              jnp.where(lane_i == 2, g1, jnp.where(lane_i == 3, g2, 0.0))))
    return out.astype(F32)


def _xattn_kernel(x_ref, kv_ref, wq_ref, wo_ref, g_ref, b_ref, wrh_ref, wrl_ref, br_ref,
                  x2_ref, x2_rows_ref, route_ref, obuf, *, alpha):
    x = x_ref[...]
    q = _dot(x.astype(BF16), wq_ref[...]).astype(BF16)
    for h in range(X_HEADS):
        cs = slice(h * HEAD_DIM, (h + 1) * HEAD_DIM)
        k = kv_ref[:, cs]
        v = kv_ref[:, D_X + h * HEAD_DIM:D_X + (h + 1) * HEAD_DIM]
        s = _dot_nt(q[:, cs], k) * ATTN_SCALE
        m = jnp.max(s, axis=-1, keepdims=True)
        p = jnp.exp(s - m)
        l = jnp.sum(p, axis=-1, keepdims=True)
        obuf[:, cs] = (_dot(p.astype(BF16), v) / l).astype(BF16)
    y = _dot(obuf[...], wo_ref[...])
    x2 = _layer_norm(alpha * x + y, g_ref[...], b_ref[...])
    x2_ref[...] = x2
    rows = x2.shape[0]
    for s in range(TOKEN_ROWS):
        x2_rows_ref[pl.ds(s, rows, stride=TOKEN_ROWS), :] = x2[:, s * LANES:(s + 1) * LANES]
    xh = x2.astype(BF16)
    xl = (x2 - xh.astype(F32)).astype(BF16)
    logits = (_dot(xh, wrh_ref[...]) + _dot(xl, wrh_ref[...]) + _dot(xh, wrl_ref[...])
              + br_ref[...])
    route_ref[...] = _route(logits)


def _xattn(x1, kv, w_mq, w_mo, ln_g, ln_b, wr_hi, wr_lo, b_r, seq, alpha):
    t = x1.shape[0]
    tm = ROW_TILE
    per_b = seq // tm
    row = lambda w: pl.BlockSpec((tm, w), lambda i: (i, 0))
    full = lambda a: pl.BlockSpec(a.shape, lambda i: (0,) * a.ndim)
    return pl.pallas_call(
        functools.partial(_xattn_kernel, alpha=alpha),
        grid=(t // tm,),
        in_specs=[row(D_MODEL),
                  pl.BlockSpec((MEM_TOKENS, 2 * D_X), lambda i: (i // per_b, 0)),
                  full(w_mq), full(w_mo), full(ln_g), full(ln_b),
                  full(wr_hi), full(wr_lo), full(b_r)],
        out_specs=[row(D_MODEL), pl.BlockSpec((tm * TOKEN_ROWS, LANES), lambda i: (i, 0)),
                   row(LANES)],
        out_shape=[jax.ShapeDtypeStruct((t, D_MODEL), F32),
                   jax.ShapeDtypeStruct((t * TOKEN_ROWS, LANES), F32),
                   jax.ShapeDtypeStruct((t, LANES), F32)],
        scratch_shapes=[pltpu.VMEM((tm, D_X), BF16)],
        compiler_params=_params("arbitrary"),
        name="xattn_ln2_router",
    )(x1, kv, w_mq, w_mo, ln_g, ln_b, wr_hi, wr_lo, b_r)


def _moe_kernel(blk_e_ref, blk_nv_ref, src_hbm, dst_hbm, x_hbm, w13_ref, w2_ref, y_hbm,
                src_smem, dst_smem, xbuf0, xbuf1, ybuf0, ybuf1, xb_ref, sem_src, sem_dst, sem_g,
                sem_s, *, nblk, bm, n_rows):
    del blk_e_ref
    b = pl.program_id(0)
    cur = b % 2
    xbuf = (xbuf0, xbuf1)
    ybuf = (ybuf0, ybuf1)

    def rows_of(blk):
        inside = (blk >= 0) & (blk < nblk)
        return jnp.where(inside, blk_nv_ref[jnp.clip(blk, 0, nblk - 1)], 0)

    nv_prev2, nv_prev, nv_cur, nv_next = rows_of(b - 2), rows_of(b - 1), rows_of(b), rows_of(b + 1)

    def src_copy(blk):
        return pltpu.make_async_copy(src_hbm.at[blk], src_smem.at[blk % 2], sem_src.at[blk % 2])

    def dst_copy(blk):
        return pltpu.make_async_copy(dst_hbm.at[blk], dst_smem.at[blk % 2], sem_dst.at[blk % 2])

    tr, pitch = TOKEN_ROWS, MOE_PITCH

    def slab(buf, r):
        return buf.at[pl.ds(r * pitch, tr)]

    def all_slabs_bytes(buf):
        return buf.at[pl.ds(0, bm * tr)]

    def issue_gathers(slot):
        for r in range(bm):
            off = pl.multiple_of(src_smem[slot, r], tr)
            pltpu.make_async_copy(x_hbm.at[pl.ds(off, tr)], slab(xbuf[slot], r),
                                  sem_g.at[slot]).start(priority=r % 2)

    def wait_gathers(slot):
        pltpu.make_async_copy(x_hbm.at[pl.ds(0, bm * tr)], all_slabs_bytes(xbuf[slot]),
                              sem_g.at[slot]).wait()

    def issue_scatters(slot):
        for r in range(bm):
            off = pl.multiple_of(dst_smem[slot, r], tr)
            pltpu.make_async_copy(slab(ybuf[slot], r), y_hbm.at[pl.ds(off, tr)],
                                  sem_s.at[slot]).start(priority=r % 2)

    def wait_scatters(slot):
        pltpu.make_async_copy(all_slabs_bytes(ybuf[slot]), y_hbm.at[pl.ds(0, bm * tr)],
                              sem_s.at[slot]).wait()

    def compute(slot):
        for s in range(tr):
            xb_ref[:, s * LANES:(s + 1) * LANES] = (
                xbuf[slot][pl.ds(s, bm, stride=pitch), :].astype(BF16))
        h13 = _dot(xb_ref[...], w13_ref[...])
        h1 = h13[:, :D_EXPERT]
        h3 = h13[:, D_EXPERT:]
        hid = (h1 / (1.0 + jnp.exp(-h1)) * h3).astype(BF16)
        y = _dot(hid, w2_ref[...])
        for s in range(tr):
            ybuf[slot][pl.ds(s, bm, stride=pitch), :] = y[:, s * LANES:(s + 1) * LANES]

    @pl.when(b == 0)
    def _():
        ybuf0[...] = jnp.zeros_like(ybuf0)
        spare = pltpu.make_async_copy(all_slabs_bytes(ybuf0),
                                      y_hbm.at[pl.ds(n_rows * tr, bm * tr)], sem_s.at[0])
        spare.start()
        spare.wait()
        src_copy(0).start()
        src_copy(0).wait()

        @pl.when(nv_cur > 0)
        def _():
            issue_gathers(0)
        if nblk > 1:
            src_copy(1).start()

    @pl.when(b >= 1)
    def _():
        dst_copy(b - 1).wait()

    @pl.when(b + 1 < nblk)
    def _():
        src_copy(b + 1).wait()
        dst_copy(b).start()

    @pl.when(b + 2 < nblk)
    def _():
        src_copy(b + 2).start()

    steady = (nv_prev > 0) & (nv_cur > 0) & (nv_next > 0)
    for p in (0, 1):
        q = 1 - p
        mine = cur == p

        @pl.when(mine & (nv_prev2 > 0) & (nv_prev > 0))
        def _():
            wait_scatters(p)

        @pl.when(mine & steady)
        def _():
            wait_gathers(p)
            issue_gathers(q)
            issue_scatters(q)
            compute(p)

        @pl.when(mine & (nv_cur > 0) & jnp.logical_not(steady))
        def _():
            wait_gathers(p)

            @pl.when(nv_next > 0)
            def _():
                issue_gathers(q)

            @pl.when(nv_prev > 0)
            def _():
                issue_scatters(q)
            compute(p)

        @pl.when(mine & (nv_cur == 0) & (nv_prev > 0))
        def _():
            issue_scatters(q)
            wait_scatters(q)


def _moe_dispatch(route, bm):
    t = route.shape[0]
    m = 2 * t
    nblk = m // bm + N_EXPERTS
    eid = route[:, 0:2].astype(jnp.int32).T.reshape(m)
    counts = jnp.sum((eid[:, None] == jnp.arange(N_EXPERTS, dtype=jnp.int32)[None, :]).astype(jnp.int32),
                     axis=0)
    order = jnp.argsort(eid).astype(jnp.int32)
    start = jnp.cumsum(counts) - counts
    nb_e = (counts + bm - 1) // bm
    bend = jnp.cumsum(nb_e)
    bstart = bend - nb_e
    blk = jnp.arange(nblk, dtype=jnp.int32)
    blk_e = jnp.minimum(jnp.sum((blk[:, None] >= bend[None, :]).astype(jnp.int32), axis=1),
                        N_EXPERTS - 1).astype(jnp.int32)
    within = blk - bstart[blk_e]
    blk_nv = jnp.where(blk < bend[-1], jnp.clip(counts[blk_e] - within * bm, 0, bm), 0).astype(jnp.int32)
    row = jnp.arange(bm, dtype=jnp.int32)[None, :]
    sorted_pos = (start[blk_e] + within * bm)[:, None] + row
    asg = order[jnp.clip(sorted_pos, 0, m - 1)]
    valid = row < blk_nv[:, None]
    row_src = jnp.where(valid, jnp.where(asg >= t, asg - t, asg), 0).astype(jnp.int32)
    row_dst = jnp.where(valid, asg, m + row).astype(jnp.int32)
    return blk_e, blk_nv, row_src * TOKEN_ROWS, row_dst * TOKEN_ROWS, nblk


def _moe(x2_rows, route, w13, w2):
    t = route.shape[0]
    bm = MOE_ROWS
    blk_e, blk_nv, row_src, row_dst, nblk = _moe_dispatch(route, bm)
    grid_spec = pltpu.PrefetchScalarGridSpec(
        num_scalar_prefetch=2,
        grid=(nblk,),
        in_specs=[pl.BlockSpec(memory_space=pl.ANY),
                  pl.BlockSpec(memory_space=pl.ANY),
                  pl.BlockSpec(memory_space=pl.ANY),
                  pl.BlockSpec((None, D_MODEL, 2 * D_EXPERT), lambda b, be, nv: (be[b], 0, 0)),
                  pl.BlockSpec((None, D_EXPERT, D_MODEL), lambda b, be, nv: (be[b], 0, 0))],
        out_specs=pl.BlockSpec(memory_space=pl.ANY),
        scratch_shapes=[pltpu.SMEM((2, bm), jnp.int32),
                        pltpu.SMEM((2, bm), jnp.int32),
                        pltpu.VMEM((bm * MOE_PITCH, LANES), F32),
                        pltpu.VMEM((bm * MOE_PITCH, LANES), F32),
                        pltpu.VMEM((bm * MOE_PITCH, LANES), F32),
                        pltpu.VMEM((bm * MOE_PITCH, LANES), F32),
                        pltpu.VMEM((bm, D_MODEL), BF16),
                        pltpu.SemaphoreType.DMA((2,)),
                        pltpu.SemaphoreType.DMA((2,)),
                        pltpu.SemaphoreType.DMA((2,)),
                        pltpu.SemaphoreType.DMA((2,))],
    )
    return pl.pallas_call(
        functools.partial(_moe_kernel, nblk=nblk, bm=bm, n_rows=2 * t),
        grid_spec=grid_spec,
        out_shape=jax.ShapeDtypeStruct(((2 * t + bm) * TOKEN_ROWS, LANES), F32),
        compiler_params=pltpu.CompilerParams(dimension_semantics=("arbitrary",),
                                             vmem_limit_bytes=VMEM_LIMIT,
                                             disable_bounds_checks=True),
        name="moe_experts",
    )(blk_e, blk_nv, row_src, row_dst, x2_rows, w13, w2)


def _final_kernel(x_ref, y0_ref, y1_ref, route_ref, g_ref, b_ref, o_ref, z_ref, *, alpha):
    r = route_ref[...]
    rows = r.shape[0]
    g0 = jnp.broadcast_to(r[:, 2:3], (rows, LANES))
    g1 = jnp.broadcast_to(r[:, 3:4], (rows, LANES))
    for s in range(TOKEN_ROWS):
        cs = slice(s * LANES, (s + 1) * LANES)
        z_ref[:, cs] = (alpha * x_ref[:, cs]
                        + g0 * y0_ref[pl.ds(s, rows, stride=TOKEN_ROWS), :]
                        + g1 * y1_ref[pl.ds(s, rows, stride=TOKEN_ROWS), :])
    o_ref[...] = _layer_norm(z_ref[...], g_ref[...], b_ref[...])


def _final(x2, y_rows, route, ln_g, ln_b, alpha):
    t = x2.shape[0]
    tm = ROW_TILE
    nt = t // tm
    row = lambda w: pl.BlockSpec((tm, w), lambda i: (i, 0))
    full = lambda a: pl.BlockSpec(a.shape, lambda i: (0,) * a.ndim)
    return pl.pallas_call(
        functools.partial(_final_kernel, alpha=alpha),
        grid=(nt,),
        in_specs=[row(D_MODEL),
                  pl.BlockSpec((tm * TOKEN_ROWS, LANES), lambda i: (i, 0)),
                  pl.BlockSpec((tm * TOKEN_ROWS, LANES), lambda i: (i + nt, 0)),
                  row(LANES), full(ln_g), full(ln_b)],
        out_specs=row(D_MODEL),
        out_shape=jax.ShapeDtypeStruct((t, D_MODEL), F32),
        scratch_shapes=[pltpu.VMEM((tm, D_MODEL), F32)],
        compiler_params=_params("arbitrary"),
        name="moe_combine_ln3",
    )(x2, y_rows, y_rows, route, ln_g, ln_b)


def _rotary_tables(seq):
    half = HEAD_DIM // 2
    inv_freq = ROPE_THETA ** (-jnp.arange(half, dtype=F32) / half)
    ang = jnp.arange(seq, dtype=F32)[:, None] * inv_freq[None, :]
    cos, sin = jnp.cos(ang), jnp.sin(ang)
    return jnp.concatenate([cos, cos], axis=-1), jnp.concatenate([-sin, sin], axis=-1)


def _prepare_weights(l, w_in, ret_gn_g, w_out, ln1_g, ln1_b, w_mq, w_mkv, w_mo, ln2_g, ln2_b,
                     w_gr, b_gr, w_er, b_er, w1, w3, w2, ln3_g, ln3_b):
    row = lambda v: v[l].reshape(1, -1).astype(F32)
    pad = LANES - N_GROUPS - N_EXPERTS
    w_r = jnp.concatenate([w_gr[l], w_er[l], jnp.zeros((D_MODEL, pad), F32)], axis=1)
    wr_hi = w_r.astype(BF16)
    wr_lo = (w_r - wr_hi.astype(F32)).astype(BF16)
    b_r = jnp.concatenate([b_gr[l], b_er[l], jnp.zeros((pad,), F32)]).reshape(1, LANES)
    return dict(
        w_in=w_in[l].astype(BF16).reshape(D_MODEL, N_SEG, SEG).transpose(1, 0, 2), gn_g=row(ret_gn_g), w_out=w_out[l].astype(BF16),
        ln1=(row(ln1_g), row(ln1_b)), w_mq=w_mq[l].astype(BF16), w_mkv=w_mkv[l].astype(BF16),
        w_mo=w_mo[l].astype(BF16), ln2=(row(ln2_g), row(ln2_b)),
        wr_hi=wr_hi, wr_lo=wr_lo, b_r=b_r,
        w13=jnp.concatenate([w1[l], w3[l]], axis=-1).astype(BF16), w2=w2[l].astype(BF16),
        ln3=(row(ln3_g), row(ln3_b)))


def _encoder_layer(x, mem, w, alpha, ret_tables):
    batch, seq, _ = x.shape
    t = batch * seq
    x2d = x.reshape(t, D_MODEL)
    cos, sin = _rotary_tables(seq)
    h, ha4, ha16 = _in_proj(x2d, w['w_in'], cos, sin, batch, seq)
    outs, lses = [], []
    for dil, qkv in zip(DILATIONS, (h.reshape(N_SEG, batch, 1, seq, SEG), ha4, ha16)):
        o, lse = _local_attn(qkv, batch, seq, dil)
        outs.append(o.reshape(t, SEG))
        lses.append(lse.reshape(t, LANES))
    ret_o = _retention(h.reshape(N_SEG, batch, seq, SEG), w['gn_g'], batch, seq, ret_tables)
    x1 = _out_proj(outs, lses, ret_o.reshape(t, SEG), x2d, w['w_out'], *w['ln1'], alpha)
    kv = _mem_kv(mem.reshape(batch * MEM_TOKENS, D_MODEL), w['w_mkv'])
    x2, x2_rows, route = _xattn(x1, kv, w['w_mq'], w['w_mo'], *w['ln2'], w['wr_hi'], w['wr_lo'],
                                w['b_r'], seq, alpha)
    y_rows = _moe(x2_rows, route, w['w13'], w['w2'])
    out = _final(x2, y_rows, route, *w['ln3'], alpha)
    return out.reshape(batch, seq, D_MODEL)


def kernel(x_prompt, x_sample, mem_prompt, mem_sample, w_in, ret_gn_g, w_out, ln1_g, ln1_b,
           w_mq, w_mkv, w_mo, ln2_g, ln2_b, w_gr, b_gr, w_er, b_er, w1, w3, w2, ln3_g, ln3_b):
    depth = w_in.shape[0]
    alpha = (2 * depth) ** 0.25
    ret_tables = _retention_tables()
    y_prompt, y_sample = x_prompt, x_sample
    for l in range(depth):
        w = _prepare_weights(l, w_in, ret_gn_g, w_out, ln1_g, ln1_b, w_mq, w_mkv, w_mo, ln2_g,
                             ln2_b, w_gr, b_gr, w_er, b_er, w1, w3, w2, ln3_g, ln3_b)
        y_prompt = _encoder_layer(y_prompt, mem_prompt, w, alpha, ret_tables)
        y_sample = _encoder_layer(y_sample, mem_sample, w, alpha, ret_tables)
    return (y_prompt, y_sample)
```

```python
import functools

import numpy as np
import jax
import jax.numpy as jnp
from jax import lax
from jax.experimental import pallas as pl
from jax.experimental.pallas import tpu as pltpu

F32 = jnp.float32
BF16 = jnp.bfloat16

D_MODEL = 2048
HEAD_DIM = 128
N_HEADS = 8
SEG = N_HEADS * HEAD_DIM
N_SEG = 7
SEG_QA, SEG_KA, SEG_VA, SEG_QR, SEG_KR, SEG_VR, SEG_GR = range(7)
DILATIONS = (1, 4, 16)
HALF_WIN = 64
ROPE_THETA = 10000.0
RET_CHUNK = 128
MEM_TOKENS = 256
X_HEADS = 4
D_X = X_HEADS * HEAD_DIM
N_GROUPS = 4
EXP_PER_GROUP = 8
N_EXPERTS = N_GROUPS * EXP_PER_GROUP
D_EXPERT = D_MODEL // 4
LN_EPS = 1e-5
NEG_INF = -1e30
ATTN_SCALE = HEAD_DIM ** -0.5

LANES = 128
VMEM_LIMIT = 52 * 1024 * 1024

IN_PROJ_ROWS = 1024
IN_PROJ_VMEM = 58 * 1024 * 1024
N_SLABS = 4
TOKEN_ROWS = D_MODEL // LANES
MOE_PITCH = 24
MOE_COPY_PIECES = 20
ATTN_ROWS = 512
RET_ROWS = 512
ROW_TILE = 256
MOE_ROWS = 256


def _params(*sem):
    return pltpu.CompilerParams(dimension_semantics=sem, vmem_limit_bytes=VMEM_LIMIT)


def _layer_norm(z, g, b):
    mu = jnp.mean(z, axis=-1, keepdims=True)
    d = z - mu
    var = jnp.mean(d * d, axis=-1, keepdims=True)
    return d * lax.rsqrt(var + LN_EPS) * g + b


def _dot_nt(a, b):
    return lax.dot_general(a, b, (((1,), (1,)), ((), ())), preferred_element_type=F32)


def _dot_tn(a, b):
    return lax.dot_general(a, b, (((0,), (0,)), ((), ())), preferred_element_type=F32)


def _dot(a, b):
    return jnp.dot(a, b, preferred_element_type=F32)


def _in_proj_kernel(x_ref, w_ref, cos_ref, sin_ref, o_ref, o4_ref, o16_ref, xb_ref, slab, slab4,
                    *, tm):
    j = pl.program_id(1)

    @pl.when(j == 0)
    def _():
        xb_ref[...] = x_ref[...].astype(BF16)

    def segment(rot, strided):
        if rot:
            scale = jnp.where(j == SEG_KR, ATTN_SCALE, 1.0).astype(F32)
            c = cos_ref[...] * scale
            s = sin_ref[...] * scale
        for pair in range(N_HEADS // 2):
            acc = _dot(xb_ref[...], w_ref[:, pair * 2 * HEAD_DIM:(pair + 1) * 2 * HEAD_DIM])
            for hh in range(2):
                h = 2 * pair + hh
                cs = slice(h * HEAD_DIM, (h + 1) * HEAD_DIM)
                t = acc[:, hh * HEAD_DIM:(hh + 1) * HEAD_DIM]
                if rot:
                    t = t * c + pltpu.roll(t, HEAD_DIM // 2, 1) * s
                o_ref[:, cs] = t.astype(BF16)
                if strided:
                    k = h % N_SLABS
                    slab[k] = t
                    for r in range(4):
                        v4 = slab[k, pl.ds(r, tm // 4, stride=4), :]
                        o4_ref[r, :, cs] = v4.astype(BF16)
                        slab4[k, r] = v4
                        for a in range(4):
                            o16_ref[r + 4 * a, :, cs] = (
                                slab4[k, r, pl.ds(a, tm // 16, stride=4), :].astype(BF16))

    is_rot = (j == SEG_QA) | (j == SEG_KA) | (j == SEG_QR) | (j == SEG_KR)
    is_attn = j <= SEG_VA
    for rot in (True, False):
        for strided in (True, False):
            cond = (is_rot if rot else jnp.logical_not(is_rot)) & (
                is_attn if strided else jnp.logical_not(is_attn))
            pl.when(cond)(functools.partial(segment, rot, strided))


def _in_proj(x2d, w_in, cos, sin, batch, seq):
    t = x2d.shape[0]
    tm = IN_PROJ_ROWS
    n_s = seq // tm
    n_att = SEG_VA + 1

    def strided_out(dil):
        shape = jax.ShapeDtypeStruct((n_att, batch, dil, seq // dil, SEG), BF16)
        spec = pl.BlockSpec((None, None, dil, tm // dil, SEG),
                            lambda i, j: (jnp.minimum(j, SEG_VA), i // n_s, 0, i % n_s, 0))
        return shape, spec

    (shape4, spec4), (shape16, spec16) = strided_out(4), strided_out(16)
    return pl.pallas_call(
        functools.partial(_in_proj_kernel, tm=tm),
        grid=(t // tm, N_SEG),
        in_specs=[
            pl.BlockSpec((tm, D_MODEL), lambda i, j: (i, 0)),
            pl.BlockSpec((None, D_MODEL, SEG), lambda i, j: (j, 0, 0)),
            pl.BlockSpec((tm, HEAD_DIM), lambda i, j: (i % n_s, 0)),
            pl.BlockSpec((tm, HEAD_DIM), lambda i, j: (i % n_s, 0)),
        ],
        out_specs=[pl.BlockSpec((None, tm, SEG), lambda i, j: (j, i, 0)), spec4, spec16],
        out_shape=[jax.ShapeDtypeStruct((N_SEG, t, SEG), BF16), shape4, shape16],
        scratch_shapes=[pltpu.VMEM((tm, D_MODEL), BF16),
                        pltpu.VMEM((N_SLABS, tm, HEAD_DIM), F32),
                        pltpu.VMEM((N_SLABS, 4, tm // 4, HEAD_DIM), F32)],
        compiler_params=pltpu.CompilerParams(dimension_semantics=("arbitrary", "arbitrary"),
                                             vmem_limit_bytes=IN_PROJ_VMEM),
        name="in_proj",
    )(x2d, w_in, cos, sin)


def _local_attn_kernel(q_ref, kp_ref, kc_ref, kn_ref, vp_ref, vc_ref, vn_ref,
                       o_ref, lse_ref, kbuf, vbuf, *, lt, sub_len):
    i = pl.program_id(2)
    hw = HALF_WIN
    kbuf[0:hw, :] = kp_ref[...]
    kbuf[hw:hw + lt, :] = kc_ref[...]
    kbuf[hw + lt:2 * hw + lt, :] = kn_ref[...]
    vbuf[0:hw, :] = vp_ref[...]
    vbuf[hw:hw + lt, :] = vc_ref[...]
    vbuf[hw + lt:2 * hw + lt, :] = vn_ref[...]

    qb = 128
    kb = qb + 2 * hw
    row = lax.broadcasted_iota(jnp.int32, (qb, kb), 0)
    col = lax.broadcasted_iota(jnp.int32, (qb, kb), 1)
    band = jnp.abs(row + hw - col) <= hw
    lane = lax.broadcasted_iota(jnp.int32, (qb, LANES), 1)

    def body(j, carry):
        r0 = pl.multiple_of(j * qb, qb)
        kpos = i * lt + r0 - hw + col
        bias = jnp.where(band, 0.0, NEG_INF).astype(F32)
        bias = jnp.where(kpos >= 0, bias, NEG_INF)
        bias = jnp.where(kpos < sub_len, bias, NEG_INF)
        lse_tile = jnp.zeros((qb, LANES), F32)
        for h in range(N_HEADS):
            cs = slice(h * HEAD_DIM, (h + 1) * HEAD_DIM)
            q = q_ref[pl.ds(r0, qb), cs]
            k = kbuf[pl.ds(r0, kb), cs]
            v = vbuf[pl.ds(r0, kb), cs]
            s = _dot_nt(q, k) * ATTN_SCALE + bias
            m = jnp.max(s, axis=-1, keepdims=True)
            p = jnp.exp(s - m)
            l = jnp.sum(p, axis=-1, keepdims=True)
            o = _dot(p.astype(BF16), v) / l
            o_ref[pl.ds(r0, qb), cs] = o.astype(BF16)
            lse_tile = jnp.where(lane == h, m + jnp.log(l), lse_tile)
        lse_ref[pl.ds(r0, qb), :] = lse_tile
        return carry

    lax.fori_loop(0, lt // qb, body, 0)


def _local_attn(qkv, batch, seq, dil):
    sub_len = seq // dil
    lt = min(sub_len, ATTN_ROWS)
    hw = HALF_WIN
    n_halo = sub_len // hw
    per = lt // hw

    def main(seg):
        return pl.BlockSpec((None, None, None, lt, SEG), lambda b, r, i: (seg, b, r, i, 0))

    def prev(seg):
        return pl.BlockSpec((None, None, None, hw, SEG),
                            lambda b, r, i: (seg, b, r, jnp.maximum(i * per - 1, 0), 0))

    def nxt(seg):
        return pl.BlockSpec((None, None, None, hw, SEG),
                            lambda b, r, i: (seg, b, r, jnp.minimum((i + 1) * per, n_halo - 1), 0))

    kern = functools.partial(_local_attn_kernel, lt=lt, sub_len=sub_len)
    return pl.pallas_call(
        kern,
        grid=(batch, dil, sub_len // lt),
        in_specs=[main(SEG_QA), prev(SEG_KA), main(SEG_KA), nxt(SEG_KA),
                  prev(SEG_VA), main(SEG_VA), nxt(SEG_VA)],
        out_specs=[pl.BlockSpec((None, None, lt, SEG), lambda b, r, i: (b, r, i, 0)),
                   pl.BlockSpec((None, None, lt, LANES), lambda b, r, i: (b, r, i, 0))],
        out_shape=[jax.ShapeDtypeStruct((batch, dil, sub_len, SEG), BF16),
                   jax.ShapeDtypeStruct((batch, dil, sub_len, LANES), F32)],
        scratch_shapes=[pltpu.VMEM((lt + 2 * hw, SEG), BF16),
                        pltpu.VMEM((lt + 2 * hw, SEG), BF16)],
        compiler_params=_params("arbitrary", "arbitrary", "arbitrary"),
        name="local_attn_d%d" % dil,
    )(qkv, qkv, qkv, qkv, qkv, qkv, qkv)


def _retention_tables():
    h = np.arange(N_HEADS, dtype=np.float64)
    gf = 1.0 - 2.0 ** (-5.0 - h)
    gb = 1.0 - 2.0 ** (-5.5 - h)
    c = RET_CHUNK
    idx = np.arange(c, dtype=np.float64)
    diff = idx[:, None] - idx[None, :]
    dec_f = np.where(diff >= 0, gf[:, None, None] ** np.maximum(diff, 0.0), 0.0)
    dec_b = np.where(diff < 0, gb[:, None, None] ** np.maximum(-diff, 0.0), 0.0)
    decay = dec_f + dec_b
    rows = lambda t: np.broadcast_to(t[:, :, None], (N_HEADS, c, HEAD_DIM))
    xi_f = rows(gf[:, None] ** (idx + 1.0)[None])
    zeta_f = rows(gf[:, None] ** (c - 1.0 - idx)[None])
    xi_b = rows(gb[:, None] ** (c - idx)[None])
    zeta_b = rows(gb[:, None] ** idx[None])
    f = lambda t: jnp.asarray(np.ascontiguousarray(t), F32)
    return (f(decay), f(xi_f), f(zeta_f), f(xi_b), f(zeta_b),
            tuple(float(g ** c) for g in gf), tuple(float(g ** c) for g in gb))


def _ret_fwd_kernel(q_ref, k_ref, v_ref, dec_ref, xi_ref, zeta_ref, o_ref, state, *, rt, cd):
    @pl.when(pl.program_id(1) == 0)
    def _():
        state[...] = jnp.zeros_like(state)

    for c in range(rt // RET_CHUNK):
        rs = slice(c * RET_CHUNK, (c + 1) * RET_CHUNK)
        for h in range(N_HEADS):
            cs = slice(h * HEAD_DIM, (h + 1) * HEAD_DIM)
            q = q_ref[rs, cs]
            k = k_ref[rs, cs]
            v = v_ref[rs, cs]
            a = _dot_nt(q, k) * dec_ref[h]
            o = _dot(a.astype(BF16), v)
            s_old = state[h]
            o = o + _dot(q, s_old.astype(BF16)) * xi_ref[h]
            kz = (k.astype(F32) * zeta_ref[h]).astype(BF16)
            state[h] = cd[h] * s_old + _dot_tn(kz, v)
            o_ref[rs, cs] = o


def _ret_bwd_kernel(q_ref, k_ref, v_ref, g_ref, r_ref, xi_ref, zeta_ref, gn_ref, o_ref, state,
                    *, rt, cd):
    @pl.when(pl.program_id(1) == 0)
    def _():
        state[...] = jnp.zeros_like(state)

    for c in reversed(range(rt // RET_CHUNK)):
        rs = slice(c * RET_CHUNK, (c + 1) * RET_CHUNK)
        for h in range(N_HEADS):
            cs = slice(h * HEAD_DIM, (h + 1) * HEAD_DIM)
            q = q_ref[rs, cs]
            k = k_ref[rs, cs]
            v = v_ref[rs, cs]
            s_old = state[h]
            r = r_ref[rs, cs] + _dot(q, s_old.astype(BF16)) * xi_ref[h]
            kz = (k.astype(F32) * zeta_ref[h]).astype(BF16)
            state[h] = cd[h] * s_old + _dot_tn(kz, v)
            mu = jnp.mean(r, axis=-1, keepdims=True)
            d = r - mu
            var = jnp.mean(d * d, axis=-1, keepdims=True)
            rn = d * lax.rsqrt(var + LN_EPS) * gn_ref[:, cs]
            g = g_ref[rs, cs].astype(F32)
            silu = g / (1.0 + jnp.exp(-g))
            o_ref[rs, cs] = (silu * rn).astype(BF16)


def _retention(h3, gn_g, batch, seq, tables):
    decay, xi_f, zeta_f, xi_b, zeta_b, cd_f, cd_b = tables
    rt = min(seq, RET_ROWS)
    nr = seq // rt
    tab = pl.BlockSpec((N_HEADS, RET_CHUNK, HEAD_DIM), lambda b, t: (0, 0, 0))
    state = pltpu.VMEM((N_HEADS, HEAD_DIM, HEAD_DIM), F32)

    fwd_seg = lambda seg: pl.BlockSpec((None, None, rt, SEG), lambda b, t: (seg, b, t, 0))
    r_fwd = pl.pallas_call(
        functools.partial(_ret_fwd_kernel, rt=rt, cd=cd_f),
        grid=(batch, nr),
        in_specs=[fwd_seg(SEG_QR), fwd_seg(SEG_KR), fwd_seg(SEG_VR), tab, tab, tab],
        out_specs=pl.BlockSpec((None, rt, SEG), lambda b, t: (b, t, 0)),
        out_shape=jax.ShapeDtypeStruct((batch, seq, SEG), F32),
        scratch_shapes=[state],
        compiler_params=_params("arbitrary", "arbitrary"),
        name="retention_fwd",
    )(h3, h3, h3, decay, xi_f, zeta_f)

    bwd_seg = lambda seg: pl.BlockSpec((None, None, rt, SEG),
                                       lambda b, t: (seg, b, nr - 1 - t, 0))
    return pl.pallas_call(
        functools.partial(_ret_bwd_kernel, rt=rt, cd=cd_b),
        grid=(batch, nr),
        in_specs=[bwd_seg(SEG_QR), bwd_seg(SEG_KR), bwd_seg(SEG_VR), bwd_seg(SEG_GR),
                  pl.BlockSpec((None, rt, SEG), lambda b, t: (b, nr - 1 - t, 0)),
                  tab, tab, pl.BlockSpec((1, SEG), lambda b, t: (0, 0))],
        out_specs=pl.BlockSpec((None, rt, SEG), lambda b, t: (b, nr - 1 - t, 0)),
        out_shape=jax.ShapeDtypeStruct((batch, seq, SEG), BF16),
        scratch_shapes=[state],
        compiler_params=_params("arbitrary", "arbitrary"),
        name="retention_bwd",
    )(h3, h3, h3, h3, r_fwd, xi_b, zeta_b, gn_g)


def _out_proj_kernel(o1_ref, o2_ref, o3_ref, l1_ref, l2_ref, l3_ref, ret_ref, x_ref, w_ref,
                     g_ref, b_ref, out_ref, attn_buf, nat_o, nat_l, *, alpha):
    rows = x_ref.shape[0]
    heads = [slice(h * HEAD_DIM, (h + 1) * HEAD_DIM) for h in range(N_HEADS)]
    for k, (dil, o_ref, l_ref) in enumerate(((4, o2_ref, l2_ref), (16, o3_ref, l3_ref))):
        for r in range(dil):
            dst = pl.ds(r, rows // dil, stride=dil)
            nat_l[k, dst, :] = l_ref[r]
            for h, cs in enumerate(heads):
                nat_o[k, h, dst, :] = o_ref[r, :, cs].astype(F32)
    la, lb, lc = l1_ref[...], nat_l[0], nat_l[1]
    m = jnp.maximum(jnp.maximum(la, lb), lc)
    ea, eb, ec = jnp.exp(la - m), jnp.exp(lb - m), jnp.exp(lc - m)
    inv = 1.0 / (ea + eb + ec)
    wa, wb, wc = ea * inv, eb * inv, ec * inv
    for h, cs in enumerate(heads):
        bc = lambda w: jnp.broadcast_to(w[:, h:h + 1], (rows, HEAD_DIM))
        mix = (bc(wa) * o1_ref[:, cs].astype(F32) + bc(wb) * nat_o[0, h] + bc(wc) * nat_o[1, h])
        attn_buf[:, cs] = mix.astype(BF16)
    y = _dot(attn_buf[...], w_ref[0:SEG, :]) + _dot(ret_ref[...], w_ref[SEG:2 * SEG, :])
    out_ref[...] = _layer_norm(alpha * x_ref[...] + y, g_ref[...], b_ref[...])


def _out_proj(outs, lses, ret_o, x2d, w_out, ln_g, ln_b, seq, alpha):
    t = x2d.shape[0]
    tm = ROW_TILE
    per_b = seq // tm
    row = lambda w: pl.BlockSpec((tm, w), lambda i: (i, 0))
    full = lambda a: pl.BlockSpec(a.shape, lambda i: (0,) * a.ndim)

    def branch(dil, width):
        return pl.BlockSpec((None, dil, tm // dil, width),
                            lambda i: (i // per_b, 0, i % per_b, 0))

    in_specs = ([row(SEG), branch(4, SEG), branch(16, SEG),
                 row(LANES), branch(4, LANES), branch(16, LANES)]
                + [row(SEG), row(D_MODEL), full(w_out), full(ln_g), full(ln_b)])
    return pl.pallas_call(
        functools.partial(_out_proj_kernel, alpha=alpha),
        grid=(t // tm,),
        in_specs=in_specs,
        out_specs=row(D_MODEL),
        out_shape=jax.ShapeDtypeStruct((t, D_MODEL), F32),
        scratch_shapes=[pltpu.VMEM((tm, SEG), BF16),
                        pltpu.VMEM((2, N_HEADS, tm, HEAD_DIM), F32),
                        pltpu.VMEM((2, tm, LANES), F32)],
        compiler_params=_params("arbitrary"),
        name="out_proj_ln1",
    )(outs[0].reshape(t, SEG), outs[1], outs[2], lses[0].reshape(t, LANES), lses[1], lses[2],
      ret_o, x2d, w_out, ln_g, ln_b)


def _mem_kv_kernel(m_ref, w_ref, o_ref):
    o_ref[...] = _dot(m_ref[...].astype(BF16), w_ref[...]).astype(BF16)


def _mem_kv(mem2d, w_mkv):
    rows = mem2d.shape[0]
    tm = ROW_TILE
    return pl.pallas_call(
        _mem_kv_kernel,
        grid=(rows // tm,),
        in_specs=[pl.BlockSpec((tm, D_MODEL), lambda i: (i, 0)),
                  pl.BlockSpec(w_mkv.shape, lambda i: (0, 0))],
        out_specs=pl.BlockSpec((tm, 2 * D_X), lambda i: (i, 0)),
        out_shape=jax.ShapeDtypeStruct((rows, 2 * D_X), BF16),
        compiler_params=_params("arbitrary"),
        name="mem_kv",
    )(mem2d, w_mkv)


def _route(logits):
    shape = logits.shape
    lane_i = lax.broadcasted_iota(jnp.int32, shape, 1)
    lane = lane_i.astype(F32)
    lowest = jnp.float32(-3.0e38)
    none = jnp.float32(LANES)
    rmax = lambda t: jnp.max(t, axis=-1, keepdims=True)
    rmin = lambda t: jnp.min(t, axis=-1, keepdims=True)

    is_group = lane_i < N_GROUPS
    gl = jnp.where(is_group, logits, lowest)
    g_max = rmax(gl)
    g_idx = rmin(jnp.where(gl == g_max, lane, none))
    p_g = 1.0 / jnp.sum(jnp.where(is_group, jnp.exp(logits - g_max), 0.0), axis=-1, keepdims=True)

    e_lo = N_GROUPS + EXP_PER_GROUP * g_idx
    el = jnp.where(lane >= e_lo, logits, lowest)
    el = jnp.where(lane < e_lo + EXP_PER_GROUP, el, lowest)
    v1 = rmax(el)
    i1 = rmin(jnp.where(el == v1, lane, none))
    el2 = jnp.where(lane == i1, lowest, el)
    v2 = rmax(el2)
    i2 = rmin(jnp.where(el2 == v2, lane, none))
    e2 = jnp.exp(v2 - v1)
    g1 = p_g / (1.0 + e2)
    g2 = p_g * e2 / (1.0 + e2)
    out = jnp.where(lane_i == 0, i1 - N_GROUPS,
                    jnp.where(lane_i == 1, i2 - N_GROUPS,
                              jnp.where(lane_i == 2, g1, jnp.where(lane_i == 3, g2, 0.0))))
    return out.astype(F32)


def _xattn_kernel(x_ref, kv_ref, wq_ref, wo_ref, g_ref, b_ref, wrh_ref, wrl_ref, br_ref,
                  x2_ref, x2_rows_ref, route_ref, obuf, *, alpha):
    x = x_ref[...]
    q = _dot(x.astype(BF16), wq_ref[...]).astype(BF16)
    for h in range(X_HEADS):
        cs = slice(h * HEAD_DIM, (h + 1) * HEAD_DIM)
        k = kv_ref[:, cs]
        v = kv_ref[:, D_X + h * HEAD_DIM:D_X + (h + 1) * HEAD_DIM]
        s = _dot_nt(q[:, cs], k) * ATTN_SCALE
        m = jnp.max(s, axis=-1, keepdims=True)
        p = jnp.exp(s - m)
        l = jnp.sum(p, axis=-1, keepdims=True)
        obuf[:, cs] = (_dot(p.astype(BF16), v) / l).astype(BF16)
    y = _dot(obuf[...], wo_ref[...])
    x2 = _layer_norm(alpha * x + y, g_ref[...], b_ref[...])
    x2_ref[...] = x2
    rows = x2.shape[0]
    for s in range(TOKEN_ROWS):
        x2_rows_ref[pl.ds(s, rows, stride=TOKEN_ROWS), :] = x2[:, s * LANES:(s + 1) * LANES]
    xh = x2.astype(BF16)
    xl = (x2 - xh.astype(F32)).astype(BF16)
    logits = (_dot(xh, wrh_ref[...]) + _dot(xl, wrh_ref[...]) + _dot(xh, wrl_ref[...])
              + br_ref[...])
    route_ref[...] = _route(logits)


def _xattn(x1, kv, w_mq, w_mo, ln_g, ln_b, wr_hi, wr_lo, b_r, seq, alpha):
    t = x1.shape[0]
    tm = ROW_TILE
    per_b = seq // tm
    row = lambda w: pl.BlockSpec((tm, w), lambda i: (i, 0))
    full = lambda a: pl.BlockSpec(a.shape, lambda i: (0,) * a.ndim)
    return pl.pallas_call(
        functools.partial(_xattn_kernel, alpha=alpha),
        grid=(t // tm,),
        in_specs=[row(D_MODEL),
                  pl.BlockSpec((MEM_TOKENS, 2 * D_X), lambda i: (i // per_b, 0)),
                  full(w_mq), full(w_mo), full(ln_g), full(ln_b),
                  full(wr_hi), full(wr_lo), full(b_r)],
        out_specs=[row(D_MODEL), pl.BlockSpec((tm * TOKEN_ROWS, LANES), lambda i: (i, 0)),
                   row(LANES)],
        out_shape=[jax.ShapeDtypeStruct((t, D_MODEL), F32),
                   jax.ShapeDtypeStruct((t * TOKEN_ROWS, LANES), F32),
                   jax.ShapeDtypeStruct((t, LANES), F32)],
        scratch_shapes=[pltpu.VMEM((tm, D_X), BF16)],
        compiler_params=_params("arbitrary"),
        name="xattn_ln2_router",
    )(x1, kv, w_mq, w_mo, ln_g, ln_b, wr_hi, wr_lo, b_r)


def _moe_kernel(blk_e_ref, blk_nv_ref, src_hbm, dst_hbm, x_hbm, w13_ref, w2_ref, y_hbm,
                src_smem, dst_smem, xbuf0, xbuf1, ybuf0, ybuf1, xb_ref, hid_ref, sem_src, sem_dst, sem_g,
                sem_s, *, nblk, bm, n_rows):
    del blk_e_ref
    b = pl.program_id(0)
    cur = b % 2
    xbuf = (xbuf0, xbuf1)
    ybuf = (ybuf0, ybuf1)

    def rows_of(blk):
        inside = (blk >= 0) & (blk < nblk)
        return jnp.where(inside, blk_nv_ref[jnp.clip(blk, 0, nblk - 1)], 0)

    nv_prev2, nv_prev, nv_cur, nv_next = rows_of(b - 2), rows_of(b - 1), rows_of(b), rows_of(b + 1)

    def src_copy(blk):
        return pltpu.make_async_copy(src_hbm.at[blk], src_smem.at[blk % 2], sem_src.at[blk % 2])

    def dst_copy(blk):
        return pltpu.make_async_copy(dst_hbm.at[blk], dst_smem.at[blk % 2], sem_dst.at[blk % 2])

    tr, pitch = TOKEN_ROWS, MOE_PITCH

    def slab(buf, r):
        return buf.at[pl.ds(r * pitch, tr)]

    def all_slabs_bytes(buf):
        return buf.at[pl.ds(0, bm * tr)]

    def gather_row(slot, r):
        off = pl.multiple_of(src_smem[slot, r], tr)
        pltpu.make_async_copy(x_hbm.at[pl.ds(off, tr)], slab(xbuf[slot], r),
                              sem_g.at[slot]).start(priority=r % 2)

    def scatter_row(slot, r):
        off = pl.multiple_of(dst_smem[slot, r], tr)
        pltpu.make_async_copy(slab(ybuf[slot], r), y_hbm.at[pl.ds(off, tr)],
                              sem_s.at[slot]).start(priority=r % 2)

    def issue_gathers(slot):
        for r in range(bm):
            gather_row(slot, r)

    def wait_gathers(slot):
        pltpu.make_async_copy(x_hbm.at[pl.ds(0, bm * tr)], all_slabs_bytes(xbuf[slot]),
                              sem_g.at[slot]).wait()

    def issue_scatters(slot):
        for r in range(bm):
            scatter_row(slot, r)

    def wait_scatters(slot):
        pltpu.make_async_copy(all_slabs_bytes(ybuf[slot]), y_hbm.at[pl.ds(0, bm * tr)],
                              sem_s.at[slot]).wait()

    def compute(slot, copies=()):
        copies = list(copies)
        per_piece = -(-len(copies) // MOE_COPY_PIECES)

        def drip():
            for issue in copies[:per_piece]:
                issue()
            del copies[:per_piece]

        nc = 2 * LANES
        for s in range(tr):
            xb_ref[:, s * LANES:(s + 1) * LANES] = (
                xbuf[slot][pl.ds(s, bm, stride=pitch), :].astype(BF16))
            drip()
        for c in range(D_EXPERT // nc):
            h1 = _dot(xb_ref[...], w13_ref[:, c * nc:(c + 1) * nc])
            drip()
            h3 = _dot(xb_ref[...], w13_ref[:, D_EXPERT + c * nc:D_EXPERT + (c + 1) * nc])
            hid_ref[:, c * nc:(c + 1) * nc] = (h1 / (1.0 + jnp.exp(-h1)) * h3).astype(BF16)
            drip()
        for c in range(D_MODEL // nc):
            y = _dot(hid_ref[...], w2_ref[:, c * nc:(c + 1) * nc])
            for k in range(nc // LANES):
                s = c * (nc // LANES) + k
                ybuf[slot][pl.ds(s, bm, stride=pitch), :] = y[:, k * LANES:(k + 1) * LANES]
            drip()
        assert not copies

    @pl.when(b == 0)
    def _():
        ybuf0[...] = jnp.zeros_like(ybuf0)
        spare = pltpu.make_async_copy(all_slabs_bytes(ybuf0),
                                      y_hbm.at[pl.ds(n_rows * tr, bm * tr)], sem_s.at[0])
        spare.start()
        spare.wait()
        src_copy(0).start()
        src_copy(0).wait()

        @pl.when(nv_cur > 0)
        def _():
            issue_gathers(0)
        if nblk > 1:
            src_copy(1).start()

    @pl.when(b >= 1)
    def _():
        dst_copy(b - 1).wait()

    @pl.when(b + 1 < nblk)
    def _():
        src_copy(b + 1).wait()
        dst_copy(b).start()

    @pl.when(b + 2 < nblk)
    def _():
        src_copy(b + 2).start()

    steady = (nv_prev > 0) & (nv_cur > 0) & (nv_next > 0)
    for p in (0, 1):
        q = 1 - p
        mine = cur == p

        @pl.when(mine & (nv_prev2 > 0) & (nv_prev > 0))
        def _():
            wait_scatters(p)

        @pl.when(mine & steady)
        def _():
            wait_gathers(p)
            copies = []
            for r in range(bm):
                copies.append(functools.partial(gather_row, q, r))
                copies.append(functools.partial(scatter_row, q, r))
            compute(p, copies)

        @pl.when(mine & (nv_cur > 0) & jnp.logical_not(steady))
        def _():
            wait_gathers(p)

            @pl.when(nv_next > 0)
            def _():
                issue_gathers(q)

            @pl.when(nv_prev > 0)
            def _():
                issue_scatters(q)
            compute(p)

        @pl.when(mine & (nv_cur == 0) & (nv_prev > 0))
        def _():
            issue_scatters(q)
            wait_scatters(q)


def _moe_dispatch(route, bm):
    t = route.shape[0]
    m = 2 * t
    nblk = m // bm + N_EXPERTS
    eid = route[:, 0:2].astype(jnp.int32).T.reshape(m)
    counts = jnp.sum((eid[:, None] == jnp.arange(N_EXPERTS, dtype=jnp.int32)[None, :]).astype(jnp.int32),
                     axis=0)
    order = jnp.argsort(eid).astype(jnp.int32)
    start = jnp.cumsum(counts) - counts
    nb_e = (counts + bm - 1) // bm
    bend = jnp.cumsum(nb_e)
    bstart = bend - nb_e
    blk = jnp.arange(nblk, dtype=jnp.int32)
    blk_e = jnp.minimum(jnp.sum((blk[:, None] >= bend[None, :]).astype(jnp.int32), axis=1),
                        N_EXPERTS - 1).astype(jnp.int32)
    within = blk - bstart[blk_e]
    blk_nv = jnp.where(blk < bend[-1], jnp.clip(counts[blk_e] - within * bm, 0, bm), 0).astype(jnp.int32)
    row = jnp.arange(bm, dtype=jnp.int32)[None, :]
    sorted_pos = (start[blk_e] + within * bm)[:, None] + row
    asg = order[jnp.clip(sorted_pos, 0, m - 1)]
    valid = row < blk_nv[:, None]
    row_src = jnp.where(valid, jnp.where(asg >= t, asg - t, asg), 0).astype(jnp.int32)
    row_dst = jnp.where(valid, asg, m + row).astype(jnp.int32)
    return blk_e, blk_nv, row_src * TOKEN_ROWS, row_dst * TOKEN_ROWS, nblk


def _moe(x2_rows, route, w13, w2):
    t = route.shape[0]
    bm = MOE_ROWS
    blk_e, blk_nv, row_src, row_dst, nblk = _moe_dispatch(route, bm)
    grid_spec = pltpu.PrefetchScalarGridSpec(
        num_scalar_prefetch=2,
        grid=(nblk,),
        in_specs=[pl.BlockSpec(memory_space=pl.ANY),
                  pl.BlockSpec(memory_space=pl.ANY),
                  pl.BlockSpec(memory_space=pl.ANY),
                  pl.BlockSpec((None, D_MODEL, 2 * D_EXPERT), lambda b, be, nv: (be[b], 0, 0)),
                  pl.BlockSpec((None, D_EXPERT, D_MODEL), lambda b, be, nv: (be[b], 0, 0))],
        out_specs=pl.BlockSpec(memory_space=pl.ANY),
        scratch_shapes=[pltpu.SMEM((2, bm), jnp.int32),
                        pltpu.SMEM((2, bm), jnp.int32),
                        pltpu.VMEM((bm * MOE_PITCH, LANES), F32),
                        pltpu.VMEM((bm * MOE_PITCH, LANES), F32),
                        pltpu.VMEM((bm * MOE_PITCH, LANES), F32),
                        pltpu.VMEM((bm * MOE_PITCH, LANES), F32),
                        pltpu.VMEM((bm, D_MODEL), BF16),
                        pltpu.VMEM((bm, D_EXPERT), BF16),
                        pltpu.SemaphoreType.DMA((2,)),
                        pltpu.SemaphoreType.DMA((2,)),
                        pltpu.SemaphoreType.DMA((2,)),
                        pltpu.SemaphoreType.DMA((2,))],
    )
    return pl.pallas_call(
        functools.partial(_moe_kernel, nblk=nblk, bm=bm, n_rows=2 * t),
        grid_spec=grid_spec,
        out_shape=jax.ShapeDtypeStruct(((2 * t + bm) * TOKEN_ROWS, LANES), F32),
        compiler_params=pltpu.CompilerParams(dimension_semantics=("arbitrary",),
                                             vmem_limit_bytes=VMEM_LIMIT,
                                             disable_bounds_checks=True),
        name="moe_experts",
    )(blk_e, blk_nv, row_src, row_dst, x2_rows, w13, w2)


def _final_kernel(x_ref, y0_ref, y1_ref, route_ref, g_ref, b_ref, o_ref, z_ref, *, alpha):
    r = route_ref[...]
    rows = r.shape[0]
    g0 = jnp.broadcast_to(r[:, 2:3], (rows, LANES))
    g1 = jnp.broadcast_to(r[:, 3:4], (rows, LANES))
    for s in range(TOKEN_ROWS):
        cs = slice(s * LANES, (s + 1) * LANES)
        z_ref[:, cs] = (alpha * x_ref[:, cs]
                        + g0 * y0_ref[pl.ds(s, rows, stride=TOKEN_ROWS), :]
                        + g1 * y1_ref[pl.ds(s, rows, stride=TOKEN_ROWS), :])
    o_ref[...] = _layer_norm(z_ref[...], g_ref[...], b_ref[...])


def _final(x2, y_rows, route, ln_g, ln_b, alpha):
    t = x2.shape[0]
    tm = ROW_TILE
    nt = t // tm
    row = lambda w: pl.BlockSpec((tm, w), lambda i: (i, 0))
    full = lambda a: pl.BlockSpec(a.shape, lambda i: (0,) * a.ndim)
    return pl.pallas_call(
        functools.partial(_final_kernel, alpha=alpha),
        grid=(nt,),
        in_specs=[row(D_MODEL),
                  pl.BlockSpec((tm * TOKEN_ROWS, LANES), lambda i: (i, 0)),
                  pl.BlockSpec((tm * TOKEN_ROWS, LANES), lambda i: (i + nt, 0)),
                  row(LANES), full(ln_g), full(ln_b)],
        out_specs=row(D_MODEL),
        out_shape=jax.ShapeDtypeStruct((t, D_MODEL), F32),
        scratch_shapes=[pltpu.VMEM((tm, D_MODEL), F32)],
        compiler_params=_params("arbitrary"),
        name="moe_combine_ln3",
    )(x2, y_rows, y_rows, route, ln_g, ln_b)


def _rotary_tables(seq):
    half = HEAD_DIM // 2
    inv_freq = ROPE_THETA ** (-jnp.arange(half, dtype=F32) / half)
    ang = jnp.arange(seq, dtype=F32)[:, None] * inv_freq[None, :]
    cos, sin = jnp.cos(ang), jnp.sin(ang)
    return jnp.concatenate([cos, cos], axis=-1), jnp.concatenate([-sin, sin], axis=-1)


def _prepare_weights(l, w_in, ret_gn_g, w_out, ln1_g, ln1_b, w_mq, w_mkv, w_mo, ln2_g, ln2_b,
                     w_gr, b_gr, w_er, b_er, w1, w3, w2, ln3_g, ln3_b):
    row = lambda v: v[l].reshape(1, -1).astype(F32)
    pad = LANES - N_GROUPS - N_EXPERTS
    w_r = jnp.concatenate([w_gr[l], w_er[l], jnp.zeros((D_MODEL, pad), F32)], axis=1)
    wr_hi = w_r.astype(BF16)
    wr_lo = (w_r - wr_hi.astype(F32)).astype(BF16)
    b_r = jnp.concatenate([b_gr[l], b_er[l], jnp.zeros((pad,), F32)]).reshape(1, LANES)
    return dict(
        w_in=w_in[l].astype(BF16).reshape(D_MODEL, N_SEG, SEG).transpose(1, 0, 2), gn_g=row(ret_gn_g), w_out=w_out[l].astype(BF16),
        ln1=(row(ln1_g), row(ln1_b)), w_mq=w_mq[l].astype(BF16), w_mkv=w_mkv[l].astype(BF16),
        w_mo=w_mo[l].astype(BF16), ln2=(row(ln2_g), row(ln2_b)),
        wr_hi=wr_hi, wr_lo=wr_lo, b_r=b_r,
        w13=jnp.concatenate([w1[l], w3[l]], axis=-1).astype(BF16), w2=w2[l].astype(BF16),
        ln3=(row(ln3_g), row(ln3_b)))


def _encoder_layer(x, mem, w, alpha, ret_tables):
    batch, seq, _ = x.shape
    t = batch * seq
    x2d = x.reshape(t, D_MODEL)
    cos, sin = _rotary_tables(seq)
    h, ha4, ha16 = _in_proj(x2d, w['w_in'], cos, sin, batch, seq)
    outs, lses = [], []
    for dil, qkv in zip(DILATIONS, (h.reshape(N_SEG, batch, 1, seq, SEG), ha4, ha16)):
        o, lse = _local_attn(qkv, batch, seq, dil)
        outs.append(o)
        lses.append(lse)
    ret_o = _retention(h.reshape(N_SEG, batch, seq, SEG), w['gn_g'], batch, seq, ret_tables)
    x1 = _out_proj(outs, lses, ret_o.reshape(t, SEG), x2d, w['w_out'], *w['ln1'], seq, alpha)
    kv = _mem_kv(mem.reshape(batch * MEM_TOKENS, D_MODEL), w['w_mkv'])
    x2, x2_rows, route = _xattn(x1, kv, w['w_mq'], w['w_mo'], *w['ln2'], w['wr_hi'], w['wr_lo'],
                                w['b_r'], seq, alpha)
    y_rows = _moe(x2_rows, route, w['w13'], w['w2'])
    out = _final(x2, y_rows, route, *w['ln3'], alpha)
    return out.reshape(batch, seq, D_MODEL)


def kernel(x_prompt, x_sample, mem_prompt, mem_sample, w_in, ret_gn_g, w_out, ln1_g, ln1_b,
           w_mq, w_mkv, w_mo, ln2_g, ln2_b, w_gr, b_gr, w_er, b_er, w1, w3, w2, ln3_g, ln3_b):
    depth = w_in.shape[0]
    alpha = (2 * depth) ** 0.25
    ret_tables = _retention_tables()
    y_prompt, y_sample = x_prompt, x_sample
    for l in range(depth):
        w = _prepare_weights(l, w_in, ret_gn_g, w_out, ln1_g, ln1_b, w_mq, w_mkv, w_mo, ln2_g,
                             ln2_b, w_gr, b_gr, w_er, b_er, w1, w3, w2, ln3_g, ln3_b)
        y_prompt = _encoder_layer(y_prompt, mem_prompt, w, alpha, ret_tables)
        y_sample = _encoder_layer(y_sample, mem_sample, w, alpha, ret_tables)
    return (y_prompt, y_sample)
```

```python
import functools

import numpy as np
import jax
import jax.numpy as jnp
from jax import lax
from jax.experimental import pallas as pl
from jax.experimental.pallas import tpu as pltpu

F32 = jnp.float32
BF16 = jnp.bfloat16

D_MODEL = 2048
HEAD_DIM = 128
N_HEADS = 8
SEG = N_HEADS * HEAD_DIM
N_SEG = 7
SEG_QA, SEG_KA, SEG_VA, SEG_QR, SEG_KR, SEG_VR, SEG_GR = range(7)
DILATIONS = (1, 4, 16)
HALF_WIN = 64
ROPE_THETA = 10000.0
RET_CHUNK = 128
MEM_TOKENS = 256
X_HEADS = 4
D_X = X_HEADS * HEAD_DIM
N_GROUPS = 4
EXP_PER_GROUP = 8
N_EXPERTS = N_GROUPS * EXP_PER_GROUP
D_EXPERT = D_MODEL // 4
LN_EPS = 1e-5
NEG_INF = -1e30
ATTN_SCALE = HEAD_DIM ** -0.5

LANES = 128
VMEM_LIMIT = 52 * 1024 * 1024

IN_PROJ_ROWS = 1024
IN_PROJ_VMEM = 58 * 1024 * 1024
N_SLABS = 4
TOKEN_ROWS = D_MODEL // LANES
MOE_PITCH = 24
MOE_COPY_PIECES = 20
ATTN_ROWS = 512
RET_ROWS = 512
ROW_TILE = 256
XATTN_ROWS = 512
MOE_ROWS = 256


def _params(*sem):
    return pltpu.CompilerParams(dimension_semantics=sem, vmem_limit_bytes=VMEM_LIMIT)


def _layer_norm(z, g, b):
    mu = jnp.mean(z, axis=-1, keepdims=True)
    d = z - mu
    var = jnp.mean(d * d, axis=-1, keepdims=True)
    return d * lax.rsqrt(var + LN_EPS) * g + b


def _dot_nt(a, b):
    return lax.dot_general(a, b, (((1,), (1,)), ((), ())), preferred_element_type=F32)


def _dot_tn(a, b):
    return lax.dot_general(a, b, (((0,), (0,)), ((), ())), preferred_element_type=F32)


def _dot(a, b):
    return jnp.dot(a, b, preferred_element_type=F32)


def _in_proj_kernel(x_ref, w_ref, cos_ref, sin_ref, o_ref, o4_ref, o16_ref, xb_ref, slab, slab4,
                    *, tm):
    j = pl.program_id(1)

    @pl.when(j == 0)
    def _():
        xb_ref[...] = x_ref[...].astype(BF16)

    def segment(rot, strided):
        if rot:
            scale = jnp.where(j == SEG_KR, ATTN_SCALE, 1.0).astype(F32)
            c = cos_ref[...] * scale
            s = sin_ref[...] * scale
        for pair in range(N_HEADS // 2):
            acc = _dot(xb_ref[...], w_ref[:, pair * 2 * HEAD_DIM:(pair + 1) * 2 * HEAD_DIM])
            for hh in range(2):
                h = 2 * pair + hh
                cs = slice(h * HEAD_DIM, (h + 1) * HEAD_DIM)
                t = acc[:, hh * HEAD_DIM:(hh + 1) * HEAD_DIM]
                if rot:
                    t = t * c + pltpu.roll(t, HEAD_DIM // 2, 1) * s
                o_ref[:, cs] = t.astype(BF16)
                if strided:
                    k = h % N_SLABS
                    slab[k] = t
                    for r in range(4):
                        v4 = slab[k, pl.ds(r, tm // 4, stride=4), :]
                        o4_ref[r, :, cs] = v4.astype(BF16)
                        slab4[k, r] = v4
                        for a in range(4):
                            o16_ref[r + 4 * a, :, cs] = (
                                slab4[k, r, pl.ds(a, tm // 16, stride=4), :].astype(BF16))

    is_rot = (j == SEG_QA) | (j == SEG_KA) | (j == SEG_QR) | (j == SEG_KR)
    is_attn = j <= SEG_VA
    for rot in (True, False):
        for strided in (True, False):
            cond = (is_rot if rot else jnp.logical_not(is_rot)) & (
                is_attn if strided else jnp.logical_not(is_attn))
            pl.when(cond)(functools.partial(segment, rot, strided))


def _in_proj(x2d, w_in, cos, sin, batch, seq):
    t = x2d.shape[0]
    tm = IN_PROJ_ROWS
    n_s = seq // tm
    n_att = SEG_VA + 1

    def strided_out(dil):
        shape = jax.ShapeDtypeStruct((n_att, batch, dil, seq // dil, SEG), BF16)
        spec = pl.BlockSpec((None, None, dil, tm // dil, SEG),
                            lambda i, j: (jnp.minimum(j, SEG_VA), i // n_s, 0, i % n_s, 0))
        return shape, spec

    (shape4, spec4), (shape16, spec16) = strided_out(4), strided_out(16)
    return pl.pallas_call(
        functools.partial(_in_proj_kernel, tm=tm),
        grid=(t // tm, N_SEG),
        in_specs=[
            pl.BlockSpec((tm, D_MODEL), lambda i, j: (i, 0)),
            pl.BlockSpec((None, D_MODEL, SEG), lambda i, j: (j, 0, 0)),
            pl.BlockSpec((tm, HEAD_DIM), lambda i, j: (i % n_s, 0)),
            pl.BlockSpec((tm, HEAD_DIM), lambda i, j: (i % n_s, 0)),
        ],
        out_specs=[pl.BlockSpec((None, tm, SEG), lambda i, j: (j, i, 0)), spec4, spec16],
        out_shape=[jax.ShapeDtypeStruct((N_SEG, t, SEG), BF16), shape4, shape16],
        scratch_shapes=[pltpu.VMEM((tm, D_MODEL), BF16),
                        pltpu.VMEM((N_SLABS, tm, HEAD_DIM), F32),
                        pltpu.VMEM((N_SLABS, 4, tm // 4, HEAD_DIM), F32)],
        compiler_params=pltpu.CompilerParams(dimension_semantics=("arbitrary", "arbitrary"),
                                             vmem_limit_bytes=IN_PROJ_VMEM),
        name="in_proj",
    )(x2d, w_in, cos, sin)


def _local_attn_kernel(q_ref, kp_ref, kc_ref, kn_ref, vp_ref, vc_ref, vn_ref,
                       o_ref, lse_ref, kbuf, vbuf, *, lt, sub_len):
    i = pl.program_id(2)
    hw = HALF_WIN
    kbuf[0:hw, :] = kp_ref[...]
    kbuf[hw:hw + lt, :] = kc_ref[...]
    kbuf[hw + lt:2 * hw + lt, :] = kn_ref[...]
    vbuf[0:hw, :] = vp_ref[...]
    vbuf[hw:hw + lt, :] = vc_ref[...]
    vbuf[hw + lt:2 * hw + lt, :] = vn_ref[...]

    qb = 128
    kb = qb + 2 * hw
    row = lax.broadcasted_iota(jnp.int32, (qb, kb), 0)
    col = lax.broadcasted_iota(jnp.int32, (qb, kb), 1)
    band = jnp.abs(row + hw - col) <= hw
    lane = lax.broadcasted_iota(jnp.int32, (qb, LANES), 1)

    def body(j, carry):
        r0 = pl.multiple_of(j * qb, qb)
        kpos = i * lt + r0 - hw + col
        bias = jnp.where(band, 0.0, NEG_INF).astype(F32)
        bias = jnp.where(kpos >= 0, bias, NEG_INF)
        bias = jnp.where(kpos < sub_len, bias, NEG_INF)
        lse_tile = jnp.zeros((qb, LANES), F32)
        for h in range(N_HEADS):
            cs = slice(h * HEAD_DIM, (h + 1) * HEAD_DIM)
            q = q_ref[pl.ds(r0, qb), cs]
            k = kbuf[pl.ds(r0, kb), cs]
            v = vbuf[pl.ds(r0, kb), cs]
            s = _dot_nt(q, k) * ATTN_SCALE + bias
            m = jnp.max(s, axis=-1, keepdims=True)
            p = jnp.exp(s - m)
            l = jnp.sum(p, axis=-1, keepdims=True)
            o = _dot(p.astype(BF16), v) / l
            o_ref[pl.ds(r0, qb), cs] = o.astype(BF16)
            lse_tile = jnp.where(lane == h, m + jnp.log(l), lse_tile)
        lse_ref[pl.ds(r0, qb), :] = lse_tile
        return carry

    lax.fori_loop(0, lt // qb, body, 0)


def _local_attn(qkv, batch, seq, dil):
    sub_len = seq // dil
    lt = min(sub_len, ATTN_ROWS)
    hw = HALF_WIN
    n_halo = sub_len // hw
    per = lt // hw

    def main(seg):
        return pl.BlockSpec((None, None, None, lt, SEG), lambda b, r, i: (seg, b, r, i, 0))

    def prev(seg):
        return pl.BlockSpec((None, None, None, hw, SEG),
                            lambda b, r, i: (seg, b, r, jnp.maximum(i * per - 1, 0), 0))

    def nxt(seg):
        return pl.BlockSpec((None, None, None, hw, SEG),
                            lambda b, r, i: (seg, b, r, jnp.minimum((i + 1) * per, n_halo - 1), 0))

    kern = functools.partial(_local_attn_kernel, lt=lt, sub_len=sub_len)
    return pl.pallas_call(
        kern,
        grid=(batch, dil, sub_len // lt),
        in_specs=[main(SEG_QA), prev(SEG_KA), main(SEG_KA), nxt(SEG_KA),
                  prev(SEG_VA), main(SEG_VA), nxt(SEG_VA)],
        out_specs=[pl.BlockSpec((None, None, lt, SEG), lambda b, r, i: (b, r, i, 0)),
                   pl.BlockSpec((None, None, lt, LANES), lambda b, r, i: (b, r, i, 0))],
        out_shape=[jax.ShapeDtypeStruct((batch, dil, sub_len, SEG), BF16),
                   jax.ShapeDtypeStruct((batch, dil, sub_len, LANES), F32)],
        scratch_shapes=[pltpu.VMEM((lt + 2 * hw, SEG), BF16),
                        pltpu.VMEM((lt + 2 * hw, SEG), BF16)],
        compiler_params=_params("arbitrary", "arbitrary", "arbitrary"),
        name="local_attn_d%d" % dil,
    )(qkv, qkv, qkv, qkv, qkv, qkv, qkv)


def _retention_tables():
    h = np.arange(N_HEADS, dtype=np.float64)
    gf = 1.0 - 2.0 ** (-5.0 - h)
    gb = 1.0 - 2.0 ** (-5.5 - h)
    c = RET_CHUNK
    idx = np.arange(c, dtype=np.float64)
    diff = idx[:, None] - idx[None, :]
    dec_f = np.where(diff >= 0, gf[:, None, None] ** np.maximum(diff, 0.0), 0.0)
    dec_b = np.where(diff < 0, gb[:, None, None] ** np.maximum(-diff, 0.0), 0.0)
    decay = dec_f + dec_b
    rows = lambda t: np.broadcast_to(t[:, :, None], (N_HEADS, c, HEAD_DIM))
    xi_f = rows(gf[:, None] ** (idx + 1.0)[None])
    zeta_f = rows(gf[:, None] ** (c - 1.0 - idx)[None])
    xi_b = rows(gb[:, None] ** (c - idx)[None])
    zeta_b = rows(gb[:, None] ** idx[None])
    f = lambda t: jnp.asarray(np.ascontiguousarray(t), F32)
    return (f(decay), f(xi_f), f(zeta_f), f(xi_b), f(zeta_b),
            tuple(float(g ** c) for g in gf), tuple(float(g ** c) for g in gb))


def _ret_fwd_kernel(q_ref, k_ref, v_ref, dec_ref, xi_ref, zeta_ref, o_ref, state, *, rt, cd):
    @pl.when(pl.program_id(1) == 0)
    def _():
        state[...] = jnp.zeros_like(state)

    for c in range(rt // RET_CHUNK):
        rs = slice(c * RET_CHUNK, (c + 1) * RET_CHUNK)
        for h in range(N_HEADS):
            cs = slice(h * HEAD_DIM, (h + 1) * HEAD_DIM)
            q = q_ref[rs, cs]
            k = k_ref[rs, cs]
            v = v_ref[rs, cs]
            a = _dot_nt(q, k) * dec_ref[h]
            o = _dot(a.astype(BF16), v)
            s_old = state[h]
            o = o + _dot(q, s_old.astype(BF16)) * xi_ref[h]
            kz = (k.astype(F32) * zeta_ref[h]).astype(BF16)
            state[h] = cd[h] * s_old + _dot_tn(kz, v)
            o_ref[rs, cs] = o


def _ret_bwd_kernel(q_ref, k_ref, v_ref, g_ref, r_ref, xi_ref, zeta_ref, gn_ref, o_ref, state,
                    *, rt, cd):
    @pl.when(pl.program_id(1) == 0)
    def _():
        state[...] = jnp.zeros_like(state)

    for c in reversed(range(rt // RET_CHUNK)):
        rs = slice(c * RET_CHUNK, (c + 1) * RET_CHUNK)
        for h in range(N_HEADS):
            cs = slice(h * HEAD_DIM, (h + 1) * HEAD_DIM)
            q = q_ref[rs, cs]
            k = k_ref[rs, cs]
            v = v_ref[rs, cs]
            s_old = state[h]
            r = r_ref[rs, cs] + _dot(q, s_old.astype(BF16)) * xi_ref[h]
            kz = (k.astype(F32) * zeta_ref[h]).astype(BF16)
            state[h] = cd[h] * s_old + _dot_tn(kz, v)
            mu = jnp.mean(r, axis=-1, keepdims=True)
            d = r - mu
            var = jnp.mean(d * d, axis=-1, keepdims=True)
            rn = d * lax.rsqrt(var + LN_EPS) * gn_ref[:, cs]
            g = g_ref[rs, cs].astype(F32)
            silu = g / (1.0 + jnp.exp(-g))
            o_ref[rs, cs] = (silu * rn).astype(BF16)


def _retention(h3, gn_g, batch, seq, tables):
    decay, xi_f, zeta_f, xi_b, zeta_b, cd_f, cd_b = tables
    rt = min(seq, RET_ROWS)
    nr = seq // rt
    tab = pl.BlockSpec((N_HEADS, RET_CHUNK, HEAD_DIM), lambda b, t: (0, 0, 0))
    state = pltpu.VMEM((N_HEADS, HEAD_DIM, HEAD_DIM), F32)

    fwd_seg = lambda seg: pl.BlockSpec((None, None, rt, SEG), lambda b, t: (seg, b, t, 0))
    r_fwd = pl.pallas_call(
        functools.partial(_ret_fwd_kernel, rt=rt, cd=cd_f),
        grid=(batch, nr),
        in_specs=[fwd_seg(SEG_QR), fwd_seg(SEG_KR), fwd_seg(SEG_VR), tab, tab, tab],
        out_specs=pl.BlockSpec((None, rt, SEG), lambda b, t: (b, t, 0)),
        out_shape=jax.ShapeDtypeStruct((batch, seq, SEG), F32),
        scratch_shapes=[state],
        compiler_params=_params("arbitrary", "arbitrary"),
        name="retention_fwd",
    )(h3, h3, h3, decay, xi_f, zeta_f)

    bwd_seg = lambda seg: pl.BlockSpec((None, None, rt, SEG),
                                       lambda b, t: (seg, b, nr - 1 - t, 0))
    return pl.pallas_call(
        functools.partial(_ret_bwd_kernel, rt=rt, cd=cd_b),
        grid=(batch, nr),
        in_specs=[bwd_seg(SEG_QR), bwd_seg(SEG_KR), bwd_seg(SEG_VR), bwd_seg(SEG_GR),
                  pl.BlockSpec((None, rt, SEG), lambda b, t: (b, nr - 1 - t, 0)),
                  tab, tab, pl.BlockSpec((1, SEG), lambda b, t: (0, 0))],
        out_specs=pl.BlockSpec((None, rt, SEG), lambda b, t: (b, nr - 1 - t, 0)),
        out_shape=jax.ShapeDtypeStruct((batch, seq, SEG), BF16),
        scratch_shapes=[state],
        compiler_params=_params("arbitrary", "arbitrary"),
        name="retention_bwd",
    )(h3, h3, h3, h3, r_fwd, xi_b, zeta_b, gn_g)


def _out_proj_kernel(o1_ref, o2_ref, o3_ref, l1_ref, l2_ref, l3_ref, ret_ref, x_ref, w_ref,
                     g_ref, b_ref, out_ref, attn_buf, nat_o, nat_l, *, alpha):
    rows = x_ref.shape[0]
    heads = [slice(h * HEAD_DIM, (h + 1) * HEAD_DIM) for h in range(N_HEADS)]
    for k, (dil, o_ref, l_ref) in enumerate(((4, o2_ref, l2_ref), (16, o3_ref, l3_ref))):
        for r in range(dil):
            dst = pl.ds(r, rows // dil, stride=dil)
            nat_l[k, dst, :] = l_ref[r]
            for h, cs in enumerate(heads):
                nat_o[k, h, dst, :] = o_ref[r, :, cs].astype(F32)
    la, lb, lc = l1_ref[...], nat_l[0], nat_l[1]
    m = jnp.maximum(jnp.maximum(la, lb), lc)
    ea, eb, ec = jnp.exp(la - m), jnp.exp(lb - m), jnp.exp(lc - m)
    inv = 1.0 / (ea + eb + ec)
    wa, wb, wc = ea * inv, eb * inv, ec * inv
    for h, cs in enumerate(heads):
        bc = lambda w: jnp.broadcast_to(w[:, h:h + 1], (rows, HEAD_DIM))
        mix = (bc(wa) * o1_ref[:, cs].astype(F32) + bc(wb) * nat_o[0, h] + bc(wc) * nat_o[1, h])
        attn_buf[:, cs] = mix.astype(BF16)
    y = _dot(attn_buf[...], w_ref[0:SEG, :]) + _dot(ret_ref[...], w_ref[SEG:2 * SEG, :])
    out_ref[...] = _layer_norm(alpha * x_ref[...] + y, g_ref[...], b_ref[...])


def _out_proj(outs, lses, ret_o, x2d, w_out, ln_g, ln_b, seq, alpha):
    t = x2d.shape[0]
    tm = ROW_TILE
    per_b = seq // tm
    row = lambda w: pl.BlockSpec((tm, w), lambda i: (i, 0))
    full = lambda a: pl.BlockSpec(a.shape, lambda i: (0,) * a.ndim)

    def branch(dil, width):
        return pl.BlockSpec((None, dil, tm // dil, width),
                            lambda i: (i // per_b, 0, i % per_b, 0))

    in_specs = ([row(SEG), branch(4, SEG), branch(16, SEG),
                 row(LANES), branch(4, LANES), branch(16, LANES)]
                + [row(SEG), row(D_MODEL), full(w_out), full(ln_g), full(ln_b)])
    return pl.pallas_call(
        functools.partial(_out_proj_kernel, alpha=alpha),
        grid=(t // tm,),
        in_specs=in_specs,
        out_specs=row(D_MODEL),
        out_shape=jax.ShapeDtypeStruct((t, D_MODEL), F32),
        scratch_shapes=[pltpu.VMEM((tm, SEG), BF16),
                        pltpu.VMEM((2, N_HEADS, tm, HEAD_DIM), F32),
                        pltpu.VMEM((2, tm, LANES), F32)],
        compiler_params=_params("arbitrary"),
        name="out_proj_ln1",
    )(outs[0].reshape(t, SEG), outs[1], outs[2], lses[0].reshape(t, LANES), lses[1], lses[2],
      ret_o, x2d, w_out, ln_g, ln_b)


def _mem_kv_kernel(m_ref, w_ref, o_ref):
    o_ref[...] = _dot(m_ref[...].astype(BF16), w_ref[...]).astype(BF16)


def _mem_kv(mem2d, w_mkv):
    rows = mem2d.shape[0]
    tm = ROW_TILE
    return pl.pallas_call(
        _mem_kv_kernel,
        grid=(rows // tm,),
        in_specs=[pl.BlockSpec((tm, D_MODEL), lambda i: (i, 0)),
                  pl.BlockSpec(w_mkv.shape, lambda i: (0, 0))],
        out_specs=pl.BlockSpec((tm, 2 * D_X), lambda i: (i, 0)),
        out_shape=jax.ShapeDtypeStruct((rows, 2 * D_X), BF16),
        compiler_params=_params("arbitrary"),
        name="mem_kv",
    )(mem2d, w_mkv)


def _route(logits):
    shape = logits.shape
    lane_i = lax.broadcasted_iota(jnp.int32, shape, 1)
    lane = lane_i.astype(F32)
    lowest = jnp.float32(-3.0e38)
    none = jnp.float32(LANES)
    rmax = lambda t: jnp.max(t, axis=-1, keepdims=True)
    rmin = lambda t: jnp.min(t, axis=-1, keepdims=True)

    is_group = lane_i < N_GROUPS
    gl = jnp.where(is_group, logits, lowest)
    g_max = rmax(gl)
    g_idx = rmin(jnp.where(gl == g_max, lane, none))
    p_g = 1.0 / jnp.sum(jnp.where(is_group, jnp.exp(logits - g_max), 0.0), axis=-1, keepdims=True)

    e_lo = N_GROUPS + EXP_PER_GROUP * g_idx
    el = jnp.where(lane >= e_lo, logits, lowest)
    el = jnp.where(lane < e_lo + EXP_PER_GROUP, el, lowest)
    v1 = rmax(el)
    i1 = rmin(jnp.where(el == v1, lane, none))
    el2 = jnp.where(lane == i1, lowest, el)
    v2 = rmax(el2)
    i2 = rmin(jnp.where(el2 == v2, lane, none))
    e2 = jnp.exp(v2 - v1)
    g1 = p_g / (1.0 + e2)
    g2 = p_g * e2 / (1.0 + e2)
    out = jnp.where(lane_i == 0, i1 - N_GROUPS,
                    jnp.where(lane_i == 1, i2 - N_GROUPS,
                              jnp.where(lane_i == 2, g1, jnp.where(lane_i == 3, g2, 0.0))))
    return out.astype(F32)


def _xattn_kernel(x_ref, kv_ref, wq_ref, wo_ref, g_ref, b_ref, wr_ref, br_ref,
                  x2_rows_ref, route_ref, obuf, *, alpha):
    sub = ROW_TILE
    for part in range(x_ref.shape[0] // sub):
        rs = slice(part * sub, (part + 1) * sub)
        x = x_ref[rs, :]
        q = _dot(x.astype(BF16), wq_ref[...]).astype(BF16)
        for h in range(X_HEADS):
            cs = slice(h * HEAD_DIM, (h + 1) * HEAD_DIM)
            k = kv_ref[:, cs]
            v = kv_ref[:, D_X + h * HEAD_DIM:D_X + (h + 1) * HEAD_DIM]
            s = _dot_nt(q[:, cs], k) * ATTN_SCALE
            m = jnp.max(s, axis=-1, keepdims=True)
            p = jnp.exp(s - m)
            l = jnp.sum(p, axis=-1, keepdims=True)
            obuf[rs, cs] = (_dot(p.astype(BF16), v) / l).astype(BF16)
        y = _dot(obuf[rs, :], wo_ref[...])
        x2 = _layer_norm(alpha * x + y, g_ref[...], b_ref[...])
        for s in range(TOKEN_ROWS):
            x2_rows_ref[pl.ds(part * sub * TOKEN_ROWS + s, sub, stride=TOKEN_ROWS), :] = (
                x2[:, s * LANES:(s + 1) * LANES])
        xh = x2.astype(BF16)
        xl = (x2 - xh.astype(F32)).astype(BF16)
        hw = _dot(xh, wr_ref[...])
        logits = hw[:, :LANES] + hw[:, LANES:] + _dot(xl, wr_ref[:, :LANES]) + br_ref[...]
        route_ref[rs, :] = _route(logits)


def _xattn(x1, kv, w_mq, w_mo, ln_g, ln_b, w_r, b_r, seq, alpha):
    t = x1.shape[0]
    tm = XATTN_ROWS
    per_b = seq // tm
    row = lambda w: pl.BlockSpec((tm, w), lambda i: (i, 0))
    full = lambda a: pl.BlockSpec(a.shape, lambda i: (0,) * a.ndim)
    return pl.pallas_call(
        functools.partial(_xattn_kernel, alpha=alpha),
        grid=(t // tm,),
        in_specs=[row(D_MODEL),
                  pl.BlockSpec((MEM_TOKENS, 2 * D_X), lambda i: (i // per_b, 0)),
                  full(w_mq), full(w_mo), full(ln_g), full(ln_b),
                  full(w_r), full(b_r)],
        out_specs=[pl.BlockSpec((tm * TOKEN_ROWS, LANES), lambda i: (i, 0)), row(LANES)],
        out_shape=[jax.ShapeDtypeStruct((t * TOKEN_ROWS, LANES), F32),
                   jax.ShapeDtypeStruct((t, LANES), F32)],
        scratch_shapes=[pltpu.VMEM((tm, D_X), BF16)],
        compiler_params=_params("arbitrary"),
        name="xattn_ln2_router",
    )(x1, kv, w_mq, w_mo, ln_g, ln_b, w_r, b_r)


def _moe_kernel(blk_e_ref, blk_nv_ref, src_hbm, dst_hbm, x_hbm, w13_ref, w2_ref, y_hbm,
                src_smem, dst_smem, xbuf0, xbuf1, ybuf0, ybuf1, xb_ref, hid_ref, sem_src, sem_dst, sem_g,
                sem_s, *, nblk, bm, n_rows):
    del blk_e_ref
    b = pl.program_id(0)
    cur = b % 2
    xbuf = (xbuf0, xbuf1)
    ybuf = (ybuf0, ybuf1)

    def rows_of(blk):
        inside = (blk >= 0) & (blk < nblk)
        return jnp.where(inside, blk_nv_ref[jnp.clip(blk, 0, nblk - 1)], 0)

    nv_prev2, nv_prev, nv_cur, nv_next = rows_of(b - 2), rows_of(b - 1), rows_of(b), rows_of(b + 1)

    def src_copy(blk):
        return pltpu.make_async_copy(src_hbm.at[blk], src_smem.at[blk % 2], sem_src.at[blk % 2])

    def dst_copy(blk):
        return pltpu.make_async_copy(dst_hbm.at[blk], dst_smem.at[blk % 2], sem_dst.at[blk % 2])

    tr, pitch = TOKEN_ROWS, MOE_PITCH

    def slab(buf, r):
        return buf.at[pl.ds(r * pitch, tr)]

    def all_slabs_bytes(buf):
        return buf.at[pl.ds(0, bm * tr)]

    def gather_row(slot, r):
        off = pl.multiple_of(src_smem[slot, r], tr)
        pltpu.make_async_copy(x_hbm.at[pl.ds(off, tr)], slab(xbuf[slot], r),
                              sem_g.at[slot]).start(priority=r % 2)

    def scatter_row(slot, r):
        off = pl.multiple_of(dst_smem[slot, r], tr)
        pltpu.make_async_copy(slab(ybuf[slot], r), y_hbm.at[pl.ds(off, tr)],
                              sem_s.at[slot]).start(priority=r % 2)

    def issue_gathers(slot):
        for r in range(bm):
            gather_row(slot, r)

    def wait_gathers(slot):
        pltpu.make_async_copy(x_hbm.at[pl.ds(0, bm * tr)], all_slabs_bytes(xbuf[slot]),
                              sem_g.at[slot]).wait()

    def issue_scatters(slot):
        for r in range(bm):
            scatter_row(slot, r)

    def wait_scatters(slot):
        pltpu.make_async_copy(all_slabs_bytes(ybuf[slot]), y_hbm.at[pl.ds(0, bm * tr)],
                              sem_s.at[slot]).wait()

    def compute(slot, copies=()):
        copies = list(copies)
        per_piece = -(-len(copies) // MOE_COPY_PIECES)

        def drip():
            for issue in copies[:per_piece]:
                issue()
            del copies[:per_piece]

        nc = 2 * LANES
        for s in range(tr):
            xb_ref[:, s * LANES:(s + 1) * LANES] = (
                xbuf[slot][pl.ds(s, bm, stride=pitch), :].astype(BF16))
            drip()
        for c in range(D_EXPERT // nc):
            h1 = _dot(xb_ref[...], w13_ref[:, c * nc:(c + 1) * nc])
            drip()
            h3 = _dot(xb_ref[...], w13_ref[:, D_EXPERT + c * nc:D_EXPERT + (c + 1) * nc])
            hid_ref[:, c * nc:(c + 1) * nc] = (h1 / (1.0 + jnp.exp(-h1)) * h3).astype(BF16)
            drip()
        for c in range(D_MODEL // nc):
            y = _dot(hid_ref[...], w2_ref[:, c * nc:(c + 1) * nc])
            for k in range(nc // LANES):
                s = c * (nc // LANES) + k
                ybuf[slot][pl.ds(s, bm, stride=pitch), :] = y[:, k * LANES:(k + 1) * LANES]
            drip()
        assert not copies

    @pl.when(b == 0)
    def _():
        ybuf0[...] = jnp.zeros_like(ybuf0)
        spare = pltpu.make_async_copy(all_slabs_bytes(ybuf0),
                                      y_hbm.at[pl.ds(n_rows * tr, bm * tr)], sem_s.at[0])
        spare.start()
        spare.wait()
        src_copy(0).start()
        src_copy(0).wait()

        @pl.when(nv_cur > 0)
        def _():
            issue_gathers(0)
        if nblk > 1:
            src_copy(1).start()

    @pl.when(b >= 1)
    def _():
        dst_copy(b - 1).wait()

    @pl.when(b + 1 < nblk)
    def _():
        src_copy(b + 1).wait()
        dst_copy(b).start()

    @pl.when(b + 2 < nblk)
    def _():
        src_copy(b + 2).start()

    steady = (nv_prev > 0) & (nv_cur > 0) & (nv_next > 0)
    for p in (0, 1):
        q = 1 - p
        mine = cur == p

        @pl.when(mine & (nv_prev2 > 0) & (nv_prev > 0))
        def _():
            wait_scatters(p)

        @pl.when(mine & steady)
        def _():
            wait_gathers(p)
            copies = []
            for r in range(bm):
                copies.append(functools.partial(gather_row, q, r))
                copies.append(functools.partial(scatter_row, q, r))
            compute(p, copies)

        @pl.when(mine & (nv_cur > 0) & jnp.logical_not(steady))
        def _():
            wait_gathers(p)

            @pl.when(nv_next > 0)
            def _():
                issue_gathers(q)

            @pl.when(nv_prev > 0)
            def _():
                issue_scatters(q)
            compute(p)

        @pl.when(mine & (nv_cur == 0) & (nv_prev > 0))
        def _():
            issue_scatters(q)
            wait_scatters(q)


def _moe_dispatch(route, bm):
    t = route.shape[0]
    m = 2 * t
    nblk = m // bm + N_EXPERTS
    eid = route[:, 0:2].astype(jnp.int32).T.reshape(m)
    counts = jnp.sum((eid[:, None] == jnp.arange(N_EXPERTS, dtype=jnp.int32)[None, :]).astype(jnp.int32),
                     axis=0)
    order = jnp.argsort(eid).astype(jnp.int32)
    start = jnp.cumsum(counts) - counts
    nb_e = (counts + bm - 1) // bm
    bend = jnp.cumsum(nb_e)
    bstart = bend - nb_e
    blk = jnp.arange(nblk, dtype=jnp.int32)
    blk_e = jnp.minimum(jnp.sum((blk[:, None] >= bend[None, :]).astype(jnp.int32), axis=1),
                        N_EXPERTS - 1).astype(jnp.int32)
    within = blk - bstart[blk_e]
    blk_nv = jnp.where(blk < bend[-1], jnp.clip(counts[blk_e] - within * bm, 0, bm), 0).astype(jnp.int32)
    row = jnp.arange(bm, dtype=jnp.int32)[None, :]
    sorted_pos = (start[blk_e] + within * bm)[:, None] + row
    asg = order[jnp.clip(sorted_pos, 0, m - 1)]
    valid = row < blk_nv[:, None]
    row_src = jnp.where(valid, jnp.where(asg >= t, asg - t, asg), 0).astype(jnp.int32)
    row_dst = jnp.where(valid, asg, m + row).astype(jnp.int32)
    return blk_e, blk_nv, row_src * TOKEN_ROWS, row_dst * TOKEN_ROWS, nblk


def _moe(x2_rows, route, w13, w2):
    t = route.shape[0]
    bm = MOE_ROWS
    blk_e, blk_nv, row_src, row_dst, nblk = _moe_dispatch(route, bm)
    grid_spec = pltpu.PrefetchScalarGridSpec(
        num_scalar_prefetch=2,
        grid=(nblk,),
        in_specs=[pl.BlockSpec(memory_space=pl.ANY),
                  pl.BlockSpec(memory_space=pl.ANY),
                  pl.BlockSpec(memory_space=pl.ANY),
                  pl.BlockSpec((None, D_MODEL, 2 * D_EXPERT), lambda b, be, nv: (be[b], 0, 0)),
                  pl.BlockSpec((None, D_EXPERT, D_MODEL), lambda b, be, nv: (be[b], 0, 0))],
        out_specs=pl.BlockSpec(memory_space=pl.ANY),
        scratch_shapes=[pltpu.SMEM((2, bm), jnp.int32),
                        pltpu.SMEM((2, bm), jnp.int32),
                        pltpu.VMEM((bm * MOE_PITCH, LANES), F32),
                        pltpu.VMEM((bm * MOE_PITCH, LANES), F32),
                        pltpu.VMEM((bm * MOE_PITCH, LANES), F32),
                        pltpu.VMEM((bm * MOE_PITCH, LANES), F32),
                        pltpu.VMEM((bm, D_MODEL), BF16),
                        pltpu.VMEM((bm, D_EXPERT), BF16),
                        pltpu.SemaphoreType.DMA((2,)),
                        pltpu.SemaphoreType.DMA((2,)),
                        pltpu.SemaphoreType.DMA((2,)),
                        pltpu.SemaphoreType.DMA((2,))],
    )
    return pl.pallas_call(
        functools.partial(_moe_kernel, nblk=nblk, bm=bm, n_rows=2 * t),
        grid_spec=grid_spec,
        out_shape=jax.ShapeDtypeStruct(((2 * t + bm) * TOKEN_ROWS, LANES), F32),
        compiler_params=pltpu.CompilerParams(dimension_semantics=("arbitrary",),
                                             vmem_limit_bytes=VMEM_LIMIT,
                                             disable_bounds_checks=True),
        name="moe_experts",
    )(blk_e, blk_nv, row_src, row_dst, x2_rows, w13, w2)


def _final_kernel(x_ref, y0_ref, y1_ref, route_ref, g_ref, b_ref, o_ref, z_ref, *, alpha):
    r = route_ref[...]
    rows = r.shape[0]
    g0 = jnp.broadcast_to(r[:, 2:3], (rows, LANES))
    g1 = jnp.broadcast_to(r[:, 3:4], (rows, LANES))
    for s in range(TOKEN_ROWS):
        cs = slice(s * LANES, (s + 1) * LANES)
        z_ref[:, cs] = (alpha * x_ref[pl.ds(s, rows, stride=TOKEN_ROWS), :]
                        + g0 * y0_ref[pl.ds(s, rows, stride=TOKEN_ROWS), :]
                        + g1 * y1_ref[pl.ds(s, rows, stride=TOKEN_ROWS), :])
    o_ref[...] = _layer_norm(z_ref[...], g_ref[...], b_ref[...])


def _final(x2_rows, y_rows, route, ln_g, ln_b, alpha):
    t = route.shape[0]
    tm = ROW_TILE
    nt = t // tm
    row = lambda w: pl.BlockSpec((tm, w), lambda i: (i, 0))
    full = lambda a: pl.BlockSpec(a.shape, lambda i: (0,) * a.ndim)
    return pl.pallas_call(
        functools.partial(_final_kernel, alpha=alpha),
        grid=(nt,),
        in_specs=[pl.BlockSpec((tm * TOKEN_ROWS, LANES), lambda i: (i, 0)),
                  pl.BlockSpec((tm * TOKEN_ROWS, LANES), lambda i: (i, 0)),
                  pl.BlockSpec((tm * TOKEN_ROWS, LANES), lambda i: (i + nt, 0)),
                  row(LANES), full(ln_g), full(ln_b)],
        out_specs=row(D_MODEL),
        out_shape=jax.ShapeDtypeStruct((t, D_MODEL), F32),
        scratch_shapes=[pltpu.VMEM((tm, D_MODEL), F32)],
        compiler_params=_params("arbitrary"),
        name="moe_combine_ln3",
    )(x2_rows, y_rows, y_rows, route, ln_g, ln_b)


def _rotary_tables(seq):
    half = HEAD_DIM // 2
    inv_freq = ROPE_THETA ** (-jnp.arange(half, dtype=F32) / half)
    ang = jnp.arange(seq, dtype=F32)[:, None] * inv_freq[None, :]
    cos, sin = jnp.cos(ang), jnp.sin(ang)
    return jnp.concatenate([cos, cos], axis=-1), jnp.concatenate([-sin, sin], axis=-1)


def _prepare_weights(l, w_in, ret_gn_g, w_out, ln1_g, ln1_b, w_mq, w_mkv, w_mo, ln2_g, ln2_b,
                     w_gr, b_gr, w_er, b_er, w1, w3, w2, ln3_g, ln3_b):
    row = lambda v: v[l].reshape(1, -1).astype(F32)
    pad = LANES - N_GROUPS - N_EXPERTS
    w_r = jnp.concatenate([w_gr[l], w_er[l], jnp.zeros((D_MODEL, pad), F32)], axis=1)
    wr_hi = w_r.astype(BF16)
    wr_lo = (w_r - wr_hi.astype(F32)).astype(BF16)
    b_r = jnp.concatenate([b_gr[l], b_er[l], jnp.zeros((pad,), F32)]).reshape(1, LANES)
    return dict(
        w_in=w_in[l].astype(BF16).reshape(D_MODEL, N_SEG, SEG).transpose(1, 0, 2), gn_g=row(ret_gn_g), w_out=w_out[l].astype(BF16),
        ln1=(row(ln1_g), row(ln1_b)), w_mq=w_mq[l].astype(BF16), w_mkv=w_mkv[l].astype(BF16),
        w_mo=w_mo[l].astype(BF16), ln2=(row(ln2_g), row(ln2_b)),
        w_r=jnp.concatenate([wr_hi, wr_lo], axis=1), b_r=b_r,
        w13=jnp.concatenate([w1[l], w3[l]], axis=-1).astype(BF16), w2=w2[l].astype(BF16),
        ln3=(row(ln3_g), row(ln3_b)))


def _encoder_layer(x, mem, w, alpha, ret_tables):
    batch, seq, _ = x.shape
    t = batch * seq
    x2d = x.reshape(t, D_MODEL)
    cos, sin = _rotary_tables(seq)
    h, ha4, ha16 = _in_proj(x2d, w['w_in'], cos, sin, batch, seq)
    outs, lses = [], []
    for dil, qkv in zip(DILATIONS, (h.reshape(N_SEG, batch, 1, seq, SEG), ha4, ha16)):
        o, lse = _local_attn(qkv, batch, seq, dil)
        outs.append(o)
        lses.append(lse)
    ret_o = _retention(h.reshape(N_SEG, batch, seq, SEG), w['gn_g'], batch, seq, ret_tables)
    x1 = _out_proj(outs, lses, ret_o.reshape(t, SEG), x2d, w['w_out'], *w['ln1'], seq, alpha)
    kv = _mem_kv(mem.reshape(batch * MEM_TOKENS, D_MODEL), w['w_mkv'])
    x2_rows, route = _xattn(x1, kv, w['w_mq'], w['w_mo'], *w['ln2'], w['w_r'], w['b_r'], seq, alpha)
    y_rows = _moe(x2_rows, route, w['w13'], w['w2'])
    out = _final(x2_rows, y_rows, route, *w['ln3'], alpha)
    return out.reshape(batch, seq, D_MODEL)


def kernel(x_prompt, x_sample, mem_prompt, mem_sample, w_in, ret_gn_g, w_out, ln1_g, ln1_b,
           w_mq, w_mkv, w_mo, ln2_g, ln2_b, w_gr, b_gr, w_er, b_er, w1, w3, w2, ln3_g, ln3_b):
    depth = w_in.shape[0]
    alpha = (2 * depth) ** 0.25
    ret_tables = _retention_tables()
    y_prompt, y_sample = x_prompt, x_sample
    for l in range(depth):
        w = _prepare_weights(l, w_in, ret_gn_g, w_out, ln1_g, ln1_b, w_mq, w_mkv, w_mo, ln2_g,
                             ln2_b, w_gr, b_gr, w_er, b_er, w1, w3, w2, ln3_g, ln3_b)
        y_prompt = _encoder_layer(y_prompt, mem_prompt, w, alpha, ret_tables)
        y_sample = _encoder_layer(y_sample, mem_sample, w, alpha, ret_tables)
    return (y_prompt, y_sample)
```

```python
import functools

import numpy as np
import jax
import jax.numpy as jnp
from jax import lax
from jax.experimental import pallas as pl
from jax.experimental.pallas import tpu as pltpu

F32 = jnp.float32
BF16 = jnp.bfloat16

D_MODEL = 2048
HEAD_DIM = 128
N_HEADS = 8
SEG = N_HEADS * HEAD_DIM
N_SEG = 7
SEG_QA, SEG_KA, SEG_VA, SEG_QR, SEG_KR, SEG_VR, SEG_GR = range(7)
DILATIONS = (1, 4, 16)
HALF_WIN = 64
ROPE_THETA = 10000.0
RET_CHUNK = 128
MEM_TOKENS = 256
X_HEADS = 4
D_X = X_HEADS * HEAD_DIM
N_GROUPS = 4
EXP_PER_GROUP = 8
N_EXPERTS = N_GROUPS * EXP_PER_GROUP
D_EXPERT = D_MODEL // 4
LN_EPS = 1e-5
NEG_INF = -1e30
ATTN_SCALE = HEAD_DIM ** -0.5
LOG2_E = 1.4426950408889634
LN_2 = 0.6931471805599453

LANES = 128
VMEM_LIMIT = 52 * 1024 * 1024

IN_PROJ_ROWS = 1024
IN_PROJ_VMEM = 58 * 1024 * 1024
N_SLABS = 4
TOKEN_ROWS = D_MODEL // LANES
MOE_PITCH = 24
MOE_COPY_PIECES = 20
ATTN_ROWS = 512
RET_ROWS = 512
ROW_TILE = 256
XATTN_ROWS = 512
MOE_ROWS = 256


def _params(*sem):
    return pltpu.CompilerParams(dimension_semantics=sem, vmem_limit_bytes=VMEM_LIMIT)


def _layer_norm(z, g, b):
    mu = jnp.mean(z, axis=-1, keepdims=True)
    d = z - mu
    var = jnp.mean(d * d, axis=-1, keepdims=True)
    return d * lax.rsqrt(var + LN_EPS) * g + b


def _dot_nt(a, b):
    return lax.dot_general(a, b, (((1,), (1,)), ((), ())), preferred_element_type=F32)


def _dot_tn(a, b):
    return lax.dot_general(a, b, (((0,), (0,)), ((), ())), preferred_element_type=F32)


def _dot(a, b):
    return jnp.dot(a, b, preferred_element_type=F32)


def _in_proj_kernel(x_ref, w_ref, cos_ref, sin_ref, o_ref, o4_ref, o16_ref, xb_ref, slab, slab4,
                    *, tm):
    j = pl.program_id(1)

    @pl.when(j == 0)
    def _():
        xb_ref[...] = x_ref[...].astype(BF16)

    def segment(rot, strided):
        if rot:
            scale = jnp.where(j == SEG_KR, ATTN_SCALE,
                              jnp.where(j == SEG_QA, ATTN_SCALE * LOG2_E, 1.0)).astype(F32)
            c = cos_ref[...] * scale
            s = sin_ref[...] * scale
        for pair in range(N_HEADS // 2):
            acc = _dot(xb_ref[...], w_ref[:, pair * 2 * HEAD_DIM:(pair + 1) * 2 * HEAD_DIM])
            for hh in range(2):
                h = 2 * pair + hh
                cs = slice(h * HEAD_DIM, (h + 1) * HEAD_DIM)
                t = acc[:, hh * HEAD_DIM:(hh + 1) * HEAD_DIM]
                if rot:
                    t = t * c + pltpu.roll(t, HEAD_DIM // 2, 1) * s
                o_ref[:, cs] = t.astype(BF16)
                if strided:
                    k = h % N_SLABS
                    slab[k] = t
                    for r in range(4):
                        v4 = slab[k, pl.ds(r, tm // 4, stride=4), :]
                        o4_ref[r, :, cs] = v4.astype(BF16)
                        slab4[k, r] = v4
                        for a in range(4):
                            o16_ref[r + 4 * a, :, cs] = (
                                slab4[k, r, pl.ds(a, tm // 16, stride=4), :].astype(BF16))

    is_rot = (j == SEG_QA) | (j == SEG_KA) | (j == SEG_QR) | (j == SEG_KR)
    is_attn = j <= SEG_VA
    for rot in (True, False):
        for strided in (True, False):
            cond = (is_rot if rot else jnp.logical_not(is_rot)) & (
                is_attn if strided else jnp.logical_not(is_attn))
            pl.when(cond)(functools.partial(segment, rot, strided))


def _in_proj(x2d, w_in, cos, sin, batch, seq):
    t = x2d.shape[0]
    tm = IN_PROJ_ROWS
    n_s = seq // tm
    n_att = SEG_VA + 1

    def strided_out(dil):
        shape = jax.ShapeDtypeStruct((n_att, batch, dil, seq // dil, SEG), BF16)
        spec = pl.BlockSpec((None, None, dil, tm // dil, SEG),
                            lambda i, j: (jnp.minimum(j, SEG_VA), i // n_s, 0, i % n_s, 0))
        return shape, spec

    (shape4, spec4), (shape16, spec16) = strided_out(4), strided_out(16)
    return pl.pallas_call(
        functools.partial(_in_proj_kernel, tm=tm),
        grid=(t // tm, N_SEG),
        in_specs=[
            pl.BlockSpec((tm, D_MODEL), lambda i, j: (i, 0)),
            pl.BlockSpec((D_MODEL, SEG), lambda i, j: (0, j)),
            pl.BlockSpec((tm, HEAD_DIM), lambda i, j: (i % n_s, 0)),
            pl.BlockSpec((tm, HEAD_DIM), lambda i, j: (i % n_s, 0)),
        ],
        out_specs=[pl.BlockSpec((None, tm, SEG), lambda i, j: (j, i, 0)), spec4, spec16],
        out_shape=[jax.ShapeDtypeStruct((N_SEG, t, SEG), BF16), shape4, shape16],
        scratch_shapes=[pltpu.VMEM((tm, D_MODEL), BF16),
                        pltpu.VMEM((N_SLABS, tm, HEAD_DIM), F32),
                        pltpu.VMEM((N_SLABS, 4, tm // 4, HEAD_DIM), F32)],
        compiler_params=pltpu.CompilerParams(dimension_semantics=("arbitrary", "arbitrary"),
                                             vmem_limit_bytes=IN_PROJ_VMEM),
        name="in_proj",
    )(x2d, w_in, cos, sin)


def _local_attn_kernel(q_ref, kp_ref, kc_ref, kn_ref, vp_ref, vc_ref, vn_ref,
                       o_ref, lse_ref, kbuf, vbuf, *, lt, sub_len):
    i = pl.program_id(2)
    hw = HALF_WIN
    kbuf[0:hw, :] = kp_ref[...]
    kbuf[hw:hw + lt, :] = kc_ref[...]
    kbuf[hw + lt:2 * hw + lt, :] = kn_ref[...]
    vbuf[0:hw, :] = vp_ref[...]
    vbuf[hw:hw + lt, :] = vc_ref[...]
    vbuf[hw + lt:2 * hw + lt, :] = vn_ref[...]

    qb = 128
    kb = qb + 2 * hw
    row = lax.broadcasted_iota(jnp.int32, (qb, kb), 0)
    col = lax.broadcasted_iota(jnp.int32, (qb, kb), 1)
    band = jnp.abs(row + hw - col) <= hw
    lane = lax.broadcasted_iota(jnp.int32, (qb, LANES), 1)

    def body(j, carry):
        r0 = pl.multiple_of(j * qb, qb)
        kpos = i * lt + r0 - hw + col
        bias = jnp.where(band, 0.0, NEG_INF).astype(F32)
        bias = jnp.where(kpos >= 0, bias, NEG_INF)
        bias = jnp.where(kpos < sub_len, bias, NEG_INF)
        lse_tile = jnp.zeros((qb, LANES), F32)
        for h in range(N_HEADS):
            cs = slice(h * HEAD_DIM, (h + 1) * HEAD_DIM)
            q = q_ref[pl.ds(r0, qb), cs]
            k = kbuf[pl.ds(r0, kb), cs]
            v = vbuf[pl.ds(r0, kb), cs]
            s = _dot_nt(q, k) + bias
            m = jnp.max(s, axis=-1, keepdims=True)
            p = jnp.exp2(s - m)
            l = jnp.sum(p, axis=-1, keepdims=True)
            o = _dot(p.astype(BF16), v) / l
            o_ref[pl.ds(r0, qb), cs] = o.astype(BF16)
            lse_tile = jnp.where(lane == h, (m + jnp.log2(l)) * LN_2, lse_tile)
        lse_ref[pl.ds(r0, qb), :] = lse_tile
        return carry

    lax.fori_loop(0, lt // qb, body, 0)


def _local_attn(qkv, batch, seq, dil):
    sub_len = seq // dil
    lt = min(sub_len, ATTN_ROWS)
    hw = HALF_WIN
    n_halo = sub_len // hw
    per = lt // hw

    def main(seg):
        return pl.BlockSpec((None, None, None, lt, SEG), lambda b, r, i: (seg, b, r, i, 0))

    def prev(seg):
        return pl.BlockSpec((None, None, None, hw, SEG),
                            lambda b, r, i: (seg, b, r, jnp.maximum(i * per - 1, 0), 0))

    def nxt(seg):
        return pl.BlockSpec((None, None, None, hw, SEG),
                            lambda b, r, i: (seg, b, r, jnp.minimum((i + 1) * per, n_halo - 1), 0))

    kern = functools.partial(_local_attn_kernel, lt=lt, sub_len=sub_len)
    return pl.pallas_call(
        kern,
        grid=(batch, dil, sub_len // lt),
        in_specs=[main(SEG_QA), prev(SEG_KA), main(SEG_KA), nxt(SEG_KA),
                  prev(SEG_VA), main(SEG_VA), nxt(SEG_VA)],
        out_specs=[pl.BlockSpec((None, None, lt, SEG), lambda b, r, i: (b, r, i, 0)),
                   pl.BlockSpec((None, None, lt, LANES), lambda b, r, i: (b, r, i, 0))],
        out_shape=[jax.ShapeDtypeStruct((batch, dil, sub_len, SEG), BF16),
                   jax.ShapeDtypeStruct((batch, dil, sub_len, LANES), F32)],
        scratch_shapes=[pltpu.VMEM((lt + 2 * hw, SEG), BF16),
                        pltpu.VMEM((lt + 2 * hw, SEG), BF16)],
        compiler_params=_params("arbitrary", "arbitrary", "arbitrary"),
        name="local_attn_d%d" % dil,
    )(qkv, qkv, qkv, qkv, qkv, qkv, qkv)


def _retention_tables():
    h = np.arange(N_HEADS, dtype=np.float64)
    gf = 1.0 - 2.0 ** (-5.0 - h)
    gb = 1.0 - 2.0 ** (-5.5 - h)
    c = RET_CHUNK
    idx = np.arange(c, dtype=np.float64)
    diff = idx[:, None] - idx[None, :]
    dec_f = np.where(diff >= 0, gf[:, None, None] ** np.maximum(diff, 0.0), 0.0)
    dec_b = np.where(diff < 0, gb[:, None, None] ** np.maximum(-diff, 0.0), 0.0)
    decay = dec_f + dec_b
    rows = lambda t: np.broadcast_to(t[:, :, None], (N_HEADS, c, HEAD_DIM))
    xi_f = rows(gf[:, None] ** (idx + 1.0)[None])
    zeta_f = rows(gf[:, None] ** (c - 1.0 - idx)[None])
    xi_b = rows(gb[:, None] ** (c - idx)[None])
    zeta_b = rows(gb[:, None] ** idx[None])
    f = lambda t: jnp.asarray(np.ascontiguousarray(t), F32)
    return (f(decay), f(xi_f), f(zeta_f), f(xi_b), f(zeta_b),
            tuple(float(g ** c) for g in gf), tuple(float(g ** c) for g in gb))


def _ret_fwd_kernel(q_ref, k_ref, v_ref, dec_ref, xi_ref, zeta_ref, o_ref, state, *, rt, cd):
    @pl.when(pl.program_id(1) == 0)
    def _():
        state[...] = jnp.zeros_like(state)

    for c in range(rt // RET_CHUNK):
        rs = slice(c * RET_CHUNK, (c + 1) * RET_CHUNK)
        for h in range(N_HEADS):
            cs = slice(h * HEAD_DIM, (h + 1) * HEAD_DIM)
            q = q_ref[rs, cs]
            k = k_ref[rs, cs]
            v = v_ref[rs, cs]
            a = _dot_nt(q, k) * dec_ref[h]
            o = _dot(a.astype(BF16), v)
            s_old = state[h]
            o = o + _dot(q, s_old.astype(BF16)) * xi_ref[h]
            kz = (k.astype(F32) * zeta_ref[h]).astype(BF16)
            state[h] = cd[h] * s_old + _dot_tn(kz, v)
            o_ref[rs, cs] = o


def _ret_bwd_kernel(q_ref, k_ref, v_ref, g_ref, r_ref, xi_ref, zeta_ref, gn_ref, o_ref, state,
                    *, rt, cd):
    @pl.when(pl.program_id(1) == 0)
    def _():
        state[...] = jnp.zeros_like(state)

    for c in reversed(range(rt // RET_CHUNK)):
        rs = slice(c * RET_CHUNK, (c + 1) * RET_CHUNK)
        for h in range(N_HEADS):
            cs = slice(h * HEAD_DIM, (h + 1) * HEAD_DIM)
            q = q_ref[rs, cs]
            k = k_ref[rs, cs]
            v = v_ref[rs, cs]
            s_old = state[h]
            r = r_ref[rs, cs] + _dot(q, s_old.astype(BF16)) * xi_ref[h]
            kz = (k.astype(F32) * zeta_ref[h]).astype(BF16)
            state[h] = cd[h] * s_old + _dot_tn(kz, v)
            mu = jnp.mean(r, axis=-1, keepdims=True)
            d = r - mu
            var = jnp.mean(d * d, axis=-1, keepdims=True)
            rn = d * lax.rsqrt(var + LN_EPS) * gn_ref[:, cs]
            g = g_ref[rs, cs].astype(F32)
            silu = g / (1.0 + jnp.exp(-g))
            o_ref[rs, cs] = (silu * rn).astype(BF16)


def _retention(h3, gn_g, batch, seq, tables):
    decay, xi_f, zeta_f, xi_b, zeta_b, cd_f, cd_b = tables
    rt = min(seq, RET_ROWS)
    nr = seq // rt
    tab = pl.BlockSpec((N_HEADS, RET_CHUNK, HEAD_DIM), lambda b, t: (0, 0, 0))
    state = pltpu.VMEM((N_HEADS, HEAD_DIM, HEAD_DIM), F32)

    fwd_seg = lambda seg: pl.BlockSpec((None, None, rt, SEG), lambda b, t: (seg, b, t, 0))
    r_fwd = pl.pallas_call(
        functools.partial(_ret_fwd_kernel, rt=rt, cd=cd_f),
        grid=(batch, nr),
        in_specs=[fwd_seg(SEG_QR), fwd_seg(SEG_KR), fwd_seg(SEG_VR), tab, tab, tab],
        out_specs=pl.BlockSpec((None, rt, SEG), lambda b, t: (b, t, 0)),
        out_shape=jax.ShapeDtypeStruct((batch, seq, SEG), F32),
        scratch_shapes=[state],
        compiler_params=_params("arbitrary", "arbitrary"),
        name="retention_fwd",
    )(h3, h3, h3, decay, xi_f, zeta_f)

    bwd_seg = lambda seg: pl.BlockSpec((None, None, rt, SEG),
                                       lambda b, t: (seg, b, nr - 1 - t, 0))
    return pl.pallas_call(
        functools.partial(_ret_bwd_kernel, rt=rt, cd=cd_b),
        grid=(batch, nr),
        in_specs=[bwd_seg(SEG_QR), bwd_seg(SEG_KR), bwd_seg(SEG_VR), bwd_seg(SEG_GR),
                  pl.BlockSpec((None, rt, SEG), lambda b, t: (b, nr - 1 - t, 0)),
                  tab, tab, pl.BlockSpec((1, SEG), lambda b, t: (0, 0))],
        out_specs=pl.BlockSpec((None, rt, SEG), lambda b, t: (b, nr - 1 - t, 0)),
        out_shape=jax.ShapeDtypeStruct((batch, seq, SEG), BF16),
        scratch_shapes=[state],
        compiler_params=_params("arbitrary", "arbitrary"),
        name="retention_bwd",
    )(h3, h3, h3, h3, r_fwd, xi_b, zeta_b, gn_g)


def _out_proj_kernel(o1_ref, o2_ref, o3_ref, l1_ref, l2_ref, l3_ref, ret_ref, x_ref, w_ref,
                     g_ref, b_ref, out_ref, attn_buf, nat_o, nat_l, *, alpha):
    rows = x_ref.shape[0]
    heads = [slice(h * HEAD_DIM, (h + 1) * HEAD_DIM) for h in range(N_HEADS)]
    for k, (dil, o_ref, l_ref) in enumerate(((4, o2_ref, l2_ref), (16, o3_ref, l3_ref))):
        for r in range(dil):
            dst = pl.ds(r, rows // dil, stride=dil)
            nat_l[k, dst, :] = l_ref[r]
            for h, cs in enumerate(heads):
                nat_o[k, h, dst, :] = o_ref[r, :, cs].astype(F32)
    la, lb, lc = l1_ref[...], nat_l[0], nat_l[1]
    m = jnp.maximum(jnp.maximum(la, lb), lc)
    ea, eb, ec = jnp.exp(la - m), jnp.exp(lb - m), jnp.exp(lc - m)
    inv = 1.0 / (ea + eb + ec)
    wa, wb, wc = ea * inv, eb * inv, ec * inv
    for h, cs in enumerate(heads):
        bc = lambda w: jnp.broadcast_to(w[:, h:h + 1], (rows, HEAD_DIM))
        mix = (bc(wa) * o1_ref[:, cs].astype(F32) + bc(wb) * nat_o[0, h] + bc(wc) * nat_o[1, h])
        attn_buf[:, cs] = mix.astype(BF16)
    y = _dot(attn_buf[...], w_ref[0:SEG, :]) + _dot(ret_ref[...], w_ref[SEG:2 * SEG, :])
    out_ref[...] = _layer_norm(alpha * x_ref[...] + y, g_ref[...], b_ref[...])


def _out_proj(outs, lses, ret_o, x2d, w_out, ln_g, ln_b, seq, alpha):
    t = x2d.shape[0]
    tm = ROW_TILE
    per_b = seq // tm
    row = lambda w: pl.BlockSpec((tm, w), lambda i: (i, 0))
    full = lambda a: pl.BlockSpec(a.shape, lambda i: (0,) * a.ndim)

    def branch(dil, width):
        return pl.BlockSpec((None, dil, tm // dil, width),
                            lambda i: (i // per_b, 0, i % per_b, 0))

    in_specs = ([row(SEG), branch(4, SEG), branch(16, SEG),
                 row(LANES), branch(4, LANES), branch(16, LANES)]
                + [row(SEG), row(D_MODEL), full(w_out), full(ln_g), full(ln_b)])
    return pl.pallas_call(
        functools.partial(_out_proj_kernel, alpha=alpha),
        grid=(t // tm,),
        in_specs=in_specs,
        out_specs=row(D_MODEL),
        out_shape=jax.ShapeDtypeStruct((t, D_MODEL), F32),
        scratch_shapes=[pltpu.VMEM((tm, SEG), BF16),
                        pltpu.VMEM((2, N_HEADS, tm, HEAD_DIM), F32),
                        pltpu.VMEM((2, tm, LANES), F32)],
        compiler_params=_params("arbitrary"),
        name="out_proj_ln1",
    )(outs[0].reshape(t, SEG), outs[1], outs[2], lses[0].reshape(t, LANES), lses[1], lses[2],
      ret_o, x2d, w_out, ln_g, ln_b)


def _mem_kv_kernel(m_ref, w_ref, o_ref):
    o_ref[...] = _dot(m_ref[...].astype(BF16), w_ref[...]).astype(BF16)


def _mem_kv(mem2d, w_mkv):
    rows = mem2d.shape[0]
    tm = ROW_TILE
    return pl.pallas_call(
        _mem_kv_kernel,
        grid=(rows // tm,),
        in_specs=[pl.BlockSpec((tm, D_MODEL), lambda i: (i, 0)),
                  pl.BlockSpec(w_mkv.shape, lambda i: (0, 0))],
        out_specs=pl.BlockSpec((tm, 2 * D_X), lambda i: (i, 0)),
        out_shape=jax.ShapeDtypeStruct((rows, 2 * D_X), BF16),
        compiler_params=_params("arbitrary"),
        name="mem_kv",
    )(mem2d, w_mkv)


def _route(logits):
    shape = logits.shape
    lane_i = lax.broadcasted_iota(jnp.int32, shape, 1)
    lane = lane_i.astype(F32)
    lowest = jnp.float32(-3.0e38)
    none = jnp.float32(LANES)
    rmax = lambda t: jnp.max(t, axis=-1, keepdims=True)
    rmin = lambda t: jnp.min(t, axis=-1, keepdims=True)

    is_group = lane_i < N_GROUPS
    gl = jnp.where(is_group, logits, lowest)
    g_max = rmax(gl)
    g_idx = rmin(jnp.where(gl == g_max, lane, none))
    p_g = 1.0 / jnp.sum(jnp.where(is_group, jnp.exp(logits - g_max), 0.0), axis=-1, keepdims=True)

    e_lo = N_GROUPS + EXP_PER_GROUP * g_idx
    el = jnp.where(lane >= e_lo, logits, lowest)
    el = jnp.where(lane < e_lo + EXP_PER_GROUP, el, lowest)
    v1 = rmax(el)
    i1 = rmin(jnp.where(el == v1, lane, none))
    el2 = jnp.where(lane == i1, lowest, el)
    v2 = rmax(el2)
    i2 = rmin(jnp.where(el2 == v2, lane, none))
    e2 = jnp.exp(v2 - v1)
    g1 = p_g / (1.0 + e2)
    g2 = p_g * e2 / (1.0 + e2)
    out = jnp.where(lane_i == 0, i1 - N_GROUPS,
                    jnp.where(lane_i == 1, i2 - N_GROUPS,
                              jnp.where(lane_i == 2, g1, jnp.where(lane_i == 3, g2, 0.0))))
    return out.astype(F32)


def _xattn_kernel(x_ref, kv_ref, wq_ref, wo_ref, g_ref, b_ref, wr_ref, br_ref,
                  x2_rows_ref, route_ref, obuf, *, alpha):
    sub = ROW_TILE
    for part in range(x_ref.shape[0] // sub):
        rs = slice(part * sub, (part + 1) * sub)
        x = x_ref[rs, :]
        q = _dot(x.astype(BF16), wq_ref[...]).astype(BF16)
        for h in range(X_HEADS):
            cs = slice(h * HEAD_DIM, (h + 1) * HEAD_DIM)
            k = kv_ref[:, cs]
            v = kv_ref[:, D_X + h * HEAD_DIM:D_X + (h + 1) * HEAD_DIM]
            s = _dot_nt(q[:, cs], k) * ATTN_SCALE
            m = jnp.max(s, axis=-1, keepdims=True)
            p = jnp.exp(s - m)
            l = jnp.sum(p, axis=-1, keepdims=True)
            obuf[rs, cs] = (_dot(p.astype(BF16), v) / l).astype(BF16)
        y = _dot(obuf[rs, :], wo_ref[...])
        x2 = _layer_norm(alpha * x + y, g_ref[...], b_ref[...])
        for s in range(TOKEN_ROWS):
            x2_rows_ref[pl.ds(part * sub * TOKEN_ROWS + s, sub, stride=TOKEN_ROWS), :] = (
                x2[:, s * LANES:(s + 1) * LANES])
        xh = x2.astype(BF16)
        xl = (x2 - xh.astype(F32)).astype(BF16)
        hw = _dot(xh, wr_ref[...])
        logits = hw[:, :LANES] + hw[:, LANES:] + _dot(xl, wr_ref[:, :LANES]) + br_ref[...]
        route_ref[rs, :] = _route(logits)


def _xattn(x1, kv, w_mq, w_mo, ln_g, ln_b, w_r, b_r, seq, alpha):
    t = x1.shape[0]
    tm = XATTN_ROWS
    per_b = seq // tm
    row = lambda w: pl.BlockSpec((tm, w), lambda i: (i, 0))
    full = lambda a: pl.BlockSpec(a.shape, lambda i: (0,) * a.ndim)
    return pl.pallas_call(
        functools.partial(_xattn_kernel, alpha=alpha),
        grid=(t // tm,),
        in_specs=[row(D_MODEL),
                  pl.BlockSpec((MEM_TOKENS, 2 * D_X), lambda i: (i // per_b, 0)),
                  full(w_mq), full(w_mo), full(ln_g), full(ln_b),
                  full(w_r), full(b_r)],
        out_specs=[pl.BlockSpec((tm * TOKEN_ROWS, LANES), lambda i: (i, 0)), row(LANES)],
        out_shape=[jax.ShapeDtypeStruct((t * TOKEN_ROWS, LANES), F32),
                   jax.ShapeDtypeStruct((t, LANES), F32)],
        scratch_shapes=[pltpu.VMEM((tm, D_X), BF16)],
        compiler_params=_params("arbitrary"),
        name="xattn_ln2_router",
    )(x1, kv, w_mq, w_mo, ln_g, ln_b, w_r, b_r)


def _moe_kernel(blk_e_ref, blk_nv_ref, src_hbm, dst_hbm, x_hbm, w1_ref, w3_ref, w2_ref, y_hbm,
                src_smem, dst_smem, xbuf0, xbuf1, ybuf0, ybuf1, xb_ref, hid_ref, sem_src, sem_dst, sem_g,
                sem_s, *, nblk, bm, n_rows):
    del blk_e_ref
    b = pl.program_id(0)
    cur = b % 2
    xbuf = (xbuf0, xbuf1)
    ybuf = (ybuf0, ybuf1)

    def rows_of(blk):
        inside = (blk >= 0) & (blk < nblk)
        return jnp.where(inside, blk_nv_ref[jnp.clip(blk, 0, nblk - 1)], 0)

    nv_prev2, nv_prev, nv_cur, nv_next = rows_of(b - 2), rows_of(b - 1), rows_of(b), rows_of(b + 1)

    def src_copy(blk):
        return pltpu.make_async_copy(src_hbm.at[blk], src_smem.at[blk % 2], sem_src.at[blk % 2])

    def dst_copy(blk):
        return pltpu.make_async_copy(dst_hbm.at[blk], dst_smem.at[blk % 2], sem_dst.at[blk % 2])

    tr, pitch = TOKEN_ROWS, MOE_PITCH

    def slab(buf, r):
        return buf.at[pl.ds(r * pitch, tr)]

    def all_slabs_bytes(buf):
        return buf.at[pl.ds(0, bm * tr)]

    def gather_row(slot, r):
        off = pl.multiple_of(src_smem[slot, r], tr)
        pltpu.make_async_copy(x_hbm.at[pl.ds(off, tr)], slab(xbuf[slot], r),
                              sem_g.at[slot]).start(priority=r % 2)

    def scatter_row(slot, r):
        off = pl.multiple_of(dst_smem[slot, r], tr)
        pltpu.make_async_copy(slab(ybuf[slot], r), y_hbm.at[pl.ds(off, tr)],
                              sem_s.at[slot]).start(priority=r % 2)

    def issue_gathers(slot):
        for r in range(bm):
            gather_row(slot, r)

    def wait_gathers(slot):
        pltpu.make_async_copy(x_hbm.at[pl.ds(0, bm * tr)], all_slabs_bytes(xbuf[slot]),
                              sem_g.at[slot]).wait()

    def issue_scatters(slot):
        for r in range(bm):
            scatter_row(slot, r)

    def wait_scatters(slot):
        pltpu.make_async_copy(all_slabs_bytes(ybuf[slot]), y_hbm.at[pl.ds(0, bm * tr)],
                              sem_s.at[slot]).wait()

    def compute(slot, copies=()):
        copies = list(copies)
        per_piece = -(-len(copies) // MOE_COPY_PIECES)

        def drip():
            for issue in copies[:per_piece]:
                issue()
            del copies[:per_piece]

        nc = 2 * LANES
        for s in range(tr):
            xb_ref[:, s * LANES:(s + 1) * LANES] = (
                xbuf[slot][pl.ds(s, bm, stride=pitch), :].astype(BF16))
            drip()
        for c in range(D_EXPERT // nc):
            h1 = _dot(xb_ref[...], w1_ref[:, c * nc:(c + 1) * nc])
            drip()
            h3 = _dot(xb_ref[...], w3_ref[:, c * nc:(c + 1) * nc])
            hid_ref[:, c * nc:(c + 1) * nc] = (h1 / (1.0 + jnp.exp(-h1)) * h3).astype(BF16)
            drip()
        for c in range(D_MODEL // nc):
            y = _dot(hid_ref[...], w2_ref[:, c * nc:(c + 1) * nc])
            for k in range(nc // LANES):
                s = c * (nc // LANES) + k
                ybuf[slot][pl.ds(s, bm, stride=pitch), :] = y[:, k * LANES:(k + 1) * LANES]
            drip()
        assert not copies

    @pl.when(b == 0)
    def _():
        ybuf0[...] = jnp.zeros_like(ybuf0)
        spare = pltpu.make_async_copy(all_slabs_bytes(ybuf0),
                                      y_hbm.at[pl.ds(n_rows * tr, bm * tr)], sem_s.at[0])
        spare.start()
        spare.wait()
        src_copy(0).start()
        src_copy(0).wait()

        @pl.when(nv_cur > 0)
        def _():
            issue_gathers(0)
        if nblk > 1:
            src_copy(1).start()

    @pl.when(b >= 1)
    def _():
        dst_copy(b - 1).wait()

    @pl.when(b + 1 < nblk)
    def _():
        src_copy(b + 1).wait()
        dst_copy(b).start()

    @pl.when(b + 2 < nblk)
    def _():
        src_copy(b + 2).start()

    steady = (nv_prev > 0) & (nv_cur > 0) & (nv_next > 0)
    for p in (0, 1):
        q = 1 - p
        mine = cur == p

        @pl.when(mine & (nv_prev2 > 0) & (nv_prev > 0))
        def _():
            wait_scatters(p)

        @pl.when(mine & steady)
        def _():
            wait_gathers(p)
            copies = []
            for r in range(bm):
                copies.append(functools.partial(gather_row, q, r))
                copies.append(functools.partial(scatter_row, q, r))
            compute(p, copies)

        @pl.when(mine & (nv_cur > 0) & jnp.logical_not(steady))
        def _():
            wait_gathers(p)

            @pl.when(nv_next > 0)
            def _():
                issue_gathers(q)

            @pl.when(nv_prev > 0)
            def _():
                issue_scatters(q)
            compute(p)

        @pl.when(mine & (nv_cur == 0) & (nv_prev > 0))
        def _():
            issue_scatters(q)
            wait_scatters(q)


def _moe_dispatch(route, bm):
    t = route.shape[0]
    m = 2 * t
    nblk = m // bm + N_EXPERTS
    eid = route[:, 0:2].astype(jnp.int32).T.reshape(m)
    counts = jnp.sum((eid[:, None] == jnp.arange(N_EXPERTS, dtype=jnp.int32)[None, :]).astype(jnp.int32),
                     axis=0)
    order = jnp.argsort(eid).astype(jnp.int32)
    start = jnp.cumsum(counts) - counts
    nb_e = (counts + bm - 1) // bm
    bend = jnp.cumsum(nb_e)
    bstart = bend - nb_e
    blk = jnp.arange(nblk, dtype=jnp.int32)
    blk_e = jnp.minimum(jnp.sum((blk[:, None] >= bend[None, :]).astype(jnp.int32), axis=1),
                        N_EXPERTS - 1).astype(jnp.int32)
    within = blk - bstart[blk_e]
    blk_nv = jnp.where(blk < bend[-1], jnp.clip(counts[blk_e] - within * bm, 0, bm), 0).astype(jnp.int32)
    row = jnp.arange(bm, dtype=jnp.int32)[None, :]
    sorted_pos = (start[blk_e] + within * bm)[:, None] + row
    asg = order[jnp.clip(sorted_pos, 0, m - 1)]
    valid = row < blk_nv[:, None]
    row_src = jnp.where(valid, jnp.where(asg >= t, asg - t, asg), 0).astype(jnp.int32)
    row_dst = jnp.where(valid, asg, m + row).astype(jnp.int32)
    return blk_e, blk_nv, row_src * TOKEN_ROWS, row_dst * TOKEN_ROWS, nblk


def _moe(x2_rows, route, w1, w3, w2):
    t = route.shape[0]
    bm = MOE_ROWS
    blk_e, blk_nv, row_src, row_dst, nblk = _moe_dispatch(route, bm)
    grid_spec = pltpu.PrefetchScalarGridSpec(
        num_scalar_prefetch=2,
        grid=(nblk,),
        in_specs=[pl.BlockSpec(memory_space=pl.ANY),
                  pl.BlockSpec(memory_space=pl.ANY),
                  pl.BlockSpec(memory_space=pl.ANY),
                  pl.BlockSpec((None, D_MODEL, D_EXPERT), lambda b, be, nv: (be[b], 0, 0)),
                  pl.BlockSpec((None, D_MODEL, D_EXPERT), lambda b, be, nv: (be[b], 0, 0)),
                  pl.BlockSpec((None, D_EXPERT, D_MODEL), lambda b, be, nv: (be[b], 0, 0))],
        out_specs=pl.BlockSpec(memory_space=pl.ANY),
        scratch_shapes=[pltpu.SMEM((2, bm), jnp.int32),
                        pltpu.SMEM((2, bm), jnp.int32),
                        pltpu.VMEM((bm * MOE_PITCH, LANES), F32),
                        pltpu.VMEM((bm * MOE_PITCH, LANES), F32),
                        pltpu.VMEM((bm * MOE_PITCH, LANES), F32),
                        pltpu.VMEM((bm * MOE_PITCH, LANES), F32),
                        pltpu.VMEM((bm, D_MODEL), BF16),
                        pltpu.VMEM((bm, D_EXPERT), BF16),
                        pltpu.SemaphoreType.DMA((2,)),
                        pltpu.SemaphoreType.DMA((2,)),
                        pltpu.SemaphoreType.DMA((2,)),
                        pltpu.SemaphoreType.DMA((2,))],
    )
    return pl.pallas_call(
        functools.partial(_moe_kernel, nblk=nblk, bm=bm, n_rows=2 * t),
        grid_spec=grid_spec,
        out_shape=jax.ShapeDtypeStruct(((2 * t + bm) * TOKEN_ROWS, LANES), F32),
        compiler_params=pltpu.CompilerParams(dimension_semantics=("arbitrary",),
                                             vmem_limit_bytes=VMEM_LIMIT,
                                             disable_bounds_checks=True),
        name="moe_experts",
    )(blk_e, blk_nv, row_src, row_dst, x2_rows, w1, w3, w2)


def _final_kernel(x_ref, y0_ref, y1_ref, route_ref, g_ref, b_ref, o_ref, z_ref, *, alpha):
    r = route_ref[...]
    rows = r.shape[0]
    g0 = jnp.broadcast_to(r[:, 2:3], (rows, LANES))
    g1 = jnp.broadcast_to(r[:, 3:4], (rows, LANES))
    for s in range(TOKEN_ROWS):
        cs = slice(s * LANES, (s + 1) * LANES)
        z_ref[:, cs] = (alpha * x_ref[pl.ds(s, rows, stride=TOKEN_ROWS), :]
                        + g0 * y0_ref[pl.ds(s, rows, stride=TOKEN_ROWS), :]
                        + g1 * y1_ref[pl.ds(s, rows, stride=TOKEN_ROWS), :])
    o_ref[...] = _layer_norm(z_ref[...], g_ref[...], b_ref[...])


def _final(x2_rows, y_rows, route, ln_g, ln_b, alpha):
    t = route.shape[0]
    tm = ROW_TILE
    nt = t // tm
    row = lambda w: pl.BlockSpec((tm, w), lambda i: (i, 0))
    full = lambda a: pl.BlockSpec(a.shape, lambda i: (0,) * a.ndim)
    return pl.pallas_call(
        functools.partial(_final_kernel, alpha=alpha),
        grid=(nt,),
        in_specs=[pl.BlockSpec((tm * TOKEN_ROWS, LANES), lambda i: (i, 0)),
                  pl.BlockSpec((tm * TOKEN_ROWS, LANES), lambda i: (i, 0)),
                  pl.BlockSpec((tm * TOKEN_ROWS, LANES), lambda i: (i + nt, 0)),
                  row(LANES), full(ln_g), full(ln_b)],
        out_specs=row(D_MODEL),
        out_shape=jax.ShapeDtypeStruct((t, D_MODEL), F32),
        scratch_shapes=[pltpu.VMEM((tm, D_MODEL), F32)],
        compiler_params=_params("arbitrary"),
        name="moe_combine_ln3",
    )(x2_rows, y_rows, y_rows, route, ln_g, ln_b)


def _rotary_tables(seq):
    half = HEAD_DIM // 2
    inv_freq = ROPE_THETA ** (-jnp.arange(half, dtype=F32) / half)
    ang = jnp.arange(seq, dtype=F32)[:, None] * inv_freq[None, :]
    cos, sin = jnp.cos(ang), jnp.sin(ang)
    return jnp.concatenate([cos, cos], axis=-1), jnp.concatenate([-sin, sin], axis=-1)


def _prepare_weights(l, w_in, ret_gn_g, w_out, ln1_g, ln1_b, w_mq, w_mkv, w_mo, ln2_g, ln2_b,
                     w_gr, b_gr, w_er, b_er, w1, w3, w2, ln3_g, ln3_b):
    row = lambda v: v[l].reshape(1, -1).astype(F32)
    pad = LANES - N_GROUPS - N_EXPERTS
    w_r = jnp.concatenate([w_gr[l], w_er[l], jnp.zeros((D_MODEL, pad), F32)], axis=1)
    wr_hi = w_r.astype(BF16)
    wr_lo = (w_r - wr_hi.astype(F32)).astype(BF16)
    b_r = jnp.concatenate([b_gr[l], b_er[l], jnp.zeros((pad,), F32)]).reshape(1, LANES)
    return dict(
        w_in=w_in[l].astype(BF16), gn_g=row(ret_gn_g), w_out=w_out[l].astype(BF16),
        ln1=(row(ln1_g), row(ln1_b)), w_mq=w_mq[l].astype(BF16), w_mkv=w_mkv[l].astype(BF16),
        w_mo=w_mo[l].astype(BF16), ln2=(row(ln2_g), row(ln2_b)),
        w_r=jnp.concatenate([wr_hi, wr_lo], axis=1), b_r=b_r,
        w1=w1[l].astype(BF16), w3=w3[l].astype(BF16), w2=w2[l].astype(BF16),
        ln3=(row(ln3_g), row(ln3_b)))


def _encoder_layer(x, mem, w, alpha, ret_tables):
    batch, seq, _ = x.shape
    t = batch * seq
    x2d = x.reshape(t, D_MODEL)
    cos, sin = _rotary_tables(seq)
    h, ha4, ha16 = _in_proj(x2d, w['w_in'], cos, sin, batch, seq)
    outs, lses = [], []
    for dil, qkv in zip(DILATIONS, (h.reshape(N_SEG, batch, 1, seq, SEG), ha4, ha16)):
        o, lse = _local_attn(qkv, batch, seq, dil)
        outs.append(o)
        lses.append(lse)
    ret_o = _retention(h.reshape(N_SEG, batch, seq, SEG), w['gn_g'], batch, seq, ret_tables)
    x1 = _out_proj(outs, lses, ret_o.reshape(t, SEG), x2d, w['w_out'], *w['ln1'], seq, alpha)
    kv = _mem_kv(mem.reshape(batch * MEM_TOKENS, D_MODEL), w['w_mkv'])
    x2_rows, route = _xattn(x1, kv, w['w_mq'], w['w_mo'], *w['ln2'], w['w_r'], w['b_r'], seq, alpha)
    y_rows = _moe(x2_rows, route, w['w1'], w['w3'], w['w2'])
    out = _final(x2_rows, y_rows, route, *w['ln3'], alpha)
    return out.reshape(batch, seq, D_MODEL)


def kernel(x_prompt, x_sample, mem_prompt, mem_sample, w_in, ret_gn_g, w_out, ln1_g, ln1_b,
           w_mq, w_mkv, w_mo, ln2_g, ln2_b, w_gr, b_gr, w_er, b_er, w1, w3, w2, ln3_g, ln3_b):
    depth = w_in.shape[0]
    alpha = (2 * depth) ** 0.25
    ret_tables = _retention_tables()
    y_prompt, y_sample = x_prompt, x_sample
    for l in range(depth):
        w = _prepare_weights(l, w_in, ret_gn_g, w_out, ln1_g, ln1_b, w_mq, w_mkv, w_mo, ln2_g,
                             ln2_b, w_gr, b_gr, w_er, b_er, w1, w3, w2, ln3_g, ln3_b)
        y_prompt = _encoder_layer(y_prompt, mem_prompt, w, alpha, ret_tables)
        y_sample = _encoder_layer(y_sample, mem_sample, w, alpha, ret_tables)
    return (y_prompt, y_sample)
```

```python
import functools

import numpy as np
import jax
import jax.numpy as jnp
from jax import lax
from jax.experimental import pallas as pl
from jax.experimental.pallas import tpu as pltpu

F32 = jnp.float32
BF16 = jnp.bfloat16

D_MODEL = 2048
HEAD_DIM = 128
N_HEADS = 8
SEG = N_HEADS * HEAD_DIM
N_SEG = 7
SEG_QA, SEG_KA, SEG_VA, SEG_QR, SEG_KR, SEG_VR, SEG_GR = range(7)
DILATIONS = (1, 4, 16)
HALF_WIN = 64
ROPE_THETA = 10000.0
RET_CHUNK = 128
MEM_TOKENS = 256
X_HEADS = 4
D_X = X_HEADS * HEAD_DIM
N_GROUPS = 4
EXP_PER_GROUP = 8
N_EXPERTS = N_GROUPS * EXP_PER_GROUP
D_EXPERT = D_MODEL // 4
LN_EPS = 1e-5
NEG_INF = -1e30
ATTN_SCALE = HEAD_DIM ** -0.5
LOG2_E = 1.4426950408889634
LN_2 = 0.6931471805599453

LANES = 128
VMEM_LIMIT = 52 * 1024 * 1024

IN_PROJ_ROWS = 1024
IN_PROJ_VMEM = 58 * 1024 * 1024
N_SLABS = 4
TOKEN_ROWS = D_MODEL // LANES
MOE_PITCH = 24
MOE_COPY_PIECES = 20
ATTN_ROWS = 512
RET_ROWS = 512
ROW_TILE = 256
XATTN_ROWS = 512
MOE_ROWS = 256


def _params(*sem):
    return pltpu.CompilerParams(dimension_semantics=sem, vmem_limit_bytes=VMEM_LIMIT)


def _layer_norm(z, g, b):
    mu = jnp.mean(z, axis=-1, keepdims=True)
    d = z - mu
    var = jnp.mean(d * d, axis=-1, keepdims=True)
    return d * lax.rsqrt(var + LN_EPS) * g + b


def _dot_nt(a, b):
    return lax.dot_general(a, b, (((1,), (1,)), ((), ())), preferred_element_type=F32)


def _dot_tn(a, b):
    return lax.dot_general(a, b, (((0,), (0,)), ((), ())), preferred_element_type=F32)


def _dot(a, b):
    return jnp.dot(a, b, preferred_element_type=F32)


def _in_proj_kernel(x_ref, w_ref, cos_ref, sin_ref, o_ref, o4_ref, o16_ref, xb_ref, slab, slab4,
                    *, tm):
    j = pl.program_id(1)

    @pl.when(j == 0)
    def _():
        xb_ref[...] = x_ref[...].astype(BF16)

    def segment(rot, strided):
        if rot:
            scale = jnp.where(j == SEG_KR, ATTN_SCALE,
                              jnp.where(j == SEG_QA, ATTN_SCALE * LOG2_E, 1.0)).astype(F32)
            c = cos_ref[...] * scale
            s = sin_ref[...] * scale
        for pair in range(N_HEADS // 2):
            acc = _dot(xb_ref[...], w_ref[:, pair * 2 * HEAD_DIM:(pair + 1) * 2 * HEAD_DIM])
            for hh in range(2):
                h = 2 * pair + hh
                cs = slice(h * HEAD_DIM, (h + 1) * HEAD_DIM)
                t = acc[:, hh * HEAD_DIM:(hh + 1) * HEAD_DIM]
                if rot:
                    t = t * c + pltpu.roll(t, HEAD_DIM // 2, 1) * s
                o_ref[:, cs] = t.astype(BF16)
                if strided:
                    k = h % N_SLABS
                    slab[k] = t
                    for r in range(4):
                        v4 = slab[k, pl.ds(r, tm // 4, stride=4), :]
                        o4_ref[r, :, cs] = v4.astype(BF16)
                        slab4[k, r] = v4
                        for a in range(4):
                            o16_ref[r + 4 * a, :, cs] = (
                                slab4[k, r, pl.ds(a, tm // 16, stride=4), :].astype(BF16))

    is_rot = (j == SEG_QA) | (j == SEG_KA) | (j == SEG_QR) | (j == SEG_KR)
    is_attn = j <= SEG_VA
    for rot in (True, False):
        for strided in (True, False):
            cond = (is_rot if rot else jnp.logical_not(is_rot)) & (
                is_attn if strided else jnp.logical_not(is_attn))
            pl.when(cond)(functools.partial(segment, rot, strided))


def _in_proj(x2d, w_in, cos, sin, batch, seq):
    t = x2d.shape[0]
    tm = IN_PROJ_ROWS
    n_s = seq // tm
    n_att = SEG_VA + 1

    def strided_out(dil):
        shape = jax.ShapeDtypeStruct((n_att, batch, dil, seq // dil, SEG), BF16)
        spec = pl.BlockSpec((None, None, dil, tm // dil, SEG),
                            lambda i, j: (jnp.minimum(j, SEG_VA), i // n_s, 0, i % n_s, 0))
        return shape, spec

    (shape4, spec4), (shape16, spec16) = strided_out(4), strided_out(16)
    return pl.pallas_call(
        functools.partial(_in_proj_kernel, tm=tm),
        grid=(t // tm, N_SEG),
        in_specs=[
            pl.BlockSpec((tm, D_MODEL), lambda i, j: (i, 0)),
            pl.BlockSpec((D_MODEL, SEG), lambda i, j: (0, j)),
            pl.BlockSpec((tm, HEAD_DIM), lambda i, j: (i % n_s, 0)),
            pl.BlockSpec((tm, HEAD_DIM), lambda i, j: (i % n_s, 0)),
        ],
        out_specs=[pl.BlockSpec((None, tm, SEG), lambda i, j: (j, i, 0)), spec4, spec16],
        out_shape=[jax.ShapeDtypeStruct((N_SEG, t, SEG), BF16), shape4, shape16],
        scratch_shapes=[pltpu.VMEM((tm, D_MODEL), BF16),
                        pltpu.VMEM((N_SLABS, tm, HEAD_DIM), F32),
                        pltpu.VMEM((N_SLABS, 4, tm // 4, HEAD_DIM), F32)],
        compiler_params=pltpu.CompilerParams(dimension_semantics=("arbitrary", "arbitrary"),
                                             vmem_limit_bytes=IN_PROJ_VMEM),
        name="in_proj",
    )(x2d, w_in, cos, sin)


def _local_attn_kernel(q_ref, kp_ref, kc_ref, kn_ref, vp_ref, vc_ref, vn_ref,
                       o_ref, lse_ref, kbuf, vbuf, *, lt, sub_len):
    i = pl.program_id(2)
    hw = HALF_WIN
    kbuf[0:hw, :] = kp_ref[...]
    kbuf[hw:hw + lt, :] = kc_ref[...]
    kbuf[hw + lt:2 * hw + lt, :] = kn_ref[...]
    vbuf[0:hw, :] = vp_ref[...]
    vbuf[hw:hw + lt, :] = vc_ref[...]
    vbuf[hw + lt:2 * hw + lt, :] = vn_ref[...]

    qb = 128
    kb = qb + 2 * hw
    row = lax.broadcasted_iota(jnp.int32, (qb, kb), 0)
    col = lax.broadcasted_iota(jnp.int32, (qb, kb), 1)
    band = jnp.abs(row + hw - col) <= hw
    lane = lax.broadcasted_iota(jnp.int32, (qb, LANES), 1)

    def body(j, carry):
        r0 = pl.multiple_of(j * qb, qb)
        kpos = i * lt + r0 - hw + col
        bias = jnp.where(band, 0.0, NEG_INF).astype(F32)
        bias = jnp.where(kpos >= 0, bias, NEG_INF)
        bias = jnp.where(kpos < sub_len, bias, NEG_INF)
        lse_tile = jnp.zeros((qb, LANES), F32)
        for h in range(N_HEADS):
            cs = slice(h * HEAD_DIM, (h + 1) * HEAD_DIM)
            q = q_ref[pl.ds(r0, qb), cs]
            k = kbuf[pl.ds(r0, kb), cs]
            v = vbuf[pl.ds(r0, kb), cs]
            s = _dot_nt(q, k) + bias
            m = jnp.max(s, axis=-1, keepdims=True)
            p = jnp.exp2(s - m)
            l = jnp.sum(p, axis=-1, keepdims=True)
            o = _dot(p.astype(BF16), v) / l
            o_ref[pl.ds(r0, qb), cs] = o.astype(BF16)
            lse_tile = jnp.where(lane == h, (m + jnp.log2(l)) * LN_2, lse_tile)
        lse_ref[pl.ds(r0, qb), :] = lse_tile
        return carry

    lax.fori_loop(0, lt // qb, body, 0)


def _local_attn(qkv, batch, seq, dil):
    sub_len = seq // dil
    lt = min(sub_len, ATTN_ROWS)
    hw = HALF_WIN
    n_halo = sub_len // hw
    per = lt // hw

    def main(seg):
        return pl.BlockSpec((None, None, None, lt, SEG), lambda b, r, i: (seg, b, r, i, 0))

    def prev(seg):
        return pl.BlockSpec((None, None, None, hw, SEG),
                            lambda b, r, i: (seg, b, r, jnp.maximum(i * per - 1, 0), 0))

    def nxt(seg):
        return pl.BlockSpec((None, None, None, hw, SEG),
                            lambda b, r, i: (seg, b, r, jnp.minimum((i + 1) * per, n_halo - 1), 0))

    kern = functools.partial(_local_attn_kernel, lt=lt, sub_len=sub_len)
    return pl.pallas_call(
        kern,
        grid=(batch, dil, sub_len // lt),
        in_specs=[main(SEG_QA), prev(SEG_KA), main(SEG_KA), nxt(SEG_KA),
                  prev(SEG_VA), main(SEG_VA), nxt(SEG_VA)],
        out_specs=[pl.BlockSpec((None, None, lt, SEG), lambda b, r, i: (b, r, i, 0)),
                   pl.BlockSpec((None, None, lt, LANES), lambda b, r, i: (b, r, i, 0))],
        out_shape=[jax.ShapeDtypeStruct((batch, dil, sub_len, SEG), BF16),
                   jax.ShapeDtypeStruct((batch, dil, sub_len, LANES), F32)],
        scratch_shapes=[pltpu.VMEM((lt + 2 * hw, SEG), BF16),
                        pltpu.VMEM((lt + 2 * hw, SEG), BF16)],
        compiler_params=_params("arbitrary", "arbitrary", "arbitrary"),
        name="local_attn_d%d" % dil,
    )(qkv, qkv, qkv, qkv, qkv, qkv, qkv)


def _retention_tables():
    h = np.arange(N_HEADS, dtype=np.float64)
    gf = 1.0 - 2.0 ** (-5.0 - h)
    gb = 1.0 - 2.0 ** (-5.5 - h)
    c = RET_CHUNK
    idx = np.arange(c, dtype=np.float64)
    diff = idx[:, None] - idx[None, :]
    dec_f = np.where(diff >= 0, gf[:, None, None] ** np.maximum(diff, 0.0), 0.0)
    dec_b = np.where(diff < 0, gb[:, None, None] ** np.maximum(-diff, 0.0), 0.0)
    decay = dec_f + dec_b
    rows = lambda t: np.broadcast_to(t[:, :, None], (N_HEADS, c, HEAD_DIM))
    xi_f = rows(gf[:, None] ** (idx + 1.0)[None])
    zeta_f = rows(gf[:, None] ** (c - 1.0 - idx)[None])
    xi_b = rows(gb[:, None] ** (c - idx)[None])
    zeta_b = rows(gb[:, None] ** idx[None])
    f = lambda t: jnp.asarray(np.ascontiguousarray(t), F32)
    return (f(decay), f(xi_f), f(zeta_f), f(xi_b), f(zeta_b),
            tuple(float(g ** c) for g in gf), tuple(float(g ** c) for g in gb))


def _ret_fwd_kernel(q_ref, k_ref, v_ref, dec_ref, xi_ref, zeta_ref, o_ref, state, *, rt, cd):
    @pl.when(pl.program_id(1) == 0)
    def _():
        state[...] = jnp.zeros_like(state)

    for c in range(rt // RET_CHUNK):
        rs = slice(c * RET_CHUNK, (c + 1) * RET_CHUNK)
        for h in range(N_HEADS):
            cs = slice(h * HEAD_DIM, (h + 1) * HEAD_DIM)
            q = q_ref[rs, cs]
            k = k_ref[rs, cs]
            v = v_ref[rs, cs]
            a = _dot_nt(q, k) * dec_ref[h]
            o = _dot(a.astype(BF16), v)
            s_old = state[h]
            o = o + _dot(q, s_old.astype(BF16)) * xi_ref[h]
            kz = (k.astype(F32) * zeta_ref[h]).astype(BF16)
            state[h] = cd[h] * s_old + _dot_tn(kz, v)
            o_ref[rs, cs] = o


def _ret_bwd_kernel(q_ref, k_ref, v_ref, g_ref, r_ref, xi_ref, zeta_ref, gn_ref, o_ref, state,
                    *, rt, cd):
    @pl.when(pl.program_id(1) == 0)
    def _():
        state[...] = jnp.zeros_like(state)

    for c in reversed(range(rt // RET_CHUNK)):
        rs = slice(c * RET_CHUNK, (c + 1) * RET_CHUNK)
        for h in range(N_HEADS):
            cs = slice(h * HEAD_DIM, (h + 1) * HEAD_DIM)
            q = q_ref[rs, cs]
            k = k_ref[rs, cs]
            v = v_ref[rs, cs]
            s_old = state[h]
            r = r_ref[rs, cs] + _dot(q, s_old.astype(BF16)) * xi_ref[h]
            kz = (k.astype(F32) * zeta_ref[h]).astype(BF16)
            state[h] = cd[h] * s_old + _dot_tn(kz, v)
            mu = jnp.mean(r, axis=-1, keepdims=True)
            d = r - mu
            var = jnp.mean(d * d, axis=-1, keepdims=True)
            rn = d * lax.rsqrt(var + LN_EPS) * gn_ref[:, cs]
            g = g_ref[rs, cs].astype(F32)
            silu = g / (1.0 + jnp.exp(-g))
            o_ref[rs, cs] = (silu * rn).astype(BF16)


def _retention(h3, gn_g, batch, seq, tables):
    decay, xi_f, zeta_f, xi_b, zeta_b, cd_f, cd_b = tables
    rt = min(seq, RET_ROWS)
    nr = seq // rt
    tab = pl.BlockSpec((N_HEADS, RET_CHUNK, HEAD_DIM), lambda b, t: (0, 0, 0))
    state = pltpu.VMEM((N_HEADS, HEAD_DIM, HEAD_DIM), F32)

    fwd_seg = lambda seg: pl.BlockSpec((None, None, rt, SEG), lambda b, t: (seg, b, t, 0))
    r_fwd = pl.pallas_call(
        functools.partial(_ret_fwd_kernel, rt=rt, cd=cd_f),
        grid=(batch, nr),
        in_specs=[fwd_seg(SEG_QR), fwd_seg(SEG_KR), fwd_seg(SEG_VR), tab, tab, tab],
        out_specs=pl.BlockSpec((None, rt, SEG), lambda b, t: (b, t, 0)),
        out_shape=jax.ShapeDtypeStruct((batch, seq, SEG), F32),
        scratch_shapes=[state],
        compiler_params=_params("arbitrary", "arbitrary"),
        name="retention_fwd",
    )(h3, h3, h3, decay, xi_f, zeta_f)

    bwd_seg = lambda seg: pl.BlockSpec((None, None, rt, SEG),
                                       lambda b, t: (seg, b, nr - 1 - t, 0))
    return pl.pallas_call(
        functools.partial(_ret_bwd_kernel, rt=rt, cd=cd_b),
        grid=(batch, nr),
        in_specs=[bwd_seg(SEG_QR), bwd_seg(SEG_KR), bwd_seg(SEG_VR), bwd_seg(SEG_GR),
                  pl.BlockSpec((None, rt, SEG), lambda b, t: (b, nr - 1 - t, 0)),
                  tab, tab, pl.BlockSpec((1, SEG), lambda b, t: (0, 0))],
        out_specs=pl.BlockSpec((None, rt, SEG), lambda b, t: (b, nr - 1 - t, 0)),
        out_shape=jax.ShapeDtypeStruct((batch, seq, SEG), BF16),
        scratch_shapes=[state],
        compiler_params=_params("arbitrary", "arbitrary"),
        name="retention_bwd",
    )(h3, h3, h3, h3, r_fwd, xi_b, zeta_b, gn_g)


def _out_proj_kernel(o1_ref, o2_ref, o3_ref, l1_ref, l2_ref, l3_ref, ret_ref, x_ref, w_ref,
                     g_ref, b_ref, out_ref, attn_buf, nat_o, nat_l, *, alpha):
    rows = x_ref.shape[0]
    heads = [slice(h * HEAD_DIM, (h + 1) * HEAD_DIM) for h in range(N_HEADS)]
    for k, (dil, o_ref, l_ref) in enumerate(((4, o2_ref, l2_ref), (16, o3_ref, l3_ref))):
        for r in range(dil):
            dst = pl.ds(r, rows // dil, stride=dil)
            nat_l[k, dst, :] = l_ref[r]
            for h, cs in enumerate(heads):
                nat_o[k, h, dst, :] = o_ref[r, :, cs].astype(F32)
    la, lb, lc = l1_ref[...], nat_l[0], nat_l[1]
    m = jnp.maximum(jnp.maximum(la, lb), lc)
    ea, eb, ec = jnp.exp(la - m), jnp.exp(lb - m), jnp.exp(lc - m)
    inv = 1.0 / (ea + eb + ec)
    wa, wb, wc = ea * inv, eb * inv, ec * inv
    for h, cs in enumerate(heads):
        bc = lambda w: jnp.broadcast_to(w[:, h:h + 1], (rows, HEAD_DIM))
        mix = (bc(wa) * o1_ref[:, cs].astype(F32) + bc(wb) * nat_o[0, h] + bc(wc) * nat_o[1, h])
        attn_buf[:, cs] = mix.astype(BF16)
    y = _dot(attn_buf[...], w_ref[0:SEG, :]) + _dot(ret_ref[...], w_ref[SEG:2 * SEG, :])
    out_ref[...] = _layer_norm(alpha * x_ref[...] + y, g_ref[...], b_ref[...])


def _out_proj(outs, lses, ret_o, x2d, w_out, ln_g, ln_b, seq, alpha):
    t = x2d.shape[0]
    tm = ROW_TILE
    per_b = seq // tm
    row = lambda w: pl.BlockSpec((tm, w), lambda i: (i, 0))
    full = lambda a: pl.BlockSpec(a.shape, lambda i: (0,) * a.ndim)

    def branch(dil, width):
        return pl.BlockSpec((None, dil, tm // dil, width),
                            lambda i: (i // per_b, 0, i % per_b, 0))

    in_specs = ([row(SEG), branch(4, SEG), branch(16, SEG),
                 row(LANES), branch(4, LANES), branch(16, LANES)]
                + [row(SEG), row(D_MODEL), full(w_out), full(ln_g), full(ln_b)])
    return pl.pallas_call(
        functools.partial(_out_proj_kernel, alpha=alpha),
        grid=(t // tm,),
        in_specs=in_specs,
        out_specs=row(D_MODEL),
        out_shape=jax.ShapeDtypeStruct((t, D_MODEL), F32),
        scratch_shapes=[pltpu.VMEM((tm, SEG), BF16),
                        pltpu.VMEM((2, N_HEADS, tm, HEAD_DIM), F32),
                        pltpu.VMEM((2, tm, LANES), F32)],
        compiler_params=_params("arbitrary"),
        name="out_proj_ln1",
    )(outs[0].reshape(t, SEG), outs[1], outs[2], lses[0].reshape(t, LANES), lses[1], lses[2],
      ret_o, x2d, w_out, ln_g, ln_b)


def _mem_kv_kernel(m_ref, w_ref, o_ref):
    o_ref[...] = _dot(m_ref[...].astype(BF16), w_ref[...]).astype(BF16)


def _mem_kv(mem2d, w_mkv):
    rows = mem2d.shape[0]
    tm = ROW_TILE
    return pl.pallas_call(
        _mem_kv_kernel,
        grid=(rows // tm,),
        in_specs=[pl.BlockSpec((tm, D_MODEL), lambda i: (i, 0)),
                  pl.BlockSpec(w_mkv.shape, lambda i: (0, 0))],
        out_specs=pl.BlockSpec((tm, 2 * D_X), lambda i: (i, 0)),
        out_shape=jax.ShapeDtypeStruct((rows, 2 * D_X), BF16),
        compiler_params=_params("arbitrary"),
        name="mem_kv",
    )(mem2d, w_mkv)


def _route(logits):
    shape = logits.shape
    lane_i = lax.broadcasted_iota(jnp.int32, shape, 1)
    lane = lane_i.astype(F32)
    lowest = jnp.float32(-3.0e38)
    none = jnp.float32(LANES)
    rmax = lambda t: jnp.max(t, axis=-1, keepdims=True)
    rmin = lambda t: jnp.min(t, axis=-1, keepdims=True)

    is_group = lane_i < N_GROUPS
    gl = jnp.where(is_group, logits, lowest)
    g_max = rmax(gl)
    g_idx = rmin(jnp.where(gl == g_max, lane, none))
    p_g = 1.0 / jnp.sum(jnp.where(is_group, jnp.exp(logits - g_max), 0.0), axis=-1, keepdims=True)

    e_lo = N_GROUPS + EXP_PER_GROUP * g_idx
    el = jnp.where(lane >= e_lo, logits, lowest)
    el = jnp.where(lane < e_lo + EXP_PER_GROUP, el, lowest)
    v1 = rmax(el)
    i1 = rmin(jnp.where(el == v1, lane, none))
    el2 = jnp.where(lane == i1, lowest, el)
    v2 = rmax(el2)
    i2 = rmin(jnp.where(el2 == v2, lane, none))
    e2 = jnp.exp(v2 - v1)
    g1 = p_g / (1.0 + e2)
    g2 = p_g * e2 / (1.0 + e2)
    out = jnp.where(lane_i == 0, i1 - N_GROUPS,
                    jnp.where(lane_i == 1, i2 - N_GROUPS,
                              jnp.where(lane_i == 2, g1, jnp.where(lane_i == 3, g2, 0.0))))
    return out.astype(F32)


def _xattn_kernel(x_ref, kv_ref, wq_ref, wo_ref, g_ref, b_ref, wr_ref, br_ref,
                  x2_rows_ref, route_ref, obuf, *, alpha):
    sub = ROW_TILE
    for part in range(x_ref.shape[0] // sub):
        rs = slice(part * sub, (part + 1) * sub)
        x = x_ref[rs, :]
        q = _dot(x.astype(BF16), wq_ref[...]).astype(BF16)
        for h in range(X_HEADS):
            cs = slice(h * HEAD_DIM, (h + 1) * HEAD_DIM)
            k = kv_ref[:, cs]
            v = kv_ref[:, D_X + h * HEAD_DIM:D_X + (h + 1) * HEAD_DIM]
            s = _dot_nt(q[:, cs], k) * ATTN_SCALE
            m = jnp.max(s, axis=-1, keepdims=True)
            p = jnp.exp(s - m)
            l = jnp.sum(p, axis=-1, keepdims=True)
            obuf[rs, cs] = (_dot(p.astype(BF16), v) / l).astype(BF16)
        y = _dot(obuf[rs, :], wo_ref[...])
        x2 = _layer_norm(alpha * x + y, g_ref[...], b_ref[...])
        for s in range(TOKEN_ROWS):
            x2_rows_ref[pl.ds(part * sub * TOKEN_ROWS + s, sub, stride=TOKEN_ROWS), :] = (
                x2[:, s * LANES:(s + 1) * LANES])
        xh = x2.astype(BF16)
        xl = (x2 - xh.astype(F32)).astype(BF16)
        hw = _dot(xh, wr_ref[...])
        logits = hw[:, :LANES] + hw[:, LANES:] + _dot(xl, wr_ref[:, :LANES]) + br_ref[...]
        route_ref[rs, :] = _route(logits)


def _xattn(x1, kv, w_mq, w_mo, ln_g, ln_b, w_r, b_r, seq, alpha):
    t = x1.shape[0]
    tm = XATTN_ROWS
    per_b = seq // tm
    row = lambda w: pl.BlockSpec((tm, w), lambda i: (i, 0))
    full = lambda a: pl.BlockSpec(a.shape, lambda i: (0,) * a.ndim)
    return pl.pallas_call(
        functools.partial(_xattn_kernel, alpha=alpha),
        grid=(t // tm,),
        in_specs=[row(D_MODEL),
                  pl.BlockSpec((MEM_TOKENS, 2 * D_X), lambda i: (i // per_b, 0)),
                  full(w_mq), full(w_mo), full(ln_g), full(ln_b),
                  full(w_r), full(b_r)],
        out_specs=[pl.BlockSpec((tm * TOKEN_ROWS, LANES), lambda i: (i, 0)), row(LANES)],
        out_shape=[jax.ShapeDtypeStruct((t * TOKEN_ROWS, LANES), F32),
                   jax.ShapeDtypeStruct((t, LANES), F32)],
        scratch_shapes=[pltpu.VMEM((tm, D_X), BF16)],
        compiler_params=_params("arbitrary"),
        name="xattn_ln2_router",
    )(x1, kv, w_mq, w_mo, ln_g, ln_b, w_r, b_r)


def _moe_kernel(blk_e_ref, blk_nv_ref, src_hbm, x_hbm, w1_ref, w3_ref, w2_ref, y_ref,
                src_smem, xbuf0, xbuf1, xb_ref, hid_ref, sem_src, sem_g, *, nblk, bm):
    del blk_e_ref
    b = pl.program_id(0)
    cur = b % 2
    xbuf = (xbuf0, xbuf1)

    def rows_of(blk):
        return jnp.where(blk < nblk, blk_nv_ref[jnp.minimum(blk, nblk - 1)], 0)

    nv_cur, nv_next = rows_of(b), rows_of(b + 1)

    def src_copy(blk):
        return pltpu.make_async_copy(src_hbm.at[blk], src_smem.at[blk % 2], sem_src.at[blk % 2])

    tr, pitch = TOKEN_ROWS, MOE_PITCH

    def gather_row(slot, r):
        off = pl.multiple_of(src_smem[slot, r], tr)
        pltpu.make_async_copy(x_hbm.at[pl.ds(off, tr)], xbuf[slot].at[pl.ds(r * pitch, tr)],
                              sem_g.at[slot]).start(priority=r % 2)

    def issue_gathers(slot):
        for r in range(bm):
            gather_row(slot, r)

    def wait_gathers(slot):
        pltpu.make_async_copy(x_hbm.at[pl.ds(0, bm * tr)], xbuf[slot].at[pl.ds(0, bm * tr)],
                              sem_g.at[slot]).wait()

    def compute(slot, copies=()):
        copies = list(copies)
        per_piece = -(-len(copies) // MOE_COPY_PIECES)

        def drip():
            for issue in copies[:per_piece]:
                issue()
            del copies[:per_piece]

        nc = 2 * LANES
        for s in range(tr):
            xb_ref[:, s * LANES:(s + 1) * LANES] = (
                xbuf[slot][pl.ds(s, bm, stride=pitch), :].astype(BF16))
            drip()
        for c in range(D_EXPERT // nc):
            h1 = _dot(xb_ref[...], w1_ref[:, c * nc:(c + 1) * nc])
            drip()
            h3 = _dot(xb_ref[...], w3_ref[:, c * nc:(c + 1) * nc])
            hid_ref[:, c * nc:(c + 1) * nc] = (h1 / (1.0 + jnp.exp(-h1)) * h3).astype(BF16)
            drip()
        for c in range(D_MODEL // nc):
            y_ref[:, c * nc:(c + 1) * nc] = _dot(hid_ref[...], w2_ref[:, c * nc:(c + 1) * nc])
            drip()
        assert not copies

    @pl.when(b == 0)
    def _():
        src_copy(0).start()
        src_copy(0).wait()

        @pl.when(nv_cur > 0)
        def _():
            issue_gathers(0)
        if nblk > 1:
            src_copy(1).start()

    @pl.when(b + 1 < nblk)
    def _():
        src_copy(b + 1).wait()

    @pl.when(b + 2 < nblk)
    def _():
        src_copy(b + 2).start()

    for p in (0, 1):
        q = 1 - p
        mine = cur == p

        @pl.when(mine & (nv_cur > 0) & (nv_next > 0))
        def _():
            wait_gathers(p)
            compute(p, [functools.partial(gather_row, q, r) for r in range(bm)])

        @pl.when(mine & (nv_cur > 0) & (nv_next == 0))
        def _():
            wait_gathers(p)
            compute(p)

    @pl.when(nv_cur == 0)
    def _():
        y_ref[...] = jnp.zeros_like(y_ref)


def _moe_dispatch(route, bm):
    t = route.shape[0]
    m = 2 * t
    nblk = m // bm + N_EXPERTS
    eid = route[:, 0:2].astype(jnp.int32).T.reshape(m)
    counts = jnp.sum((eid[:, None] == jnp.arange(N_EXPERTS, dtype=jnp.int32)[None, :]).astype(jnp.int32),
                     axis=0)
    order = jnp.argsort(eid).astype(jnp.int32)
    rank = jnp.argsort(order).astype(jnp.int32)
    start = jnp.cumsum(counts) - counts
    nb_e = (counts + bm - 1) // bm
    bend = jnp.cumsum(nb_e)
    bstart = bend - nb_e
    blk = jnp.arange(nblk, dtype=jnp.int32)
    blk_e = jnp.minimum(jnp.sum((blk[:, None] >= bend[None, :]).astype(jnp.int32), axis=1),
                        N_EXPERTS - 1).astype(jnp.int32)
    within = blk - bstart[blk_e]
    blk_nv = jnp.where(blk < bend[-1], jnp.clip(counts[blk_e] - within * bm, 0, bm), 0).astype(jnp.int32)
    row = jnp.arange(bm, dtype=jnp.int32)[None, :]
    sorted_pos = (start[blk_e] + within * bm)[:, None] + row
    asg = order[jnp.clip(sorted_pos, 0, m - 1)]
    valid = row < blk_nv[:, None]
    row_src = jnp.where(valid, jnp.where(asg >= t, asg - t, asg), 0).astype(jnp.int32)
    row_of_asg = (bstart[eid] * bm + rank - start[eid]).astype(jnp.int32)
    return blk_e, blk_nv, row_src * TOKEN_ROWS, row_of_asg, nblk


def _moe(x2_rows, route, w1, w3, w2):
    bm = MOE_ROWS
    blk_e, blk_nv, row_src, row_of_asg, nblk = _moe_dispatch(route, bm)
    grid_spec = pltpu.PrefetchScalarGridSpec(
        num_scalar_prefetch=2,
        grid=(nblk,),
        in_specs=[pl.BlockSpec(memory_space=pl.ANY),
                  pl.BlockSpec(memory_space=pl.ANY),
                  pl.BlockSpec((None, D_MODEL, D_EXPERT), lambda b, be, nv: (be[b], 0, 0)),
                  pl.BlockSpec((None, D_MODEL, D_EXPERT), lambda b, be, nv: (be[b], 0, 0)),
                  pl.BlockSpec((None, D_EXPERT, D_MODEL), lambda b, be, nv: (be[b], 0, 0))],
        out_specs=pl.BlockSpec((bm, D_MODEL), lambda b, be, nv: (b, 0)),
        scratch_shapes=[pltpu.SMEM((2, bm), jnp.int32),
                        pltpu.VMEM((bm * MOE_PITCH, LANES), F32),
                        pltpu.VMEM((bm * MOE_PITCH, LANES), F32),
                        pltpu.VMEM((bm, D_MODEL), BF16),
                        pltpu.VMEM((bm, D_EXPERT), BF16),
                        pltpu.SemaphoreType.DMA((2,)),
                        pltpu.SemaphoreType.DMA((2,))],
    )
    y_sorted = pl.pallas_call(
        functools.partial(_moe_kernel, nblk=nblk, bm=bm),
        grid_spec=grid_spec,
        out_shape=jax.ShapeDtypeStruct((nblk * bm, D_MODEL), F32),
        compiler_params=pltpu.CompilerParams(dimension_semantics=("arbitrary",),
                                             vmem_limit_bytes=VMEM_LIMIT,
                                             disable_bounds_checks=True),
        name="moe_experts",
    )(blk_e, blk_nv, row_src, x2_rows, w1, w3, w2)
    return y_sorted, row_of_asg


def _final_kernel(pos_ref, x_ref, y_hbm, route_ref, g_ref, b_ref, o_ref, ybuf, z_ref, sem,
                  *, alpha, tm, t, nt):
    i = pl.program_id(0)

    def issue(tile, slot):
        for k in range(2):
            for r in range(tm):
                p = pos_ref[k * t + tile * tm + r]
                pltpu.make_async_copy(y_hbm.at[pl.ds(p, 1)], ybuf.at[slot, pl.ds(k * tm + r, 1)],
                                      sem.at[slot]).start(priority=r % 2)

    @pl.when(i == 0)
    def _():
        issue(0, 0)

    @pl.when(i + 1 < nt)
    def _():
        issue(i + 1, (i + 1) % 2)

    slot = i % 2
    pltpu.make_async_copy(y_hbm.at[pl.ds(0, 2 * tm)], ybuf.at[slot], sem.at[slot]).wait()
    r = route_ref[...]
    g0 = jnp.broadcast_to(r[:, 2:3], (tm, LANES))
    g1 = jnp.broadcast_to(r[:, 3:4], (tm, LANES))
    for s in range(TOKEN_ROWS):
        cs = slice(s * LANES, (s + 1) * LANES)
        z_ref[:, cs] = (alpha * x_ref[pl.ds(s, tm, stride=TOKEN_ROWS), :]
                        + g0 * ybuf[slot, 0:tm, cs] + g1 * ybuf[slot, tm:2 * tm, cs])
    o_ref[...] = _layer_norm(z_ref[...], g_ref[...], b_ref[...])


def _final(x2_rows, y_sorted, row_of_asg, route, ln_g, ln_b, alpha):
    t = route.shape[0]
    tm = ROW_TILE
    nt = t // tm
    grid_spec = pltpu.PrefetchScalarGridSpec(
        num_scalar_prefetch=1,
        grid=(nt,),
        in_specs=[pl.BlockSpec((tm * TOKEN_ROWS, LANES), lambda i, pos: (i, 0)),
                  pl.BlockSpec(memory_space=pl.ANY),
                  pl.BlockSpec((tm, LANES), lambda i, pos: (i, 0)),
                  pl.BlockSpec(ln_g.shape, lambda i, pos: (0, 0)),
                  pl.BlockSpec(ln_b.shape, lambda i, pos: (0, 0))],
        out_specs=pl.BlockSpec((tm, D_MODEL), lambda i, pos: (i, 0)),
        scratch_shapes=[pltpu.VMEM((2, 2 * tm, D_MODEL), F32),
                        pltpu.VMEM((tm, D_MODEL), F32),
                        pltpu.SemaphoreType.DMA((2,))],
    )
    return pl.pallas_call(
        functools.partial(_final_kernel, alpha=alpha, tm=tm, t=t, nt=nt),
        grid_spec=grid_spec,
        out_shape=jax.ShapeDtypeStruct((t, D_MODEL), F32),
        compiler_params=pltpu.CompilerParams(dimension_semantics=("arbitrary",),
                                             vmem_limit_bytes=VMEM_LIMIT,
                                             disable_bounds_checks=True),
        name="moe_combine_ln3",
    )(row_of_asg, x2_rows, y_sorted, route, ln_g, ln_b)


def _rotary_tables(seq):
    half = HEAD_DIM // 2
    inv_freq = ROPE_THETA ** (-jnp.arange(half, dtype=F32) / half)
    ang = jnp.arange(seq, dtype=F32)[:, None] * inv_freq[None, :]
    cos, sin = jnp.cos(ang), jnp.sin(ang)
    return jnp.concatenate([cos, cos], axis=-1), jnp.concatenate([-sin, sin], axis=-1)


def _prepare_weights(l, w_in, ret_gn_g, w_out, ln1_g, ln1_b, w_mq, w_mkv, w_mo, ln2_g, ln2_b,
                     w_gr, b_gr, w_er, b_er, w1, w3, w2, ln3_g, ln3_b):
    row = lambda v: v[l].reshape(1, -1).astype(F32)
    pad = LANES - N_GROUPS - N_EXPERTS
    w_r = jnp.concatenate([w_gr[l], w_er[l], jnp.zeros((D_MODEL, pad), F32)], axis=1)
    wr_hi = w_r.astype(BF16)
    wr_lo = (w_r - wr_hi.astype(F32)).astype(BF16)
    b_r = jnp.concatenate([b_gr[l], b_er[l], jnp.zeros((pad,), F32)]).reshape(1, LANES)
    return dict(
        w_in=w_in[l].astype(BF16), gn_g=row(ret_gn_g), w_out=w_out[l].astype(BF16),
        ln1=(row(ln1_g), row(ln1_b)), w_mq=w_mq[l].astype(BF16), w_mkv=w_mkv[l].astype(BF16),
        w_mo=w_mo[l].astype(BF16), ln2=(row(ln2_g), row(ln2_b)),
        w_r=jnp.concatenate([wr_hi, wr_lo], axis=1), b_r=b_r,
        w1=w1[l].astype(BF16), w3=w3[l].astype(BF16), w2=w2[l].astype(BF16),
        ln3=(row(ln3_g), row(ln3_b)))


def _encoder_layer(x, mem, w, alpha, ret_tables):
    batch, seq, _ = x.shape
    t = batch * seq
    x2d = x.reshape(t, D_MODEL)
    cos, sin = _rotary_tables(seq)
    h, ha4, ha16 = _in_proj(x2d, w['w_in'], cos, sin, batch, seq)
    outs, lses = [], []
    for dil, qkv in zip(DILATIONS, (h.reshape(N_SEG, batch, 1, seq, SEG), ha4, ha16)):
        o, lse = _local_attn(qkv, batch, seq, dil)
        outs.append(o)
        lses.append(lse)
    ret_o = _retention(h.reshape(N_SEG, batch, seq, SEG), w['gn_g'], batch, seq, ret_tables)
    x1 = _out_proj(outs, lses, ret_o.reshape(t, SEG), x2d, w['w_out'], *w['ln1'], seq, alpha)
    kv = _mem_kv(mem.reshape(batch * MEM_TOKENS, D_MODEL), w['w_mkv'])
    x2_rows, route = _xattn(x1, kv, w['w_mq'], w['w_mo'], *w['ln2'], w['w_r'], w['b_r'], seq, alpha)
    y_sorted, row_of_asg = _moe(x2_rows, route, w['w1'], w['w3'], w['w2'])
    out = _final(x2_rows, y_sorted, row_of_asg, route, *w['ln3'], alpha)
    return out.reshape(batch, seq, D_MODEL)


def kernel(x_prompt, x_sample, mem_prompt, mem_sample, w_in, ret_gn_g, w_out, ln1_g, ln1_b,
           w_mq, w_mkv, w_mo, ln2_g, ln2_b, w_gr, b_gr, w_er, b_er, w1, w3, w2, ln3_g, ln3_b):
    depth = w_in.shape[0]
    alpha = (2 * depth) ** 0.25
    ret_tables = _retention_tables()
    y_prompt, y_sample = x_prompt, x_sample
    for l in range(depth):
        w = _prepare_weights(l, w_in, ret_gn_g, w_out, ln1_g, ln1_b, w_mq, w_mkv, w_mo, ln2_g,
                             ln2_b, w_gr, b_gr, w_er, b_er, w1, w3, w2, ln3_g, ln3_b)
        y_prompt = _encoder_layer(y_prompt, mem_prompt, w, alpha, ret_tables)
        y_sample = _encoder_layer(y_sample, mem_sample, w, alpha, ret_tables)
    return (y_prompt, y_sample)
```

```python
import functools

import numpy as np
import jax
import jax.numpy as jnp
from jax import lax
from jax.experimental import pallas as pl
from jax.experimental.pallas import tpu as pltpu

F32 = jnp.float32
BF16 = jnp.bfloat16

D_MODEL = 2048
HEAD_DIM = 128
N_HEADS = 8
SEG = N_HEADS * HEAD_DIM
N_SEG = 7
SEG_QA, SEG_KA, SEG_VA, SEG_QR, SEG_KR, SEG_VR, SEG_GR = range(7)
DILATIONS = (1, 4, 16)
HALF_WIN = 64
ROPE_THETA = 10000.0
RET_CHUNK = 128
MEM_TOKENS = 256
X_HEADS = 4
D_X = X_HEADS * HEAD_DIM
N_GROUPS = 4
EXP_PER_GROUP = 8
N_EXPERTS = N_GROUPS * EXP_PER_GROUP
D_EXPERT = D_MODEL // 4
LN_EPS = 1e-5
NEG_INF = -1e30
ATTN_SCALE = HEAD_DIM ** -0.5
LOG2_E = 1.4426950408889634
LN_2 = 0.6931471805599453

LANES = 128
VMEM_LIMIT = 52 * 1024 * 1024

IN_PROJ_ROWS = 1024
IN_PROJ_VMEM = 58 * 1024 * 1024
N_SLABS = 4
TOKEN_ROWS = D_MODEL // LANES
MOE_PITCH = 24
MOE_COPY_PIECES = 12
ATTN_ROWS = 512
RET_ROWS = 512
ROW_TILE = 256
XATTN_ROWS = 512
MOE_ROWS = 256


def _params(*sem):
    return pltpu.CompilerParams(dimension_semantics=sem, vmem_limit_bytes=VMEM_LIMIT)


def _layer_norm(z, g, b):
    mu = jnp.mean(z, axis=-1, keepdims=True)
    d = z - mu
    var = jnp.mean(d * d, axis=-1, keepdims=True)
    return d * lax.rsqrt(var + LN_EPS) * g + b


def _dot_nt(a, b):
    return lax.dot_general(a, b, (((1,), (1,)), ((), ())), preferred_element_type=F32)


def _dot_tn(a, b):
    return lax.dot_general(a, b, (((0,), (0,)), ((), ())), preferred_element_type=F32)


def _dot(a, b):
    return jnp.dot(a, b, preferred_element_type=F32)


def _in_proj_kernel(x_ref, w_ref, cos_ref, sin_ref, o_ref, o4_ref, o16_ref, xb_ref, slab, slab4,
                    *, tm):
    j = pl.program_id(1)

    @pl.when(j == 0)
    def _():
        xb_ref[...] = x_ref[...].astype(BF16)

    def segment(rot, strided):
        if rot:
            scale = jnp.where(j == SEG_KR, ATTN_SCALE,
                              jnp.where(j == SEG_QA, ATTN_SCALE * LOG2_E, 1.0)).astype(F32)
            c = cos_ref[...] * scale
            s = sin_ref[...] * scale
        for pair in range(N_HEADS // 2):
            acc = _dot(xb_ref[...], w_ref[:, pair * 2 * HEAD_DIM:(pair + 1) * 2 * HEAD_DIM])
            for hh in range(2):
                h = 2 * pair + hh
                cs = slice(h * HEAD_DIM, (h + 1) * HEAD_DIM)
                t = acc[:, hh * HEAD_DIM:(hh + 1) * HEAD_DIM]
                if rot:
                    t = t * c + pltpu.roll(t, HEAD_DIM // 2, 1) * s
                o_ref[:, cs] = t.astype(BF16)
                if strided:
                    k = h % N_SLABS
                    slab[k] = t
                    for r in range(4):
                        v4 = slab[k, pl.ds(r, tm // 4, stride=4), :]
                        o4_ref[r, :, cs] = v4.astype(BF16)
                        slab4[k, r] = v4
                        for a in range(4):
                            o16_ref[r + 4 * a, :, cs] = (
                                slab4[k, r, pl.ds(a, tm // 16, stride=4), :].astype(BF16))

    is_rot = (j == SEG_QA) | (j == SEG_KA) | (j == SEG_QR) | (j == SEG_KR)
    is_attn = j <= SEG_VA
    for rot in (True, False):
        for strided in (True, False):
            cond = (is_rot if rot else jnp.logical_not(is_rot)) & (
                is_attn if strided else jnp.logical_not(is_attn))
            pl.when(cond)(functools.partial(segment, rot, strided))


def _in_proj(x2d, w_in, cos, sin, batch, seq):
    t = x2d.shape[0]
    tm = IN_PROJ_ROWS
    n_s = seq // tm
    n_att = SEG_VA + 1

    def strided_out(dil):
        shape = jax.ShapeDtypeStruct((n_att, batch, dil, seq // dil, SEG), BF16)
        spec = pl.BlockSpec((None, None, dil, tm // dil, SEG),
                            lambda i, j: (jnp.minimum(j, SEG_VA), i // n_s, 0, i % n_s, 0))
        return shape, spec

    (shape4, spec4), (shape16, spec16) = strided_out(4), strided_out(16)
    return pl.pallas_call(
        functools.partial(_in_proj_kernel, tm=tm),
        grid=(t // tm, N_SEG),
        in_specs=[
            pl.BlockSpec((tm, D_MODEL), lambda i, j: (i, 0)),
            pl.BlockSpec((D_MODEL, SEG), lambda i, j: (0, j)),
            pl.BlockSpec((tm, HEAD_DIM), lambda i, j: (i % n_s, 0)),
            pl.BlockSpec((tm, HEAD_DIM), lambda i, j: (i % n_s, 0)),
        ],
        out_specs=[pl.BlockSpec((None, tm, SEG), lambda i, j: (j, i, 0)), spec4, spec16],
        out_shape=[jax.ShapeDtypeStruct((N_SEG, t, SEG), BF16), shape4, shape16],
        scratch_shapes=[pltpu.VMEM((tm, D_MODEL), BF16),
                        pltpu.VMEM((N_SLABS, tm, HEAD_DIM), F32),
                        pltpu.VMEM((N_SLABS, 4, tm // 4, HEAD_DIM), F32)],
        compiler_params=pltpu.CompilerParams(dimension_semantics=("arbitrary", "arbitrary"),
                                             vmem_limit_bytes=IN_PROJ_VMEM),
        name="in_proj",
    )(x2d, w_in, cos, sin)


def _local_attn_kernel(q_ref, kp_ref, kc_ref, kn_ref, vp_ref, vc_ref, vn_ref,
                       o_ref, lse_ref, kbuf, vbuf, *, lt, sub_len):
    i = pl.program_id(2)
    hw = HALF_WIN
    kbuf[0:hw, :] = kp_ref[...]
    kbuf[hw:hw + lt, :] = kc_ref[...]
    kbuf[hw + lt:2 * hw + lt, :] = kn_ref[...]
    vbuf[0:hw, :] = vp_ref[...]
    vbuf[hw:hw + lt, :] = vc_ref[...]
    vbuf[hw + lt:2 * hw + lt, :] = vn_ref[...]

    qb = 128
    kb = qb + 2 * hw
    row = lax.broadcasted_iota(jnp.int32, (qb, kb), 0)
    col = lax.broadcasted_iota(jnp.int32, (qb, kb), 1)
    band = jnp.abs(row + hw - col) <= hw
    lane = lax.broadcasted_iota(jnp.int32, (qb, LANES), 1)

    def body(j, carry):
        r0 = pl.multiple_of(j * qb, qb)
        kpos = i * lt + r0 - hw + col
        bias = jnp.where(band, 0.0, NEG_INF).astype(F32)
        bias = jnp.where(kpos >= 0, bias, NEG_INF)
        bias = jnp.where(kpos < sub_len, bias, NEG_INF)
        lse_tile = jnp.zeros((qb, LANES), F32)
        for h in range(N_HEADS):
            cs = slice(h * HEAD_DIM, (h + 1) * HEAD_DIM)
            q = q_ref[pl.ds(r0, qb), cs]
            k = kbuf[pl.ds(r0, kb), cs]
            v = vbuf[pl.ds(r0, kb), cs]
            s = _dot_nt(q, k) + bias
            m = jnp.max(s, axis=-1, keepdims=True)
            p = jnp.exp2(s - m)
            l = jnp.sum(p, axis=-1, keepdims=True)
            o = _dot(p.astype(BF16), v) / l
            o_ref[pl.ds(r0, qb), cs] = o.astype(BF16)
            lse_tile = jnp.where(lane == h, (m + jnp.log2(l)) * LN_2, lse_tile)
        lse_ref[pl.ds(r0, qb), :] = lse_tile
        return carry

    lax.fori_loop(0, lt // qb, body, 0)


def _local_attn(qkv, batch, seq, dil):
    sub_len = seq // dil
    lt = min(sub_len, ATTN_ROWS)
    hw = HALF_WIN
    n_halo = sub_len // hw
    per = lt // hw

    def main(seg):
        return pl.BlockSpec((None, None, None, lt, SEG), lambda b, r, i: (seg, b, r, i, 0))

    def prev(seg):
        return pl.BlockSpec((None, None, None, hw, SEG),
                            lambda b, r, i: (seg, b, r, jnp.maximum(i * per - 1, 0), 0))

    def nxt(seg):
        return pl.BlockSpec((None, None, None, hw, SEG),
                            lambda b, r, i: (seg, b, r, jnp.minimum((i + 1) * per, n_halo - 1), 0))

    kern = functools.partial(_local_attn_kernel, lt=lt, sub_len=sub_len)
    return pl.pallas_call(
        kern,
        grid=(batch, dil, sub_len // lt),
        in_specs=[main(SEG_QA), prev(SEG_KA), main(SEG_KA), nxt(SEG_KA),
                  prev(SEG_VA), main(SEG_VA), nxt(SEG_VA)],
        out_specs=[pl.BlockSpec((None, None, lt, SEG), lambda b, r, i: (b, r, i, 0)),
                   pl.BlockSpec((None, None, lt, LANES), lambda b, r, i: (b, r, i, 0))],
        out_shape=[jax.ShapeDtypeStruct((batch, dil, sub_len, SEG), BF16),
                   jax.ShapeDtypeStruct((batch, dil, sub_len, LANES), F32)],
        scratch_shapes=[pltpu.VMEM((lt + 2 * hw, SEG), BF16),
                        pltpu.VMEM((lt + 2 * hw, SEG), BF16)],
        compiler_params=_params("arbitrary", "arbitrary", "arbitrary"),
        name="local_attn_d%d" % dil,
    )(qkv, qkv, qkv, qkv, qkv, qkv, qkv)


def _retention_tables():
    h = np.arange(N_HEADS, dtype=np.float64)
    gf = 1.0 - 2.0 ** (-5.0 - h)
    gb = 1.0 - 2.0 ** (-5.5 - h)
    c = RET_CHUNK
    idx = np.arange(c, dtype=np.float64)
    diff = idx[:, None] - idx[None, :]
    dec_f = np.where(diff >= 0, gf[:, None, None] ** np.maximum(diff, 0.0), 0.0)
    dec_b = np.where(diff < 0, gb[:, None, None] ** np.maximum(-diff, 0.0), 0.0)
    decay = dec_f + dec_b
    rows = lambda t: np.broadcast_to(t[:, :, None], (N_HEADS, c, HEAD_DIM))
    xi_f = rows(gf[:, None] ** (idx + 1.0)[None])
    zeta_f = rows(gf[:, None] ** (c - 1.0 - idx)[None])
    xi_b = rows(gb[:, None] ** (c - idx)[None])
    zeta_b = rows(gb[:, None] ** idx[None])
    f = lambda t: jnp.asarray(np.ascontiguousarray(t), F32)
    return (f(decay), f(xi_f), f(zeta_f), f(xi_b), f(zeta_b),
            tuple(float(g ** c) for g in gf), tuple(float(g ** c) for g in gb))


def _ret_fwd_kernel(q_ref, k_ref, v_ref, dec_ref, xi_ref, zeta_ref, o_ref, state, *, rt, cd):
    @pl.when(pl.program_id(1) == 0)
    def _():
        state[...] = jnp.zeros_like(state)

    for c in range(rt // RET_CHUNK):
        rs = slice(c * RET_CHUNK, (c + 1) * RET_CHUNK)
        for h in range(N_HEADS):
            cs = slice(h * HEAD_DIM, (h + 1) * HEAD_DIM)
            q = q_ref[rs, cs]
            k = k_ref[rs, cs]
            v = v_ref[rs, cs]
            a = _dot_nt(q, k) * dec_ref[h]
            o = _dot(a.astype(BF16), v)
            s_old = state[h]
            o = o + _dot(q, s_old.astype(BF16)) * xi_ref[h]
            kz = (k.astype(F32) * zeta_ref[h]).astype(BF16)
            state[h] = cd[h] * s_old + _dot_tn(kz, v)
            o_ref[rs, cs] = o


def _ret_bwd_kernel(q_ref, k_ref, v_ref, g_ref, r_ref, xi_ref, zeta_ref, gn_ref, o_ref, state,
                    *, rt, cd):
    @pl.when(pl.program_id(1) == 0)
    def _():
        state[...] = jnp.zeros_like(state)

    for c in reversed(range(rt // RET_CHUNK)):
        rs = slice(c * RET_CHUNK, (c + 1) * RET_CHUNK)
        for h in range(N_HEADS):
            cs = slice(h * HEAD_DIM, (h + 1) * HEAD_DIM)
            q = q_ref[rs, cs]
            k = k_ref[rs, cs]
            v = v_ref[rs, cs]
            s_old = state[h]
            r = r_ref[rs, cs] + _dot(q, s_old.astype(BF16)) * xi_ref[h]
            kz = (k.astype(F32) * zeta_ref[h]).astype(BF16)
            state[h] = cd[h] * s_old + _dot_tn(kz, v)
            mu = jnp.mean(r, axis=-1, keepdims=True)
            d = r - mu
            var = jnp.mean(d * d, axis=-1, keepdims=True)
            rn = d * lax.rsqrt(var + LN_EPS) * gn_ref[:, cs]
            g = g_ref[rs, cs].astype(F32)
            silu = g / (1.0 + jnp.exp(-g))
            o_ref[rs, cs] = (silu * rn).astype(BF16)


def _retention(h3, gn_g, batch, seq, tables):
    decay, xi_f, zeta_f, xi_b, zeta_b, cd_f, cd_b = tables
    rt = min(seq, RET_ROWS)
    nr = seq // rt
    tab = pl.BlockSpec((N_HEADS, RET_CHUNK, HEAD_DIM), lambda b, t: (0, 0, 0))
    state = pltpu.VMEM((N_HEADS, HEAD_DIM, HEAD_DIM), F32)

    fwd_seg = lambda seg: pl.BlockSpec((None, None, rt, SEG), lambda b, t: (seg, b, t, 0))
    r_fwd = pl.pallas_call(
        functools.partial(_ret_fwd_kernel, rt=rt, cd=cd_f),
        grid=(batch, nr),
        in_specs=[fwd_seg(SEG_QR), fwd_seg(SEG_KR), fwd_seg(SEG_VR), tab, tab, tab],
        out_specs=pl.BlockSpec((None, rt, SEG), lambda b, t: (b, t, 0)),
        out_shape=jax.ShapeDtypeStruct((batch, seq, SEG), F32),
        scratch_shapes=[state],
        compiler_params=_params("arbitrary", "arbitrary"),
        name="retention_fwd",
    )(h3, h3, h3, decay, xi_f, zeta_f)

    bwd_seg = lambda seg: pl.BlockSpec((None, None, rt, SEG),
                                       lambda b, t: (seg, b, nr - 1 - t, 0))
    return pl.pallas_call(
        functools.partial(_ret_bwd_kernel, rt=rt, cd=cd_b),
        grid=(batch, nr),
        in_specs=[bwd_seg(SEG_QR), bwd_seg(SEG_KR), bwd_seg(SEG_VR), bwd_seg(SEG_GR),
                  pl.BlockSpec((None, rt, SEG), lambda b, t: (b, nr - 1 - t, 0)),
                  tab, tab, pl.BlockSpec((1, SEG), lambda b, t: (0, 0))],
        out_specs=pl.BlockSpec((None, rt, SEG), lambda b, t: (b, nr - 1 - t, 0)),
        out_shape=jax.ShapeDtypeStruct((batch, seq, SEG), BF16),
        scratch_shapes=[state],
        compiler_params=_params("arbitrary", "arbitrary"),
        name="retention_bwd",
    )(h3, h3, h3, h3, r_fwd, xi_b, zeta_b, gn_g)


def _out_proj_kernel(o1_ref, o2_ref, o3_ref, l1_ref, l2_ref, l3_ref, ret_ref, x_ref, w_ref,
                     g_ref, b_ref, out_ref, attn_buf, nat_o, nat_l, *, alpha):
    rows = x_ref.shape[0]
    heads = [slice(h * HEAD_DIM, (h + 1) * HEAD_DIM) for h in range(N_HEADS)]
    for k, (dil, o_ref, l_ref) in enumerate(((4, o2_ref, l2_ref), (16, o3_ref, l3_ref))):
        for r in range(dil):
            dst = pl.ds(r, rows // dil, stride=dil)
            nat_l[k, dst, :] = l_ref[r]
            for h, cs in enumerate(heads):
                nat_o[k, h, dst, :] = o_ref[r, :, cs].astype(F32)
    la, lb, lc = l1_ref[...], nat_l[0], nat_l[1]
    m = jnp.maximum(jnp.maximum(la, lb), lc)
    ea, eb, ec = jnp.exp(la - m), jnp.exp(lb - m), jnp.exp(lc - m)
    inv = 1.0 / (ea + eb + ec)
    wa, wb, wc = ea * inv, eb * inv, ec * inv
    for h, cs in enumerate(heads):
        bc = lambda w: jnp.broadcast_to(w[:, h:h + 1], (rows, HEAD_DIM))
        mix = (bc(wa) * o1_ref[:, cs].astype(F32) + bc(wb) * nat_o[0, h] + bc(wc) * nat_o[1, h])
        attn_buf[:, cs] = mix.astype(BF16)
    y = _dot(attn_buf[...], w_ref[0:SEG, :]) + _dot(ret_ref[...], w_ref[SEG:2 * SEG, :])
    out_ref[...] = _layer_norm(alpha * x_ref[...] + y, g_ref[...], b_ref[...])


def _out_proj(outs, lses, ret_o, x2d, w_out, ln_g, ln_b, seq, alpha):
    t = x2d.shape[0]
    tm = ROW_TILE
    per_b = seq // tm
    row = lambda w: pl.BlockSpec((tm, w), lambda i: (i, 0))
    full = lambda a: pl.BlockSpec(a.shape, lambda i: (0,) * a.ndim)

    def branch(dil, width):
        return pl.BlockSpec((None, dil, tm // dil, width),
                            lambda i: (i // per_b, 0, i % per_b, 0))

    in_specs = ([row(SEG), branch(4, SEG), branch(16, SEG),
                 row(LANES), branch(4, LANES), branch(16, LANES)]
                + [row(SEG), row(D_MODEL), full(w_out), full(ln_g), full(ln_b)])
    return pl.pallas_call(
        functools.partial(_out_proj_kernel, alpha=alpha),
        grid=(t // tm,),
        in_specs=in_specs,
        out_specs=row(D_MODEL),
        out_shape=jax.ShapeDtypeStruct((t, D_MODEL), F32),
        scratch_shapes=[pltpu.VMEM((tm, SEG), BF16),
                        pltpu.VMEM((2, N_HEADS, tm, HEAD_DIM), F32),
                        pltpu.VMEM((2, tm, LANES), F32)],
        compiler_params=_params("arbitrary"),
        name="out_proj_ln1",
    )(outs[0].reshape(t, SEG), outs[1], outs[2], lses[0].reshape(t, LANES), lses[1], lses[2],
      ret_o, x2d, w_out, ln_g, ln_b)


def _mem_kv_kernel(m_ref, w_ref, o_ref):
    o_ref[...] = _dot(m_ref[...].astype(BF16), w_ref[...]).astype(BF16)


def _mem_kv(mem2d, w_mkv):
    rows = mem2d.shape[0]
    tm = ROW_TILE
    return pl.pallas_call(
        _mem_kv_kernel,
        grid=(rows // tm,),
        in_specs=[pl.BlockSpec((tm, D_MODEL), lambda i: (i, 0)),
                  pl.BlockSpec(w_mkv.shape, lambda i: (0, 0))],
        out_specs=pl.BlockSpec((tm, 2 * D_X), lambda i: (i, 0)),
        out_shape=jax.ShapeDtypeStruct((rows, 2 * D_X), BF16),
        compiler_params=_params("arbitrary"),
        name="mem_kv",
    )(mem2d, w_mkv)


def _route(logits):
    shape = logits.shape
    lane_i = lax.broadcasted_iota(jnp.int32, shape, 1)
    lane = lane_i.astype(F32)
    lowest = jnp.float32(-3.0e38)
    none = jnp.float32(LANES)
    rmax = lambda t: jnp.max(t, axis=-1, keepdims=True)
    rmin = lambda t: jnp.min(t, axis=-1, keepdims=True)

    is_group = lane_i < N_GROUPS
    gl = jnp.where(is_group, logits, lowest)
    g_max = rmax(gl)
    g_idx = rmin(jnp.where(gl == g_max, lane, none))
    p_g = 1.0 / jnp.sum(jnp.where(is_group, jnp.exp(logits - g_max), 0.0), axis=-1, keepdims=True)

    e_lo = N_GROUPS + EXP_PER_GROUP * g_idx
    el = jnp.where(lane >= e_lo, logits, lowest)
    el = jnp.where(lane < e_lo + EXP_PER_GROUP, el, lowest)
    v1 = rmax(el)
    i1 = rmin(jnp.where(el == v1, lane, none))
    el2 = jnp.where(lane == i1, lowest, el)
    v2 = rmax(el2)
    i2 = rmin(jnp.where(el2 == v2, lane, none))
    e2 = jnp.exp(v2 - v1)
    g1 = p_g / (1.0 + e2)
    g2 = p_g * e2 / (1.0 + e2)
    out = jnp.where(lane_i == 0, i1 - N_GROUPS,
                    jnp.where(lane_i == 1, i2 - N_GROUPS,
                              jnp.where(lane_i == 2, g1, jnp.where(lane_i == 3, g2, 0.0))))
    return out.astype(F32)


def _xattn_kernel(x_ref, kv_ref, wq_ref, wo_ref, g_ref, b_ref, wr_ref, br_ref,
                  x2_rows_ref, route_ref, obuf, *, alpha):
    sub = ROW_TILE
    for part in range(x_ref.shape[0] // sub):
        rs = slice(part * sub, (part + 1) * sub)
        x = x_ref[rs, :]
        q = _dot(x.astype(BF16), wq_ref[...]).astype(BF16)
        for h in range(X_HEADS):
            cs = slice(h * HEAD_DIM, (h + 1) * HEAD_DIM)
            k = kv_ref[:, cs]
            v = kv_ref[:, D_X + h * HEAD_DIM:D_X + (h + 1) * HEAD_DIM]
            s = _dot_nt(q[:, cs], k) * ATTN_SCALE
            m = jnp.max(s, axis=-1, keepdims=True)
            p = jnp.exp(s - m)
            l = jnp.sum(p, axis=-1, keepdims=True)
            obuf[rs, cs] = (_dot(p.astype(BF16), v) / l).astype(BF16)
        y = _dot(obuf[rs, :], wo_ref[...])
        x2 = _layer_norm(alpha * x + y, g_ref[...], b_ref[...])
        for s in range(TOKEN_ROWS):
            x2_rows_ref[pl.ds(part * sub * TOKEN_ROWS + s, sub, stride=TOKEN_ROWS), :] = (
                x2[:, s * LANES:(s + 1) * LANES])
        xh = x2.astype(BF16)
        xl = (x2 - xh.astype(F32)).astype(BF16)
        hw = _dot(xh, wr_ref[...])
        logits = hw[:, :LANES] + hw[:, LANES:] + _dot(xl, wr_ref[:, :LANES]) + br_ref[...]
        route_ref[rs, :] = _route(logits)


def _xattn(x1, kv, w_mq, w_mo, ln_g, ln_b, w_r, b_r, seq, alpha):
    t = x1.shape[0]
    tm = XATTN_ROWS
    per_b = seq // tm
    row = lambda w: pl.BlockSpec((tm, w), lambda i: (i, 0))
    full = lambda a: pl.BlockSpec(a.shape, lambda i: (0,) * a.ndim)
    return pl.pallas_call(
        functools.partial(_xattn_kernel, alpha=alpha),
        grid=(t // tm,),
        in_specs=[row(D_MODEL),
                  pl.BlockSpec((MEM_TOKENS, 2 * D_X), lambda i: (i // per_b, 0)),
                  full(w_mq), full(w_mo), full(ln_g), full(ln_b),
                  full(w_r), full(b_r)],
        out_specs=[pl.BlockSpec((tm * TOKEN_ROWS, LANES), lambda i: (i, 0)), row(LANES)],
        out_shape=[jax.ShapeDtypeStruct((t * TOKEN_ROWS, LANES), F32),
                   jax.ShapeDtypeStruct((t, LANES), F32)],
        scratch_shapes=[pltpu.VMEM((tm, D_X), BF16)],
        compiler_params=_params("arbitrary"),
        name="xattn_ln2_router",
    )(x1, kv, w_mq, w_mo, ln_g, ln_b, w_r, b_r)


def _moe_kernel(blk_e_ref, blk_nv_ref, src_hbm, x_hbm, w1_ref, w3_ref, w2_ref, y_ref,
                src_smem, xbuf0, xbuf1, xb_ref, hid_ref, sem_src, sem_g, *, nblk, bm):
    del blk_e_ref
    b = pl.program_id(0)
    cur = b % 2
    xbuf = (xbuf0, xbuf1)

    def rows_of(blk):
        return jnp.where(blk < nblk, blk_nv_ref[jnp.minimum(blk, nblk - 1)], 0)

    nv_cur, nv_next = rows_of(b), rows_of(b + 1)

    def src_copy(blk):
        return pltpu.make_async_copy(src_hbm.at[blk], src_smem.at[blk % 2], sem_src.at[blk % 2])

    tr, pitch = TOKEN_ROWS, MOE_PITCH

    def gather_row(slot, r):
        off = pl.multiple_of(src_smem[slot, r], tr)
        pltpu.make_async_copy(x_hbm.at[pl.ds(off, tr)], xbuf[slot].at[pl.ds(r * pitch, tr)],
                              sem_g.at[slot]).start(priority=r % 2)

    def issue_gathers(slot):
        for r in range(bm):
            gather_row(slot, r)

    def wait_gathers(slot):
        pltpu.make_async_copy(x_hbm.at[pl.ds(0, bm * tr)], xbuf[slot].at[pl.ds(0, bm * tr)],
                              sem_g.at[slot]).wait()

    def compute(slot, copies=()):
        copies = list(copies)
        per_piece = -(-len(copies) // MOE_COPY_PIECES)

        def drip():
            for issue in copies[:per_piece]:
                issue()
            del copies[:per_piece]

        nc = 2 * LANES
        for s in range(tr):
            xb_ref[:, s * LANES:(s + 1) * LANES] = (
                xbuf[slot][pl.ds(s, bm, stride=pitch), :].astype(BF16))
            drip()
        for c in range(D_EXPERT // nc):
            h1 = _dot(xb_ref[...], w1_ref[:, c * nc:(c + 1) * nc])
            drip()
            h3 = _dot(xb_ref[...], w3_ref[:, c * nc:(c + 1) * nc])
            hid_ref[:, c * nc:(c + 1) * nc] = (h1 / (1.0 + jnp.exp(-h1)) * h3).astype(BF16)
            drip()
        for c in range(D_MODEL // nc):
            y_ref[:, c * nc:(c + 1) * nc] = _dot(hid_ref[...], w2_ref[:, c * nc:(c + 1) * nc])
            drip()
        assert not copies

    @pl.when(b == 0)
    def _():
        src_copy(0).start()
        src_copy(0).wait()

        @pl.when(nv_cur > 0)
        def _():
            issue_gathers(0)
        if nblk > 1:
            src_copy(1).start()

    @pl.when(b + 1 < nblk)
    def _():
        src_copy(b + 1).wait()

    @pl.when(b + 2 < nblk)
    def _():
        src_copy(b + 2).start()

    for p in (0, 1):
        q = 1 - p
        mine = cur == p

        @pl.when(mine & (nv_cur > 0) & (nv_next > 0))
        def _():
            wait_gathers(p)
            compute(p, [functools.partial(gather_row, q, r) for r in range(bm)])

        @pl.when(mine & (nv_cur > 0) & (nv_next == 0))
        def _():
            wait_gathers(p)
            compute(p)

    @pl.when(nv_cur == 0)
    def _():
        y_ref[...] = jnp.zeros_like(y_ref)


def _moe_dispatch(route, bm):
    t = route.shape[0]
    m = 2 * t
    nblk = m // bm + N_EXPERTS
    eid = route[:, 0:2].astype(jnp.int32).T.reshape(m)
    counts = jnp.sum((eid[:, None] == jnp.arange(N_EXPERTS, dtype=jnp.int32)[None, :]).astype(jnp.int32),
                     axis=0)
    order = jnp.argsort(eid).astype(jnp.int32)
    rank = jnp.argsort(order).astype(jnp.int32)
    start = jnp.cumsum(counts) - counts
    nb_e = (counts + bm - 1) // bm
    bend = jnp.cumsum(nb_e)
    bstart = bend - nb_e
    blk = jnp.arange(nblk, dtype=jnp.int32)
    blk_e = jnp.minimum(jnp.sum((blk[:, None] >= bend[None, :]).astype(jnp.int32), axis=1),
                        N_EXPERTS - 1).astype(jnp.int32)
    within = blk - bstart[blk_e]
    blk_nv = jnp.where(blk < bend[-1], jnp.clip(counts[blk_e] - within * bm, 0, bm), 0).astype(jnp.int32)
    row = jnp.arange(bm, dtype=jnp.int32)[None, :]
    sorted_pos = (start[blk_e] + within * bm)[:, None] + row
    asg = order[jnp.clip(sorted_pos, 0, m - 1)]
    valid = row < blk_nv[:, None]
    row_src = jnp.where(valid, jnp.where(asg >= t, asg - t, asg), 0).astype(jnp.int32)
    row_of_asg = (bstart[eid] * bm + rank - start[eid]).astype(jnp.int32)
    return blk_e, blk_nv, row_src * TOKEN_ROWS, row_of_asg, nblk


def _moe(x2_rows, route, w1, w3, w2):
    bm = MOE_ROWS
    blk_e, blk_nv, row_src, row_of_asg, nblk = _moe_dispatch(route, bm)
    grid_spec = pltpu.PrefetchScalarGridSpec(
        num_scalar_prefetch=2,
        grid=(nblk,),
        in_specs=[pl.BlockSpec(memory_space=pl.ANY),
                  pl.BlockSpec(memory_space=pl.ANY),
                  pl.BlockSpec((None, D_MODEL, D_EXPERT), lambda b, be, nv: (be[b], 0, 0)),
                  pl.BlockSpec((None, D_MODEL, D_EXPERT), lambda b, be, nv: (be[b], 0, 0)),
                  pl.BlockSpec((None, D_EXPERT, D_MODEL), lambda b, be, nv: (be[b], 0, 0))],
        out_specs=pl.BlockSpec((bm, D_MODEL), lambda b, be, nv: (b, 0)),
        scratch_shapes=[pltpu.SMEM((2, bm), jnp.int32),
                        pltpu.VMEM((bm * MOE_PITCH, LANES), F32),
                        pltpu.VMEM((bm * MOE_PITCH, LANES), F32),
                        pltpu.VMEM((bm, D_MODEL), BF16),
                        pltpu.VMEM((bm, D_EXPERT), BF16),
                        pltpu.SemaphoreType.DMA((2,)),
                        pltpu.SemaphoreType.DMA((2,))],
    )
    y_sorted = pl.pallas_call(
        functools.partial(_moe_kernel, nblk=nblk, bm=bm),
        grid_spec=grid_spec,
        out_shape=jax.ShapeDtypeStruct((nblk * bm, D_MODEL), F32),
        compiler_params=pltpu.CompilerParams(dimension_semantics=("arbitrary",),
                                             vmem_limit_bytes=VMEM_LIMIT,
                                             disable_bounds_checks=True),
        name="moe_experts",
    )(blk_e, blk_nv, row_src, x2_rows, w1, w3, w2)
    return y_sorted, row_of_asg


def _final_kernel(pos_ref, x_ref, y_hbm, route_ref, g_ref, b_ref, o_ref, ybuf, z_ref, sem,
                  *, alpha, tm, t, nt):
    i = pl.program_id(0)

    def issue(tile, slot):
        for k in range(2):
            for r in range(tm):
                p = pos_ref[k * t + tile * tm + r]
                pltpu.make_async_copy(y_hbm.at[pl.ds(p, 1)], ybuf.at[slot, pl.ds(k * tm + r, 1)],
                                      sem.at[slot]).start(priority=r % 2)

    @pl.when(i == 0)
    def _():
        issue(0, 0)

    @pl.when(i + 1 < nt)
    def _():
        issue(i + 1, (i + 1) % 2)

    slot = i % 2
    pltpu.make_async_copy(y_hbm.at[pl.ds(0, 2 * tm)], ybuf.at[slot], sem.at[slot]).wait()
    r = route_ref[...]
    g0 = jnp.broadcast_to(r[:, 2:3], (tm, LANES))
    g1 = jnp.broadcast_to(r[:, 3:4], (tm, LANES))
    for s in range(TOKEN_ROWS):
        cs = slice(s * LANES, (s + 1) * LANES)
        z_ref[:, cs] = (alpha * x_ref[pl.ds(s, tm, stride=TOKEN_ROWS), :]
                        + g0 * ybuf[slot, 0:tm, cs] + g1 * ybuf[slot, tm:2 * tm, cs])
    o_ref[...] = _layer_norm(z_ref[...], g_ref[...], b_ref[...])


def _final(x2_rows, y_sorted, row_of_asg, route, ln_g, ln_b, alpha):
    t = route.shape[0]
    tm = ROW_TILE
    nt = t // tm
    grid_spec = pltpu.PrefetchScalarGridSpec(
        num_scalar_prefetch=1,
        grid=(nt,),
        in_specs=[pl.BlockSpec((tm * TOKEN_ROWS, LANES), lambda i, pos: (i, 0)),
                  pl.BlockSpec(memory_space=pl.ANY),
                  pl.BlockSpec((tm, LANES), lambda i, pos: (i, 0)),
                  pl.BlockSpec(ln_g.shape, lambda i, pos: (0, 0)),
                  pl.BlockSpec(ln_b.shape, lambda i, pos: (0, 0))],
        out_specs=pl.BlockSpec((tm, D_MODEL), lambda i, pos: (i, 0)),
        scratch_shapes=[pltpu.VMEM((2, 2 * tm, D_MODEL), F32),
                        pltpu.VMEM((tm, D_MODEL), F32),
                        pltpu.SemaphoreType.DMA((2,))],
    )
    return pl.pallas_call(
        functools.partial(_final_kernel, alpha=alpha, tm=tm, t=t, nt=nt),
        grid_spec=grid_spec,
        out_shape=jax.ShapeDtypeStruct((t, D_MODEL), F32),
        compiler_params=pltpu.CompilerParams(dimension_semantics=("arbitrary",),
                                             vmem_limit_bytes=VMEM_LIMIT,
                                             disable_bounds_checks=True),
        name="moe_combine_ln3",
    )(row_of_asg, x2_rows, y_sorted, route, ln_g, ln_b)


def _rotary_tables(seq):
    half = HEAD_DIM // 2
    inv_freq = ROPE_THETA ** (-jnp.arange(half, dtype=F32) / half)
    ang = jnp.arange(seq, dtype=F32)[:, None] * inv_freq[None, :]
    cos, sin = jnp.cos(ang), jnp.sin(ang)
    return jnp.concatenate([cos, cos], axis=-1), jnp.concatenate([-sin, sin], axis=-1)


def _prepare_weights(l, w_in, ret_gn_g, w_out, ln1_g, ln1_b, w_mq, w_mkv, w_mo, ln2_g, ln2_b,
                     w_gr, b_gr, w_er, b_er, w1, w3, w2, ln3_g, ln3_b):
    row = lambda v: v[l].reshape(1, -1).astype(F32)
    pad = LANES - N_GROUPS - N_EXPERTS
    w_r = jnp.concatenate([w_gr[l], w_er[l], jnp.zeros((D_MODEL, pad), F32)], axis=1)
    wr_hi = w_r.astype(BF16)
    wr_lo = (w_r - wr_hi.astype(F32)).astype(BF16)
    b_r = jnp.concatenate([b_gr[l], b_er[l], jnp.zeros((pad,), F32)]).reshape(1, LANES)
    return dict(
        w_in=w_in[l].astype(BF16), gn_g=row(ret_gn_g), w_out=w_out[l].astype(BF16),
        ln1=(row(ln1_g), row(ln1_b)), w_mq=w_mq[l].astype(BF16), w_mkv=w_mkv[l].astype(BF16),
        w_mo=w_mo[l].astype(BF16), ln2=(row(ln2_g), row(ln2_b)),
        w_r=jnp.concatenate([wr_hi, wr_lo], axis=1), b_r=b_r,
        w1=w1[l].astype(BF16), w3=w3[l].astype(BF16), w2=w2[l].astype(BF16),
        ln3=(row(ln3_g), row(ln3_b)))


def _encoder_layer(x, mem, w, alpha, ret_tables):
    batch, seq, _ = x.shape
    t = batch * seq
    x2d = x.reshape(t, D_MODEL)
    cos, sin = _rotary_tables(seq)
    h, ha4, ha16 = _in_proj(x2d, w['w_in'], cos, sin, batch, seq)
    outs, lses = [], []
    for dil, qkv in zip(DILATIONS, (h.reshape(N_SEG, batch, 1, seq, SEG), ha4, ha16)):
        o, lse = _local_attn(qkv, batch, seq, dil)
        outs.append(o)
        lses.append(lse)
    ret_o = _retention(h.reshape(N_SEG, batch, seq, SEG), w['gn_g'], batch, seq, ret_tables)
    x1 = _out_proj(outs, lses, ret_o.reshape(t, SEG), x2d, w['w_out'], *w['ln1'], seq, alpha)
    kv = _mem_kv(mem.reshape(batch * MEM_TOKENS, D_MODEL), w['w_mkv'])
    x2_rows, route = _xattn(x1, kv, w['w_mq'], w['w_mo'], *w['ln2'], w['w_r'], w['b_r'], seq, alpha)
    y_sorted, row_of_asg = _moe(x2_rows, route, w['w1'], w['w3'], w['w2'])
    out = _final(x2_rows, y_sorted, row_of_asg, route, *w['ln3'], alpha)
    return out.reshape(batch, seq, D_MODEL)


def kernel(x_prompt, x_sample, mem_prompt, mem_sample, w_in, ret_gn_g, w_out, ln1_g, ln1_b,
           w_mq, w_mkv, w_mo, ln2_g, ln2_b, w_gr, b_gr, w_er, b_er, w1, w3, w2, ln3_g, ln3_b):
    depth = w_in.shape[0]
    alpha = (2 * depth) ** 0.25
    ret_tables = _retention_tables()
    y_prompt, y_sample = x_prompt, x_sample
    for l in range(depth):
        w = _prepare_weights(l, w_in, ret_gn_g, w_out, ln1_g, ln1_b, w_mq, w_mkv, w_mo, ln2_g,
                             ln2_b, w_gr, b_gr, w_er, b_er, w1, w3, w2, ln3_g, ln3_b)
        y_prompt = _encoder_layer(y_prompt, mem_prompt, w, alpha, ret_tables)
        y_sample = _encoder_layer(y_sample, mem_sample, w, alpha, ret_tables)
    return (y_prompt, y_sample)
```

```python
import functools

import numpy as np
import jax
import jax.numpy as jnp
from jax import lax
from jax.experimental import pallas as pl
from jax.experimental.pallas import tpu as pltpu

F32 = jnp.float32
BF16 = jnp.bfloat16

D_MODEL = 2048
HEAD_DIM = 128
N_HEADS = 8
SEG = N_HEADS * HEAD_DIM
N_SEG = 7
SEG_QA, SEG_KA, SEG_VA, SEG_QR, SEG_KR, SEG_VR, SEG_GR = range(7)
DILATIONS = (1, 4, 16)
HALF_WIN = 64
ROPE_THETA = 10000.0
RET_CHUNK = 128
MEM_TOKENS = 256
X_HEADS = 4
D_X = X_HEADS * HEAD_DIM
N_GROUPS = 4
EXP_PER_GROUP = 8
N_EXPERTS = N_GROUPS * EXP_PER_GROUP
D_EXPERT = D_MODEL // 4
LN_EPS = 1e-5
NEG_INF = -1e30
ATTN_SCALE = HEAD_DIM ** -0.5
LOG2_E = 1.4426950408889634
LN_2 = 0.6931471805599453

LANES = 128
VMEM_LIMIT = 52 * 1024 * 1024

IN_PROJ_ROWS = 1024
IN_PROJ_VMEM = 58 * 1024 * 1024
N_SLABS = 4
TOKEN_ROWS = D_MODEL // LANES
MOE_COPY_PIECES = 12
ATTN_ROWS = 512
RET_ROWS = 512
ROW_TILE = 256
XATTN_ROWS = 512
MOE_ROWS = 256


def _params(*sem):
    return pltpu.CompilerParams(dimension_semantics=sem, vmem_limit_bytes=VMEM_LIMIT)


def _layer_norm(z, g, b):
    mu = jnp.mean(z, axis=-1, keepdims=True)
    d = z - mu
    var = jnp.mean(d * d, axis=-1, keepdims=True)
    return d * lax.rsqrt(var + LN_EPS) * g + b


def _dot_nt(a, b):
    return lax.dot_general(a, b, (((1,), (1,)), ((), ())), preferred_element_type=F32)


def _dot_tn(a, b):
    return lax.dot_general(a, b, (((0,), (0,)), ((), ())), preferred_element_type=F32)


def _dot(a, b):
    return jnp.dot(a, b, preferred_element_type=F32)


def _in_proj_kernel(x_ref, w_ref, cos_ref, sin_ref, o_ref, o4_ref, o16_ref, xb_ref, slab, slab4,
                    *, tm):
    j = pl.program_id(1)

    @pl.when(j == 0)
    def _():
        xb_ref[...] = x_ref[...].astype(BF16)

    def segment(rot, strided):
        if rot:
            scale = jnp.where(j == SEG_KR, ATTN_SCALE,
                              jnp.where(j == SEG_QA, ATTN_SCALE * LOG2_E, 1.0)).astype(F32)
            c = cos_ref[...] * scale
            s = sin_ref[...] * scale
        for pair in range(N_HEADS // 2):
            acc = _dot(xb_ref[...], w_ref[:, pair * 2 * HEAD_DIM:(pair + 1) * 2 * HEAD_DIM])
            for hh in range(2):
                h = 2 * pair + hh
                cs = slice(h * HEAD_DIM, (h + 1) * HEAD_DIM)
                t = acc[:, hh * HEAD_DIM:(hh + 1) * HEAD_DIM]
                if rot:
                    t = t * c + pltpu.roll(t, HEAD_DIM // 2, 1) * s
                o_ref[:, cs] = t.astype(BF16)
                if strided:
                    k = h % N_SLABS
                    slab[k] = t
                    for r in range(4):
                        v4 = slab[k, pl.ds(r, tm // 4, stride=4), :]
                        o4_ref[r, :, cs] = v4.astype(BF16)
                        slab4[k, r] = v4
                        for a in range(4):
                            o16_ref[r + 4 * a, :, cs] = (
                                slab4[k, r, pl.ds(a, tm // 16, stride=4), :].astype(BF16))

    is_rot = (j == SEG_QA) | (j == SEG_KA) | (j == SEG_QR) | (j == SEG_KR)
    is_attn = j <= SEG_VA
    for rot in (True, False):
        for strided in (True, False):
            cond = (is_rot if rot else jnp.logical_not(is_rot)) & (
                is_attn if strided else jnp.logical_not(is_attn))
            pl.when(cond)(functools.partial(segment, rot, strided))


def _in_proj(x2d, w_in, cos, sin, batch, seq):
    t = x2d.shape[0]
    tm = IN_PROJ_ROWS
    n_s = seq // tm
    n_att = SEG_VA + 1

    def strided_out(dil):
        shape = jax.ShapeDtypeStruct((n_att, batch, dil, seq // dil, SEG), BF16)
        spec = pl.BlockSpec((None, None, dil, tm // dil, SEG),
                            lambda i, j: (jnp.minimum(j, SEG_VA), i // n_s, 0, i % n_s, 0))
        return shape, spec

    (shape4, spec4), (shape16, spec16) = strided_out(4), strided_out(16)
    return pl.pallas_call(
        functools.partial(_in_proj_kernel, tm=tm),
        grid=(t // tm, N_SEG),
        in_specs=[
            pl.BlockSpec((tm, D_MODEL), lambda i, j: (i, 0)),
            pl.BlockSpec((D_MODEL, SEG), lambda i, j: (0, j)),
            pl.BlockSpec((tm, HEAD_DIM), lambda i, j: (i % n_s, 0)),
            pl.BlockSpec((tm, HEAD_DIM), lambda i, j: (i % n_s, 0)),
        ],
        out_specs=[pl.BlockSpec((None, tm, SEG), lambda i, j: (j, i, 0)), spec4, spec16],
        out_shape=[jax.ShapeDtypeStruct((N_SEG, t, SEG), BF16), shape4, shape16],
        scratch_shapes=[pltpu.VMEM((tm, D_MODEL), BF16),
                        pltpu.VMEM((N_SLABS, tm, HEAD_DIM), F32),
                        pltpu.VMEM((N_SLABS, 4, tm // 4, HEAD_DIM), F32)],
        compiler_params=pltpu.CompilerParams(dimension_semantics=("arbitrary", "arbitrary"),
                                             vmem_limit_bytes=IN_PROJ_VMEM),
        name="in_proj",
    )(x2d, w_in, cos, sin)


def _local_attn_kernel(q_ref, kp_ref, kc_ref, kn_ref, vp_ref, vc_ref, vn_ref,
                       o_ref, lse_ref, kbuf, vbuf, *, lt, sub_len):
    i = pl.program_id(2)
    hw = HALF_WIN
    kbuf[0:hw, :] = kp_ref[...]
    kbuf[hw:hw + lt, :] = kc_ref[...]
    kbuf[hw + lt:2 * hw + lt, :] = kn_ref[...]
    vbuf[0:hw, :] = vp_ref[...]
    vbuf[hw:hw + lt, :] = vc_ref[...]
    vbuf[hw + lt:2 * hw + lt, :] = vn_ref[...]

    qb = 128
    kb = qb + 2 * hw
    row = lax.broadcasted_iota(jnp.int32, (qb, kb), 0)
    col = lax.broadcasted_iota(jnp.int32, (qb, kb), 1)
    band = jnp.abs(row + hw - col) <= hw
    lane = lax.broadcasted_iota(jnp.int32, (qb, LANES), 1)

    def body(j, carry):
        r0 = pl.multiple_of(j * qb, qb)
        kpos = i * lt + r0 - hw + col
        bias = jnp.where(band, 0.0, NEG_INF).astype(F32)
        bias = jnp.where(kpos >= 0, bias, NEG_INF)
        bias = jnp.where(kpos < sub_len, bias, NEG_INF)
        lse_tile = jnp.zeros((qb, LANES), F32)
        for h in range(N_HEADS):
            cs = slice(h * HEAD_DIM, (h + 1) * HEAD_DIM)
            q = q_ref[pl.ds(r0, qb), cs]
            k = kbuf[pl.ds(r0, kb), cs]
            v = vbuf[pl.ds(r0, kb), cs]
            s = _dot_nt(q, k) + bias
            m = jnp.max(s, axis=-1, keepdims=True)
            p = jnp.exp2(s - m)
            l = jnp.sum(p, axis=-1, keepdims=True)
            o = _dot(p.astype(BF16), v) / l
            o_ref[pl.ds(r0, qb), cs] = o.astype(BF16)
            lse_tile = jnp.where(lane == h, (m + jnp.log2(l)) * LN_2, lse_tile)
        lse_ref[pl.ds(r0, qb), :] = lse_tile
        return carry

    lax.fori_loop(0, lt // qb, body, 0)


def _local_attn(qkv, batch, seq, dil):
    sub_len = seq // dil
    lt = min(sub_len, ATTN_ROWS)
    hw = HALF_WIN
    n_halo = sub_len // hw
    per = lt // hw

    def main(seg):
        return pl.BlockSpec((None, None, None, lt, SEG), lambda b, r, i: (seg, b, r, i, 0))

    def prev(seg):
        return pl.BlockSpec((None, None, None, hw, SEG),
                            lambda b, r, i: (seg, b, r, jnp.maximum(i * per - 1, 0), 0))

    def nxt(seg):
        return pl.BlockSpec((None, None, None, hw, SEG),
                            lambda b, r, i: (seg, b, r, jnp.minimum((i + 1) * per, n_halo - 1), 0))

    kern = functools.partial(_local_attn_kernel, lt=lt, sub_len=sub_len)
    return pl.pallas_call(
        kern,
        grid=(batch, dil, sub_len // lt),
        in_specs=[main(SEG_QA), prev(SEG_KA), main(SEG_KA), nxt(SEG_KA),
                  prev(SEG_VA), main(SEG_VA), nxt(SEG_VA)],
        out_specs=[pl.BlockSpec((None, None, lt, SEG), lambda b, r, i: (b, r, i, 0)),
                   pl.BlockSpec((None, None, lt, LANES), lambda b, r, i: (b, r, i, 0))],
        out_shape=[jax.ShapeDtypeStruct((batch, dil, sub_len, SEG), BF16),
                   jax.ShapeDtypeStruct((batch, dil, sub_len, LANES), F32)],
        scratch_shapes=[pltpu.VMEM((lt + 2 * hw, SEG), BF16),
                        pltpu.VMEM((lt + 2 * hw, SEG), BF16)],
        compiler_params=_params("arbitrary", "arbitrary", "arbitrary"),
        name="local_attn_d%d" % dil,
    )(qkv, qkv, qkv, qkv, qkv, qkv, qkv)


def _retention_tables():
    h = np.arange(N_HEADS, dtype=np.float64)
    gf = 1.0 - 2.0 ** (-5.0 - h)
    gb = 1.0 - 2.0 ** (-5.5 - h)
    c = RET_CHUNK
    idx = np.arange(c, dtype=np.float64)
    diff = idx[:, None] - idx[None, :]
    dec_f = np.where(diff >= 0, gf[:, None, None] ** np.maximum(diff, 0.0), 0.0)
    dec_b = np.where(diff < 0, gb[:, None, None] ** np.maximum(-diff, 0.0), 0.0)
    decay = dec_f + dec_b
    rows = lambda t: np.broadcast_to(t[:, :, None], (N_HEADS, c, HEAD_DIM))
    xi_f = rows(gf[:, None] ** (idx + 1.0)[None])
    zeta_f = rows(gf[:, None] ** (c - 1.0 - idx)[None])
    xi_b = rows(gb[:, None] ** (c - idx)[None])
    zeta_b = rows(gb[:, None] ** idx[None])
    f = lambda t: jnp.asarray(np.ascontiguousarray(t), F32)
    return (f(decay), f(xi_f), f(zeta_f), f(xi_b), f(zeta_b),
            tuple(float(g ** c) for g in gf), tuple(float(g ** c) for g in gb))


def _ret_fwd_kernel(q_ref, k_ref, v_ref, dec_ref, xi_ref, zeta_ref, o_ref, state, *, rt, cd):
    @pl.when(pl.program_id(1) == 0)
    def _():
        state[...] = jnp.zeros_like(state)

    for c in range(rt // RET_CHUNK):
        rs = slice(c * RET_CHUNK, (c + 1) * RET_CHUNK)
        for h in range(N_HEADS):
            cs = slice(h * HEAD_DIM, (h + 1) * HEAD_DIM)
            q = q_ref[rs, cs]
            k = k_ref[rs, cs]
            v = v_ref[rs, cs]
            a = _dot_nt(q, k) * dec_ref[h]
            o = _dot(a.astype(BF16), v)
            s_old = state[h]
            o = o + _dot(q, s_old.astype(BF16)) * xi_ref[h]
            kz = (k.astype(F32) * zeta_ref[h]).astype(BF16)
            state[h] = cd[h] * s_old + _dot_tn(kz, v)
            o_ref[rs, cs] = o


def _ret_bwd_kernel(q_ref, k_ref, v_ref, g_ref, r_ref, xi_ref, zeta_ref, gn_ref, o_ref, state,
                    *, rt, cd):
    @pl.when(pl.program_id(1) == 0)
    def _():
        state[...] = jnp.zeros_like(state)

    for c in reversed(range(rt // RET_CHUNK)):
        rs = slice(c * RET_CHUNK, (c + 1) * RET_CHUNK)
        for h in range(N_HEADS):
            cs = slice(h * HEAD_DIM, (h + 1) * HEAD_DIM)
            q = q_ref[rs, cs]
            k = k_ref[rs, cs]
            v = v_ref[rs, cs]
            s_old = state[h]
            r = r_ref[rs, cs] + _dot(q, s_old.astype(BF16)) * xi_ref[h]
            kz = (k.astype(F32) * zeta_ref[h]).astype(BF16)
            state[h] = cd[h] * s_old + _dot_tn(kz, v)
            mu = jnp.mean(r, axis=-1, keepdims=True)
            d = r - mu
            var = jnp.mean(d * d, axis=-1, keepdims=True)
            rn = d * lax.rsqrt(var + LN_EPS) * gn_ref[:, cs]
            g = g_ref[rs, cs].astype(F32)
            silu = g / (1.0 + jnp.exp(-g))
            o_ref[rs, cs] = (silu * rn).astype(BF16)


def _retention(h3, gn_g, batch, seq, tables):
    decay, xi_f, zeta_f, xi_b, zeta_b, cd_f, cd_b = tables
    rt = min(seq, RET_ROWS)
    nr = seq // rt
    tab = pl.BlockSpec((N_HEADS, RET_CHUNK, HEAD_DIM), lambda b, t: (0, 0, 0))
    state = pltpu.VMEM((N_HEADS, HEAD_DIM, HEAD_DIM), F32)

    fwd_seg = lambda seg: pl.BlockSpec((None, None, rt, SEG), lambda b, t: (seg, b, t, 0))
    r_fwd = pl.pallas_call(
        functools.partial(_ret_fwd_kernel, rt=rt, cd=cd_f),
        grid=(batch, nr),
        in_specs=[fwd_seg(SEG_QR), fwd_seg(SEG_KR), fwd_seg(SEG_VR), tab, tab, tab],
        out_specs=pl.BlockSpec((None, rt, SEG), lambda b, t: (b, t, 0)),
        out_shape=jax.ShapeDtypeStruct((batch, seq, SEG), F32),
        scratch_shapes=[state],
        compiler_params=_params("arbitrary", "arbitrary"),
        name="retention_fwd",
    )(h3, h3, h3, decay, xi_f, zeta_f)

    bwd_seg = lambda seg: pl.BlockSpec((None, None, rt, SEG),
                                       lambda b, t: (seg, b, nr - 1 - t, 0))
    return pl.pallas_call(
        functools.partial(_ret_bwd_kernel, rt=rt, cd=cd_b),
        grid=(batch, nr),
        in_specs=[bwd_seg(SEG_QR), bwd_seg(SEG_KR), bwd_seg(SEG_VR), bwd_seg(SEG_GR),
                  pl.BlockSpec((None, rt, SEG), lambda b, t: (b, nr - 1 - t, 0)),
                  tab, tab, pl.BlockSpec((1, SEG), lambda b, t: (0, 0))],
        out_specs=pl.BlockSpec((None, rt, SEG), lambda b, t: (b, nr - 1 - t, 0)),
        out_shape=jax.ShapeDtypeStruct((batch, seq, SEG), BF16),
        scratch_shapes=[state],
        compiler_params=_params("arbitrary", "arbitrary"),
        name="retention_bwd",
    )(h3, h3, h3, h3, r_fwd, xi_b, zeta_b, gn_g)


def _out_proj_kernel(o1_ref, o2_ref, o3_ref, l1_ref, l2_ref, l3_ref, ret_ref, x_ref, w_ref,
                     g_ref, b_ref, out_ref, attn_buf, nat_o, nat_l, *, alpha):
    rows = x_ref.shape[0]
    heads = [slice(h * HEAD_DIM, (h + 1) * HEAD_DIM) for h in range(N_HEADS)]
    for k, (dil, o_ref, l_ref) in enumerate(((4, o2_ref, l2_ref), (16, o3_ref, l3_ref))):
        for r in range(dil):
            dst = pl.ds(r, rows // dil, stride=dil)
            nat_l[k, dst, :] = l_ref[r]
            for h, cs in enumerate(heads):
                nat_o[k, h, dst, :] = o_ref[r, :, cs].astype(F32)
    la, lb, lc = l1_ref[...], nat_l[0], nat_l[1]
    m = jnp.maximum(jnp.maximum(la, lb), lc)
    ea, eb, ec = jnp.exp(la - m), jnp.exp(lb - m), jnp.exp(lc - m)
    inv = 1.0 / (ea + eb + ec)
    wa, wb, wc = ea * inv, eb * inv, ec * inv
    for h, cs in enumerate(heads):
        bc = lambda w: jnp.broadcast_to(w[:, h:h + 1], (rows, HEAD_DIM))
        mix = (bc(wa) * o1_ref[:, cs].astype(F32) + bc(wb) * nat_o[0, h] + bc(wc) * nat_o[1, h])
        attn_buf[:, cs] = mix.astype(BF16)
    y = _dot(attn_buf[...], w_ref[0:SEG, :]) + _dot(ret_ref[...], w_ref[SEG:2 * SEG, :])
    out_ref[...] = _layer_norm(alpha * x_ref[...] + y, g_ref[...], b_ref[...])


def _out_proj(outs, lses, ret_o, x2d, w_out, ln_g, ln_b, seq, alpha):
    t = x2d.shape[0]
    tm = ROW_TILE
    per_b = seq // tm
    row = lambda w: pl.BlockSpec((tm, w), lambda i: (i, 0))
    full = lambda a: pl.BlockSpec(a.shape, lambda i: (0,) * a.ndim)

    def branch(dil, width):
        return pl.BlockSpec((None, dil, tm // dil, width),
                            lambda i: (i // per_b, 0, i % per_b, 0))

    in_specs = ([row(SEG), branch(4, SEG), branch(16, SEG),
                 row(LANES), branch(4, LANES), branch(16, LANES)]
                + [row(SEG), row(D_MODEL), full(w_out), full(ln_g), full(ln_b)])
    return pl.pallas_call(
        functools.partial(_out_proj_kernel, alpha=alpha),
        grid=(t // tm,),
        in_specs=in_specs,
        out_specs=row(D_MODEL),
        out_shape=jax.ShapeDtypeStruct((t, D_MODEL), F32),
        scratch_shapes=[pltpu.VMEM((tm, SEG), BF16),
                        pltpu.VMEM((2, N_HEADS, tm, HEAD_DIM), F32),
                        pltpu.VMEM((2, tm, LANES), F32)],
        compiler_params=_params("arbitrary"),
        name="out_proj_ln1",
    )(outs[0].reshape(t, SEG), outs[1], outs[2], lses[0].reshape(t, LANES), lses[1], lses[2],
      ret_o, x2d, w_out, ln_g, ln_b)


def _mem_kv_kernel(m_ref, w_ref, o_ref):
    o_ref[...] = _dot(m_ref[...].astype(BF16), w_ref[...]).astype(BF16)


def _mem_kv(mem2d, w_mkv):
    rows = mem2d.shape[0]
    tm = ROW_TILE
    return pl.pallas_call(
        _mem_kv_kernel,
        grid=(rows // tm,),
        in_specs=[pl.BlockSpec((tm, D_MODEL), lambda i: (i, 0)),
                  pl.BlockSpec(w_mkv.shape, lambda i: (0, 0))],
        out_specs=pl.BlockSpec((tm, 2 * D_X), lambda i: (i, 0)),
        out_shape=jax.ShapeDtypeStruct((rows, 2 * D_X), BF16),
        compiler_params=_params("arbitrary"),
        name="mem_kv",
    )(mem2d, w_mkv)


def _route(logits):
    shape = logits.shape
    lane_i = lax.broadcasted_iota(jnp.int32, shape, 1)
    lane = lane_i.astype(F32)
    lowest = jnp.float32(-3.0e38)
    none = jnp.float32(LANES)
    rmax = lambda t: jnp.max(t, axis=-1, keepdims=True)
    rmin = lambda t: jnp.min(t, axis=-1, keepdims=True)

    is_group = lane_i < N_GROUPS
    gl = jnp.where(is_group, logits, lowest)
    g_max = rmax(gl)
    g_idx = rmin(jnp.where(gl == g_max, lane, none))
    p_g = 1.0 / jnp.sum(jnp.where(is_group, jnp.exp(logits - g_max), 0.0), axis=-1, keepdims=True)

    e_lo = N_GROUPS + EXP_PER_GROUP * g_idx
    el = jnp.where(lane >= e_lo, logits, lowest)
    el = jnp.where(lane < e_lo + EXP_PER_GROUP, el, lowest)
    v1 = rmax(el)
    i1 = rmin(jnp.where(el == v1, lane, none))
    el2 = jnp.where(lane == i1, lowest, el)
    v2 = rmax(el2)
    i2 = rmin(jnp.where(el2 == v2, lane, none))
    e2 = jnp.exp(v2 - v1)
    g1 = p_g / (1.0 + e2)
    g2 = p_g * e2 / (1.0 + e2)
    out = jnp.where(lane_i == 0, i1 - N_GROUPS,
                    jnp.where(lane_i == 1, i2 - N_GROUPS,
                              jnp.where(lane_i == 2, g1, jnp.where(lane_i == 3, g2, 0.0))))
    return out.astype(F32)


def _xattn_kernel(x_ref, kv_ref, wq_ref, wo_ref, g_ref, b_ref, wr_ref, br_ref,
                  x2_ref, route_ref, obuf, *, alpha):
    sub = ROW_TILE
    for part in range(x_ref.shape[0] // sub):
        rs = slice(part * sub, (part + 1) * sub)
        x = x_ref[rs, :]
        q = _dot(x.astype(BF16), wq_ref[...]).astype(BF16)
        for h in range(X_HEADS):
            cs = slice(h * HEAD_DIM, (h + 1) * HEAD_DIM)
            k = kv_ref[:, cs]
            v = kv_ref[:, D_X + h * HEAD_DIM:D_X + (h + 1) * HEAD_DIM]
            s = _dot_nt(q[:, cs], k) * ATTN_SCALE
            m = jnp.max(s, axis=-1, keepdims=True)
            p = jnp.exp(s - m)
            l = jnp.sum(p, axis=-1, keepdims=True)
            obuf[rs, cs] = (_dot(p.astype(BF16), v) / l).astype(BF16)
        y = _dot(obuf[rs, :], wo_ref[...])
        x2 = _layer_norm(alpha * x + y, g_ref[...], b_ref[...])
        x2_ref[rs, :] = x2
        xh = x2.astype(BF16)
        xl = (x2 - xh.astype(F32)).astype(BF16)
        hw = _dot(xh, wr_ref[...])
        logits = hw[:, :LANES] + hw[:, LANES:] + _dot(xl, wr_ref[:, :LANES]) + br_ref[...]
        route_ref[rs, :] = _route(logits)


def _xattn(x1, kv, w_mq, w_mo, ln_g, ln_b, w_r, b_r, seq, alpha):
    t = x1.shape[0]
    tm = XATTN_ROWS
    per_b = seq // tm
    row = lambda w: pl.BlockSpec((tm, w), lambda i: (i, 0))
    full = lambda a: pl.BlockSpec(a.shape, lambda i: (0,) * a.ndim)
    return pl.pallas_call(
        functools.partial(_xattn_kernel, alpha=alpha),
        grid=(t // tm,),
        in_specs=[row(D_MODEL),
                  pl.BlockSpec((MEM_TOKENS, 2 * D_X), lambda i: (i // per_b, 0)),
                  full(w_mq), full(w_mo), full(ln_g), full(ln_b),
                  full(w_r), full(b_r)],
        out_specs=[row(D_MODEL), row(LANES)],
        out_shape=[jax.ShapeDtypeStruct((t, D_MODEL), F32),
                   jax.ShapeDtypeStruct((t, LANES), F32)],
        scratch_shapes=[pltpu.VMEM((tm, D_X), BF16)],
        compiler_params=_params("arbitrary"),
        name="xattn_ln2_router",
    )(x1, kv, w_mq, w_mo, ln_g, ln_b, w_r, b_r)


def _moe_kernel(blk_e_ref, blk_nv_ref, src_hbm, x_hbm, w1_ref, w3_ref, w2_ref, y_ref,
                src_smem, xbuf0, xbuf1, xb_ref, hid_ref, sem_src, sem_g, *, nblk, bm):
    del blk_e_ref
    b = pl.program_id(0)
    cur = b % 2
    xbuf = (xbuf0, xbuf1)

    def rows_of(blk):
        return jnp.where(blk < nblk, blk_nv_ref[jnp.minimum(blk, nblk - 1)], 0)

    nv_cur, nv_next = rows_of(b), rows_of(b + 1)

    def src_copy(blk):
        return pltpu.make_async_copy(src_hbm.at[blk], src_smem.at[blk % 2], sem_src.at[blk % 2])

    def gather_row(slot, r):
        pltpu.make_async_copy(x_hbm.at[pl.ds(src_smem[slot, r], 1)], xbuf[slot].at[pl.ds(r, 1)],
                              sem_g.at[slot]).start(priority=r % 2)

    def issue_gathers(slot):
        for r in range(bm):
            gather_row(slot, r)

    def wait_gathers(slot):
        pltpu.make_async_copy(x_hbm.at[pl.ds(0, bm)], xbuf[slot], sem_g.at[slot]).wait()

    def compute(slot, copies=()):
        copies = list(copies)
        per_piece = -(-len(copies) // MOE_COPY_PIECES)

        def drip():
            for issue in copies[:per_piece]:
                issue()
            del copies[:per_piece]

        nc = 2 * LANES
        for s in range(TOKEN_ROWS):
            cs = slice(s * LANES, (s + 1) * LANES)
            xb_ref[:, cs] = xbuf[slot][:, cs].astype(BF16)
            drip()
        for c in range(D_EXPERT // nc):
            h1 = _dot(xb_ref[...], w1_ref[:, c * nc:(c + 1) * nc])
            drip()
            h3 = _dot(xb_ref[...], w3_ref[:, c * nc:(c + 1) * nc])
            hid_ref[:, c * nc:(c + 1) * nc] = (h1 / (1.0 + jnp.exp(-h1)) * h3).astype(BF16)
            drip()
        for c in range(D_MODEL // nc):
            y_ref[:, c * nc:(c + 1) * nc] = _dot(hid_ref[...], w2_ref[:, c * nc:(c + 1) * nc])
            drip()
        assert not copies

    @pl.when(b == 0)
    def _():
        src_copy(0).start()
        src_copy(0).wait()

        @pl.when(nv_cur > 0)
        def _():
            issue_gathers(0)
        if nblk > 1:
            src_copy(1).start()

    @pl.when(b + 1 < nblk)
    def _():
        src_copy(b + 1).wait()

    @pl.when(b + 2 < nblk)
    def _():
        src_copy(b + 2).start()

    for p in (0, 1):
        q = 1 - p
        mine = cur == p

        @pl.when(mine & (nv_cur > 0) & (nv_next > 0))
        def _():
            wait_gathers(p)
            compute(p, [functools.partial(gather_row, q, r) for r in range(bm)])

        @pl.when(mine & (nv_cur > 0) & (nv_next == 0))
        def _():
            wait_gathers(p)
            compute(p)

    @pl.when(nv_cur == 0)
    def _():
        y_ref[...] = jnp.zeros_like(y_ref)


def _moe_dispatch(route, bm):
    t = route.shape[0]
    m = 2 * t
    nblk = m // bm + N_EXPERTS
    eid = route[:, 0:2].astype(jnp.int32).T.reshape(m)
    counts = jnp.sum((eid[:, None] == jnp.arange(N_EXPERTS, dtype=jnp.int32)[None, :]).astype(jnp.int32),
                     axis=0)
    order = jnp.argsort(eid).astype(jnp.int32)
    rank = jnp.argsort(order).astype(jnp.int32)
    start = jnp.cumsum(counts) - counts
    nb_e = (counts + bm - 1) // bm
    bend = jnp.cumsum(nb_e)
    bstart = bend - nb_e
    blk = jnp.arange(nblk, dtype=jnp.int32)
    blk_e = jnp.minimum(jnp.sum((blk[:, None] >= bend[None, :]).astype(jnp.int32), axis=1),
                        N_EXPERTS - 1).astype(jnp.int32)
    within = blk - bstart[blk_e]
    blk_nv = jnp.where(blk < bend[-1], jnp.clip(counts[blk_e] - within * bm, 0, bm), 0).astype(jnp.int32)
    row = jnp.arange(bm, dtype=jnp.int32)[None, :]
    sorted_pos = (start[blk_e] + within * bm)[:, None] + row
    asg = order[jnp.clip(sorted_pos, 0, m - 1)]
    valid = row < blk_nv[:, None]
    row_src = jnp.where(valid, jnp.where(asg >= t, asg - t, asg), 0).astype(jnp.int32)
    row_of_asg = (bstart[eid] * bm + rank - start[eid]).astype(jnp.int32)
    return blk_e, blk_nv, row_src, row_of_asg, nblk


def _moe(x2, route, w1, w3, w2):
    bm = MOE_ROWS
    blk_e, blk_nv, row_src, row_of_asg, nblk = _moe_dispatch(route, bm)
    grid_spec = pltpu.PrefetchScalarGridSpec(
        num_scalar_prefetch=2,
        grid=(nblk,),
        in_specs=[pl.BlockSpec(memory_space=pl.ANY),
                  pl.BlockSpec(memory_space=pl.ANY),
                  pl.BlockSpec((None, D_MODEL, D_EXPERT), lambda b, be, nv: (be[b], 0, 0)),
                  pl.BlockSpec((None, D_MODEL, D_EXPERT), lambda b, be, nv: (be[b], 0, 0)),
                  pl.BlockSpec((None, D_EXPERT, D_MODEL), lambda b, be, nv: (be[b], 0, 0))],
        out_specs=pl.BlockSpec((bm, D_MODEL), lambda b, be, nv: (b, 0)),
        scratch_shapes=[pltpu.SMEM((2, bm), jnp.int32),
                        pltpu.VMEM((bm, D_MODEL), F32),
                        pltpu.VMEM((bm, D_MODEL), F32),
                        pltpu.VMEM((bm, D_MODEL), BF16),
                        pltpu.VMEM((bm, D_EXPERT), BF16),
                        pltpu.SemaphoreType.DMA((2,)),
                        pltpu.SemaphoreType.DMA((2,))],
    )
    y_sorted = pl.pallas_call(
        functools.partial(_moe_kernel, nblk=nblk, bm=bm),
        grid_spec=grid_spec,
        out_shape=jax.ShapeDtypeStruct((nblk * bm, D_MODEL), F32),
        compiler_params=pltpu.CompilerParams(dimension_semantics=("arbitrary",),
                                             vmem_limit_bytes=VMEM_LIMIT,
                                             disable_bounds_checks=True),
        name="moe_experts",
    )(blk_e, blk_nv, row_src, x2, w1, w3, w2)
    return y_sorted, row_of_asg


def _final_kernel(pos_ref, x_ref, y_hbm, route_ref, g_ref, b_ref, o_ref, ybuf, z_ref, sem,
                  *, alpha, tm, t, nt):
    i = pl.program_id(0)

    def issue(tile, slot):
        for k in range(2):
            for r in range(tm):
                p = pos_ref[k * t + tile * tm + r]
                pltpu.make_async_copy(y_hbm.at[pl.ds(p, 1)], ybuf.at[slot, pl.ds(k * tm + r, 1)],
                                      sem.at[slot]).start(priority=r % 2)

    @pl.when(i == 0)
    def _():
        issue(0, 0)

    @pl.when(i + 1 < nt)
    def _():
        issue(i + 1, (i + 1) % 2)

    slot = i % 2
    pltpu.make_async_copy(y_hbm.at[pl.ds(0, 2 * tm)], ybuf.at[slot], sem.at[slot]).wait()
    r = route_ref[...]
    g0 = jnp.broadcast_to(r[:, 2:3], (tm, LANES))
    g1 = jnp.broadcast_to(r[:, 3:4], (tm, LANES))
    for s in range(TOKEN_ROWS):
        cs = slice(s * LANES, (s + 1) * LANES)
        z_ref[:, cs] = (alpha * x_ref[:, cs]
                        + g0 * ybuf[slot, 0:tm, cs] + g1 * ybuf[slot, tm:2 * tm, cs])
    o_ref[...] = _layer_norm(z_ref[...], g_ref[...], b_ref[...])


def _final(x2, y_sorted, row_of_asg, route, ln_g, ln_b, alpha):
    t = route.shape[0]
    tm = ROW_TILE
    nt = t // tm
    grid_spec = pltpu.PrefetchScalarGridSpec(
        num_scalar_prefetch=1,
        grid=(nt,),
        in_specs=[pl.BlockSpec((tm, D_MODEL), lambda i, pos: (i, 0)),
                  pl.BlockSpec(memory_space=pl.ANY),
                  pl.BlockSpec((tm, LANES), lambda i, pos: (i, 0)),
                  pl.BlockSpec(ln_g.shape, lambda i, pos: (0, 0)),
                  pl.BlockSpec(ln_b.shape, lambda i, pos: (0, 0))],
        out_specs=pl.BlockSpec((tm, D_MODEL), lambda i, pos: (i, 0)),
        scratch_shapes=[pltpu.VMEM((2, 2 * tm, D_MODEL), F32),
                        pltpu.VMEM((tm, D_MODEL), F32),
                        pltpu.SemaphoreType.DMA((2,))],
    )
    return pl.pallas_call(
        functools.partial(_final_kernel, alpha=alpha, tm=tm, t=t, nt=nt),
        grid_spec=grid_spec,
        out_shape=jax.ShapeDtypeStruct((t, D_MODEL), F32),
        compiler_params=pltpu.CompilerParams(dimension_semantics=("arbitrary",),
                                             vmem_limit_bytes=VMEM_LIMIT,
                                             disable_bounds_checks=True),
        name="moe_combine_ln3",
    )(row_of_asg, x2, y_sorted, route, ln_g, ln_b)


def _rotary_tables(seq):
    half = HEAD_DIM // 2
    inv_freq = ROPE_THETA ** (-jnp.arange(half, dtype=F32) / half)
    ang = jnp.arange(seq, dtype=F32)[:, None] * inv_freq[None, :]
    cos, sin = jnp.cos(ang), jnp.sin(ang)
    return jnp.concatenate([cos, cos], axis=-1), jnp.concatenate([-sin, sin], axis=-1)


def _prepare_weights(l, w_in, ret_gn_g, w_out, ln1_g, ln1_b, w_mq, w_mkv, w_mo, ln2_g, ln2_b,
                     w_gr, b_gr, w_er, b_er, w1, w3, w2, ln3_g, ln3_b):
    row = lambda v: v[l].reshape(1, -1).astype(F32)
    pad = LANES - N_GROUPS - N_EXPERTS
    w_r = jnp.concatenate([w_gr[l], w_er[l], jnp.zeros((D_MODEL, pad), F32)], axis=1)
    wr_hi = w_r.astype(BF16)
    wr_lo = (w_r - wr_hi.astype(F32)).astype(BF16)
    b_r = jnp.concatenate([b_gr[l], b_er[l], jnp.zeros((pad,), F32)]).reshape(1, LANES)
    return dict(
        w_in=w_in[l].astype(BF16), gn_g=row(ret_gn_g), w_out=w_out[l].astype(BF16),
        ln1=(row(ln1_g), row(ln1_b)), w_mq=w_mq[l].astype(BF16), w_mkv=w_mkv[l].astype(BF16),
        w_mo=w_mo[l].astype(BF16), ln2=(row(ln2_g), row(ln2_b)),
        w_r=jnp.concatenate([wr_hi, wr_lo], axis=1), b_r=b_r,
        w1=w1[l].astype(BF16), w3=w3[l].astype(BF16), w2=w2[l].astype(BF16),
        ln3=(row(ln3_g), row(ln3_b)))


def _encoder_layer(x, mem, w, alpha, ret_tables):
    batch, seq, _ = x.shape
    t = batch * seq
    x2d = x.reshape(t, D_MODEL)
    cos, sin = _rotary_tables(seq)
    h, ha4, ha16 = _in_proj(x2d, w['w_in'], cos, sin, batch, seq)
    outs, lses = [], []
    for dil, qkv in zip(DILATIONS, (h.reshape(N_SEG, batch, 1, seq, SEG), ha4, ha16)):
        o, lse = _local_attn(qkv, batch, seq, dil)
        outs.append(o)
        lses.append(lse)
    ret_o = _retention(h.reshape(N_SEG, batch, seq, SEG), w['gn_g'], batch, seq, ret_tables)
    x1 = _out_proj(outs, lses, ret_o.reshape(t, SEG), x2d, w['w_out'], *w['ln1'], seq, alpha)
    kv = _mem_kv(mem.reshape(batch * MEM_TOKENS, D_MODEL), w['w_mkv'])
    x2, route = _xattn(x1, kv, w['w_mq'], w['w_mo'], *w['ln2'], w['w_r'], w['b_r'], seq, alpha)
    y_sorted, row_of_asg = _moe(x2, route, w['w1'], w['w3'], w['w2'])
    out = _final(x2, y_sorted, row_of_asg, route, *w['ln3'], alpha)
    return out.reshape(batch, seq, D_MODEL)


def kernel(x_prompt, x_sample, mem_prompt, mem_sample, w_in, ret_gn_g, w_out, ln1_g, ln1_b,
           w_mq, w_mkv, w_mo, ln2_g, ln2_b, w_gr, b_gr, w_er, b_er, w1, w3, w2, ln3_g, ln3_b):
    depth = w_in.shape[0]
    alpha = (2 * depth) ** 0.25
    ret_tables = _retention_tables()
    y_prompt, y_sample = x_prompt, x_sample
    for l in range(depth):
        w = _prepare_weights(l, w_in, ret_gn_g, w_out, ln1_g, ln1_b, w_mq, w_mkv, w_mo, ln2_g,
                             ln2_b, w_gr, b_gr, w_er, b_er, w1, w3, w2, ln3_g, ln3_b)
        y_prompt = _encoder_layer(y_prompt, mem_prompt, w, alpha, ret_tables)
        y_sample = _encoder_layer(y_sample, mem_sample, w, alpha, ret_tables)
    return (y_prompt, y_sample)
```
